```python
import jax, jax.numpy as jnp
from jax import lax
import numpy as np

D_MODEL = 1024
BATCH = 8
SEQ = 2048
DEPTH = 4
DEC_BATCH = 128
DEC_SEQ = 4
PAST_LEN = 16384
PAGE_SIZE = 128

N_MIXERS = 2
N_CONV = (DEPTH + 1) // 2
N_RWKV = DEPTH // 2
CONV_WIDTH = 3
HEAD_SIZE = 64
N_HEADS = D_MODEL // HEAD_SIZE
D_DECAY_LORA = 64
D_AAA_LORA = 64
D_MV_LORA = 32
D_GATE_LORA = 128
N_KEYS = 128
N_EXPERTS = N_KEYS * N_KEYS
PEER_HEADS = 8
PEER_TOPK = 16
D_QUERY = 256
PEER_CHUNK = 256
RMS_EPS = 1e-6
GN_EPS = 64e-5

kernel_name = "hybrid_conv_rwkv7_peer_adaln_step"


def rmsnorm(x, g):
    x32 = x.astype(jnp.float32)
    y = x32 * lax.rsqrt(jnp.mean(x32 * x32, axis=-1, keepdims=True) + RMS_EPS)
    return y.astype(x.dtype) * g


def modulate(h, shift, scale):
    return h * (1 + scale[:, None, :]) + shift[:, None, :]


def conv_mixer(h, prev, w_in, conv_k, w_out):
    T = h.shape[1]
    bg, cg, xp = jnp.split(h @ w_in, 3, axis=-1)
    u = cg * xp
    up = jnp.concatenate([prev.astype(u.dtype), u], axis=1)
    conv = up[:, 0:T] * conv_k[0]
    for j in range(1, CONV_WIDTH):
        conv = conv + up[:, j:j + T] * conv_k[j]
    y = (bg * conv) @ w_out
    return y, up[:, -(CONV_WIDTH - 1):]


def wkv_scan(r, w, k, v, a, b, s0):
    def step(S, inp):
        rt, wt, kt, vt, at, bt = inp
        sa = jnp.einsum('bhij,bhj->bhi', S, at)
        S = S * wt[:, :, None, :] + sa[..., None] * bt[:, :, None, :] + vt[..., None] * kt[:, :, None, :]
        return S, jnp.einsum('bhij,bhj->bhi', S, rt)
    xs = tuple(jnp.moveaxis(t, 1, 0) for t in (r, w, k, v, a, b))
    S, ys = lax.scan(step, s0, xs)
    return jnp.moveaxis(ys, 0, 1), S


def rwkv_mixer(h, prev, s0, mix, wr, wk, wv, wo, w0, w1, w2, a0, a1, a2, g1, g2,
               k_k, k_a, r_k, lnw, lnb, v_first, vres):
    B, T, D = h.shape
    hs = jnp.concatenate([prev[:, None, :].astype(h.dtype), h[:, :-1]], axis=1)
    xx = hs - h
    xr, xw, xk, xv, xa, xg = [h + xx * mix[i] for i in range(6)]
    r = xr @ wr
    wlog = -jax.nn.softplus(-(w0 + jnp.tanh(xw @ w1) @ w2)) - 0.5
    k = xk @ wk
    v = xv @ wv
    if vres is None:
        v_first = v
    else:
        v0, v1, v2 = vres
        v = v + (v_first - v) * jax.nn.sigmoid(v0 + (xv @ v1) @ v2)
    a = jax.nn.sigmoid(a0 + (xa @ a1) @ a2)
    g = jax.nn.sigmoid(xg @ g1) @ g2
    heads = lambda t: t.reshape(B, T, N_HEADS, HEAD_SIZE).astype(jnp.float32)
    kk = heads(k * k_k)
    kk = kk * lax.rsqrt(jnp.maximum(jnp.sum(kk * kk, axis=-1, keepdims=True), 1e-24))
    k = k * (1 + (a - 1) * k_a)
    r_h, k_h, v_h, a_h = heads(r), heads(k), heads(v), heads(a)
    decay = jnp.exp(-jnp.exp(heads(wlog)))
    y, S = wkv_scan(r_h, decay, k_h, v_h, -kk, kk * a_h, s0.astype(jnp.float32))
    mu = jnp.mean(y, axis=-1, keepdims=True)
    var = jnp.mean(jnp.square(y - mu), axis=-1, keepdims=True)
    y = ((y - mu) * lax.rsqrt(var + GN_EPS)).reshape(B, T, D) * lnw + lnb
    bonus = jnp.sum(r_h * k_h * r_k, axis=-1, keepdims=True) * v_h
    y = y + bonus.reshape(B, T, D)
    out = (y.astype(h.dtype) * g) @ wo
    return out, h[:, -1], S, v_first


def peer_ffn(h, wq, k1, k2, u_tab, v_tab):
    B, T, D = h.shape
    n = B * T
    n_pad = -(-n // PEER_CHUNK) * PEER_CHUNK
    flat = jnp.pad(h.reshape(n, D), ((0, n_pad - n), (0, 0)))
    half = D_QUERY // 2

    def chunk(hc):
        q = (hc @ wq).reshape(-1, PEER_HEADS, D_QUERY)
        s1 = jnp.einsum('chd,hkd->chk', q[..., :half], k1).astype(jnp.float32)
        s2 = jnp.einsum('chd,hkd->chk', q[..., half:], k2).astype(jnp.float32)
        v1, i1 = lax.top_k(s1, PEER_TOPK)
        v2, i2 = lax.top_k(s2, PEER_TOPK)
        cand = (v1[..., :, None] + v2[..., None, :]).reshape(-1, PEER_HEADS, PEER_TOPK * PEER_TOPK)
        sc, ci = lax.top_k(cand, PEER_TOPK)
        e1 = jnp.take_along_axis(i1, ci // PEER_TOPK, axis=-1)
        e2 = jnp.take_along_axis(i2, ci % PEER_TOPK, axis=-1)
        eidx = e1 * N_KEYS + e2
        gate = jax.nn.softmax(sc, axis=-1)
        z = jax.nn.gelu(jnp.einsum('cd,chkd->chk', hc, u_tab[eidx]), approximate=False)
        coef = (gate * z.astype(jnp.float32)).astype(hc.dtype)
        return jnp.einsum('chk,chkd->cd', coef, v_tab[eidx])

    out = lax.map(chunk, flat.reshape(-1, PEER_CHUNK, D))
    return out.reshape(n_pad, D)[:n].reshape(B, T, D)


def setup_inputs(seed: int = 0) -> dict:
    key = jax.random.key(seed)
    ks = iter(jax.random.split(key, 48))
    D = D_MODEL
    inv = D ** -0.5
    nrm = lambda shape, s: jax.random.normal(next(ks), shape, jnp.float32) * s
    uni = lambda shape, lo, hi: jax.random.uniform(next(ks), shape, jnp.float32, lo, hi)
    return {
        "x_prompt": nrm((BATCH, SEQ, D), 1.0),
        "x_sample": nrm((DEC_BATCH, DEC_SEQ, D), 1.0),
        "state_conv": nrm((N_CONV, DEC_BATCH, CONV_WIDTH - 1, D), 0.5),
        "state_shift": nrm((N_RWKV, DEC_BATCH, D), 1.0),
        "state_wkv": nrm((N_RWKV, DEC_BATCH, N_HEADS, HEAD_SIZE, HEAD_SIZE), 0.1),
        "c_prompt": nrm((BATCH, D), 1.0),
        "c_sample": nrm((DEC_BATCH, D), 1.0),
        "w_ada": nrm((DEPTH, D, 6 * D), 0.5 * inv),
        "b_ada": nrm((DEPTH, 6 * D), 0.02),
        "g_mix": 1.0 + nrm((DEPTH, D), 0.02),
        "g_ffn": 1.0 + nrm((DEPTH, D), 0.02),
        "g_final": 1.0 + nrm((D,), 0.02),
        "conv_w_in": nrm((N_CONV, D, 3 * D), inv),
        "conv_k": nrm((N_CONV, CONV_WIDTH, D), CONV_WIDTH ** -0.5),
        "conv_w_out": nrm((N_CONV, D, D), inv),
        "rwkv_mix": uni((N_RWKV, 6, D), 0.0, 1.0),
        "rwkv_wr": nrm((N_RWKV, D, D), inv),
        "rwkv_wk": nrm((N_RWKV, D, D), inv),
        "rwkv_wv": nrm((N_RWKV, D, D), inv),
        "rwkv_wo": nrm((N_RWKV, D, D), inv),
        "rwkv_w0": uni((N_RWKV, D), -6.0, 1.0),
        "rwkv_w1": nrm((N_RWKV, D, D_DECAY_LORA), inv),
        "rwkv_w2": nrm((N_RWKV, D_DECAY_LORA, D), 0.5 * D_DECAY_LORA ** -0.5),
        "rwkv_a0": nrm((N_RWKV, D), 0.1),
        "rwkv_a1": nrm((N_RWKV, D, D_AAA_LORA), inv),
        "rwkv_a2": nrm((N_RWKV, D_AAA_LORA, D), 0.5 * D_AAA_LORA ** -0.5),
        "rwkv_v0": nrm((N_RWKV - 1, D), 0.1),
        "rwkv_v1": nrm((N_RWKV - 1, D, D_MV_LORA), inv),
        "rwkv_v2": nrm((N_RWKV - 1, D_MV_LORA, D), 0.5 * D_MV_LORA ** -0.5),
        "rwkv_g1": nrm((N_RWKV, D, D_GATE_LORA), inv),
        "rwkv_g2": nrm((N_RWKV, D_GATE_LORA, D), D_GATE_LORA ** -0.5),
        "rwkv_k_k": 0.85 + nrm((N_RWKV, D), 0.02),
        "rwkv_k_a": 1.0 + nrm((N_RWKV, D), 0.02),
        "rwkv_r_k": nrm((N_RWKV, N_HEADS, HEAD_SIZE), 0.1),
        "rwkv_lnw": 1.0 + nrm((N_RWKV, D), 0.02),
        "rwkv_lnb": nrm((N_RWKV, D), 0.02),
        "peer_wq": nrm((DEPTH, D, PEER_HEADS * D_QUERY), inv),
        "peer_k1": nrm((DEPTH, PEER_HEADS, N_KEYS, D_QUERY // 2), (D_QUERY // 2) ** -0.5),
        "peer_k2": nrm((DEPTH, PEER_HEADS, N_KEYS, D_QUERY // 2), (D_QUERY // 2) ** -0.5),
        "peer_u": nrm((DEPTH, N_EXPERTS, D), inv),
        "peer_v": nrm((DEPTH, N_EXPERTS, D), 0.5),
    }


def reference(x_prompt, x_sample, state_conv, state_shift, state_wkv, c_prompt, c_sample,
              w_ada, b_ada, g_mix, g_ffn, g_final, conv_w_in, conv_k, conv_w_out,
              rwkv_mix, rwkv_wr, rwkv_wk, rwkv_wv, rwkv_wo, rwkv_w0, rwkv_w1, rwkv_w2,
              rwkv_a0, rwkv_a1, rwkv_a2, rwkv_v0, rwkv_v1, rwkv_v2, rwkv_g1, rwkv_g2,
              rwkv_k_k, rwkv_k_a, rwkv_r_k, rwkv_lnw, rwkv_lnb,
              peer_wq, peer_k1, peer_k2, peer_u, peer_v):
    def run(x, c, st_conv, st_shift, st_wkv):
        cs = jax.nn.silu(c)
        new_conv, new_shift, new_wkv = [], [], []
        v_first = None
        for i in range(DEPTH):
            m = cs @ w_ada[i] + b_ada[i]
            sh_m, sc_m, gt_m, sh_f, sc_f, gt_f = jnp.split(m, 6, axis=-1)
            h = modulate(rmsnorm(x, g_mix[i]), sh_m, sc_m)
            j = i // N_MIXERS
            if i % N_MIXERS == 0:
                y, cbuf = conv_mixer(h, st_conv[j], conv_w_in[j], conv_k[j], conv_w_out[j])
                new_conv.append(cbuf)
            else:
                vres = None if j == 0 else (rwkv_v0[j - 1], rwkv_v1[j - 1], rwkv_v2[j - 1])
                y, last, S, v_first = rwkv_mixer(
                    h, st_shift[j], st_wkv[j], rwkv_mix[j], rwkv_wr[j], rwkv_wk[j], rwkv_wv[j], rwkv_wo[j],
                    rwkv_w0[j], rwkv_w1[j], rwkv_w2[j], rwkv_a0[j], rwkv_a1[j], rwkv_a2[j],
                    rwkv_g1[j], rwkv_g2[j], rwkv_k_k[j], rwkv_k_a[j], rwkv_r_k[j],
                    rwkv_lnw[j], rwkv_lnb[j], v_first, vres)
                new_shift.append(last)
                new_wkv.append(S)
            x = x + gt_m[:, None, :] * y
            h = modulate(rmsnorm(x, g_ffn[i]), sh_f, sc_f)
            x = x + gt_f[:, None, :] * peer_ffn(h, peer_wq[i], peer_k1[i], peer_k2[i], peer_u[i], peer_v[i])
        return rmsnorm(x, g_final), jnp.stack(new_conv), jnp.stack(new_shift), jnp.stack(new_wkv)

    D = D_MODEL
    p_conv0 = jnp.zeros((N_CONV, BATCH, CONV_WIDTH - 1, D), x_prompt.dtype)
    p_shift0 = jnp.zeros((N_RWKV, BATCH, D), x_prompt.dtype)
    p_wkv0 = jnp.zeros((N_RWKV, BATCH, N_HEADS, HEAD_SIZE, HEAD_SIZE), jnp.float32)
    y_prompt, p_conv, p_shift, p_wkv = run(x_prompt, c_prompt, p_conv0, p_shift0, p_wkv0)
    y_sample, s_conv, s_shift, s_wkv = run(x_sample, c_sample, state_conv, state_shift, state_wkv)
    return (y_prompt, y_sample, p_conv, p_shift, p_wkv, s_conv, s_shift, s_wkv)
```

```python
import functools
import math

import jax
import jax.numpy as jnp
from jax import lax
from jax.experimental import pallas as pl
from jax.experimental.pallas import tpu as pltpu

D_MODEL = 1024
DEPTH = 4
CONV_WIDTH = 3
HEAD_SIZE = 64
N_HEADS = D_MODEL // HEAD_SIZE
N_KEYS = 128
N_EXPERTS = N_KEYS * N_KEYS
PEER_HEADS = 8
PEER_TOPK = 16
D_QUERY = 256
RMS_EPS = 1e-6
GN_EPS = 64e-5

LORA_PAD = 128
TOKEN_TILE = 512
RWKV_PROJ_TILE = 256
PEER_EXPERT_TILE = 512
PEER_LANE_CHUNK = 128
WKV_CHUNK = 64
WKV_SHORT_CHUNK = 8
VMEM_LIMIT = 56 * 1024 * 1024

F32 = jnp.float32
BF16 = jnp.bfloat16


def _params(*sem):
    return pltpu.CompilerParams(dimension_semantics=sem, vmem_limit_bytes=VMEM_LIMIT)


def _sigmoid(x):
    return 1.0 / (1.0 + jnp.exp(-x))


def _rms_mod(x, g, sc, sh):
    y = x * lax.rsqrt(jnp.mean(x * x, axis=-1, keepdims=True) + RMS_EPS)
    return (y * g) * (1.0 + sc) + sh


def _bdot(a, b):
    return jnp.dot(a.astype(BF16), b.astype(BF16), preferred_element_type=F32)


def _fdot(a, b):
    return jnp.dot(a, b, preferred_element_type=F32)


def _fdot_nt(a, b):
    return lax.dot_general(a, b, (((1,), (1,)), ((), ())), preferred_element_type=F32)


def _fdot_tn(a, b):
    return lax.dot_general(a, b, (((0,), (0,)), ((), ())), preferred_element_type=F32)


class _Layout:
    def __init__(self, batch, seq, tile=TOKEN_TILE):
        self.batch, self.seq = batch, seq
        self.n = batch * seq
        self.tile = min(tile, self.n)
        assert self.n % self.tile == 0
        self.n_tiles = self.n // self.tile
        if seq % self.tile == 0:
            self.rows = 1
            self.tiles_per_group = seq // self.tile
        else:
            assert self.tile % seq == 0
            self.rows = self.tile
            self.tiles_per_group = 1

    def expand(self, m):
        if self.rows == 1:
            return m[:, None, :]
        return jnp.repeat(m, self.seq, axis=0).reshape(self.n_tiles, self.tile, m.shape[-1])

    def mod_spec(self):
        tpg = self.tiles_per_group
        return pl.BlockSpec((1, self.rows, D_MODEL), lambda i: (i // tpg, 0, 0))

    def tok_spec(self, width=D_MODEL):
        return pl.BlockSpec((self.tile, width), lambda i: (i, 0))


def _full_spec(shape):
    nd = len(shape)
    return pl.BlockSpec(shape, lambda *_: (0,) * nd)


def _adaln_kernel(c_ref, w_ref, b_ref, o_ref):
    c = c_ref[...]
    cs = c * _sigmoid(c)
    o_ref[0] = _fdot(cs, w_ref[0]) + b_ref[0]


def _adaln(c, w_ada, b_ada):
    nb = c.shape[0]
    tn = 1536
    return pl.pallas_call(
        _adaln_kernel,
        grid=(DEPTH, 6 * D_MODEL // tn),
        in_specs=[
            pl.BlockSpec((nb, D_MODEL), lambda l, j: (0, 0)),
            pl.BlockSpec((1, D_MODEL, tn), lambda l, j: (l, 0, j)),
            pl.BlockSpec((1, 1, tn), lambda l, j: (l, 0, j)),
        ],
        out_specs=pl.BlockSpec((1, nb, tn), lambda l, j: (l, 0, j)),
        out_shape=jax.ShapeDtypeStruct((DEPTH, nb, 6 * D_MODEL), F32),
        compiler_params=_params("arbitrary", "arbitrary"),
        name="adaln",
    )(c, w_ada, b_ada.reshape(DEPTH, 1, 6 * D_MODEL))


def _norm_mod_kernel(x_ref, g_ref, sc_ref, sh_ref, o_ref):
    o_ref[...] = _rms_mod(x_ref[...], g_ref[...], sc_ref[0], sh_ref[0])


def _norm_mod(lay, x, g, sc, sh):
    return pl.pallas_call(
        _norm_mod_kernel,
        grid=(lay.n_tiles,),
        in_specs=[lay.tok_spec(), _full_spec((1, D_MODEL)), lay.mod_spec(), lay.mod_spec()],
        out_specs=lay.tok_spec(),
        out_shape=jax.ShapeDtypeStruct((lay.n, D_MODEL), F32),
        compiler_params=_params("arbitrary"),
        name="norm_mod",
    )(x, g.reshape(1, D_MODEL), sc, sh)


def _final_norm_kernel(x_ref, g_ref, o_ref):
    x = x_ref[...]
    o_ref[...] = x * lax.rsqrt(jnp.mean(x * x, axis=-1, keepdims=True) + RMS_EPS) * g_ref[...]


def _final_norm(lay, x, g):
    return pl.pallas_call(
        _final_norm_kernel,
        grid=(lay.n_tiles,),
        in_specs=[lay.tok_spec(), _full_spec((1, D_MODEL))],
        out_specs=lay.tok_spec(),
        out_shape=jax.ShapeDtypeStruct((lay.n, D_MODEL), F32),
        compiler_params=_params("arbitrary"),
        name="final_norm",
    )(x, g.reshape(1, D_MODEL))


def _conv_in_kernel(x_ref, g_ref, sc_ref, sh_ref, w_ref, bg_ref, u_ref):
    h = _rms_mod(x_ref[...], g_ref[...], sc_ref[0], sh_ref[0]).astype(BF16)
    d = D_MODEL
    bg_ref[...] = _fdot(h, w_ref[:, 0:d])
    u_ref[...] = _fdot(h, w_ref[:, d:2 * d]) * _fdot(h, w_ref[:, 2 * d:3 * d])


def _conv_in(lay, x, g, sc, sh, w_in):
    shp = jax.ShapeDtypeStruct((lay.n, D_MODEL), F32)
    return pl.pallas_call(
        _conv_in_kernel,
        grid=(lay.n_tiles,),
        in_specs=[lay.tok_spec(), _full_spec((1, D_MODEL)), lay.mod_spec(), lay.mod_spec(),
                  _full_spec((D_MODEL, 3 * D_MODEL))],
        out_specs=[lay.tok_spec(), lay.tok_spec()],
        out_shape=[shp, shp],
        compiler_params=_params("arbitrary"),
        name="conv_in",
    )(x, g.reshape(1, D_MODEL), sc, sh, w_in.astype(BF16))


def _conv_out_kernel(bg_ref, u0_ref, u1_ref, u2_ref, ck_ref, w_ref, x_ref, gt_ref, o_ref):
    conv = u0_ref[...] * ck_ref[0:1, :] + u1_ref[...] * ck_ref[1:2, :] + u2_ref[...] * ck_ref[2:3, :]
    y = _bdot(bg_ref[...] * conv, w_ref[...])
    o_ref[...] = x_ref[...] + gt_ref[0] * y


def _conv_out(lay, bg, u0, u1, u2, conv_k, w_out, x, gt):
    return pl.pallas_call(
        _conv_out_kernel,
        grid=(lay.n_tiles,),
        in_specs=[lay.tok_spec()] * 4 + [_full_spec((CONV_WIDTH, D_MODEL)), _full_spec((D_MODEL, D_MODEL)),
                                         lay.tok_spec(), lay.mod_spec()],
        out_specs=lay.tok_spec(),
        out_shape=jax.ShapeDtypeStruct((lay.n, D_MODEL), F32),
        compiler_params=_params("arbitrary"),
        name="conv_out",
    )(bg, u0, u1, u2, conv_k, w_out.astype(BF16), x, gt)


def _mm_res_kernel(a_ref, w_ref, x_ref, gt_ref, o_ref):
    o_ref[...] = x_ref[...] + gt_ref[0] * _bdot(a_ref[...], w_ref[...])


def _mm_res(lay, a, w, x, gt):
    return pl.pallas_call(
        _mm_res_kernel,
        grid=(lay.n_tiles,),
        in_specs=[lay.tok_spec(), _full_spec((D_MODEL, D_MODEL)), lay.tok_spec(), lay.mod_spec()],
        out_specs=lay.tok_spec(),
        out_shape=jax.ShapeDtypeStruct((lay.n, D_MODEL), F32),
        compiler_params=_params("arbitrary"),
        name="mm_res",
    )(a, w.astype(BF16), x, gt)


def _rwkv_proj_kernel(*refs, has_vres):
    if has_vres:
        (h_ref, hs_ref, vf_ref, mix_ref, vec_ref, wr_ref, wk_ref, wv_ref, w1_ref, w2_ref, a1_ref, a2_ref,
         g1_ref, g2_ref, v1_ref, v2_ref, r_ref, km_ref, kk_ref, v_ref, lw_ref, a_ref, g_ref) = refs
    else:
        (h_ref, hs_ref, mix_ref, vec_ref, wr_ref, wk_ref, wv_ref, w1_ref, w2_ref, a1_ref, a2_ref,
         g1_ref, g2_ref, r_ref, km_ref, kk_ref, v_ref, lw_ref, a_ref, g_ref) = refs
    h = h_ref[...]
    xx = hs_ref[...] - h
    mixed = lambda i: (h + xx * mix_ref[i:i + 1, :]).astype(BF16)
    w0, a0, v0, k_k, k_a = (vec_ref[i:i + 1, :] for i in range(5))

    r_ref[...] = _fdot(mixed(0), wr_ref[...])
    wl = w0 + _bdot(jnp.tanh(_fdot(mixed(1), w1_ref[...])), w2_ref[...])
    z = -wl
    wlog = -(jnp.maximum(z, 0.0) + jnp.log(1.0 + jnp.exp(-jnp.abs(z)))) - 0.5
    lw_ref[...] = -jnp.exp(wlog)
    k = _fdot(mixed(2), wk_ref[...])
    xv = mixed(3)
    v = _fdot(xv, wv_ref[...])
    if has_vres:
        v = v + (vf_ref[...] - v) * _sigmoid(v0 + _bdot(_fdot(xv, v1_ref[...]), v2_ref[...]))
    v_ref[...] = v
    a = _sigmoid(a0 + _bdot(_fdot(mixed(4), a1_ref[...]), a2_ref[...]))
    a_ref[...] = a
    g_ref[...] = _bdot(_sigmoid(_fdot(mixed(5), g1_ref[...])), g2_ref[...])
    kk_ref[...] = k * k_k
    km_ref[...] = k * (1.0 + (a - 1.0) * k_a)


def _pad_cols(w):
    return jnp.pad(w, ((0, 0), (0, LORA_PAD - w.shape[1]))).astype(BF16)


def _pad_rows(w):
    return jnp.pad(w, ((0, LORA_PAD - w.shape[0]), (0, 0))).astype(BF16)


def _rwkv_proj(lay, h, hs, v_first, mix, vecs, wr, wk, wv, w1, w2, a1, a2, g1, g2, v1, v2):
    has_vres = v_first is not None
    sq = _full_spec((D_MODEL, D_MODEL))
    down = _full_spec((D_MODEL, LORA_PAD))
    up = _full_spec((LORA_PAD, D_MODEL))
    in_specs = [lay.tok_spec(), lay.tok_spec()] + ([lay.tok_spec()] if has_vres else [])
    in_specs += [_full_spec((6, D_MODEL)), _full_spec((8, D_MODEL)), sq, sq, sq, down, up, down, up, down, up]
    args = [h, hs] + ([v_first] if has_vres else [])
    args += [mix, vecs, wr.astype(BF16), wk.astype(BF16), wv.astype(BF16), _pad_cols(w1), _pad_rows(w2),
             _pad_cols(a1), _pad_rows(a2), _pad_cols(g1), _pad_rows(g2)]
    if has_vres:
        in_specs += [down, up]
        args += [_pad_cols(v1), _pad_rows(v2)]
    shp = jax.ShapeDtypeStruct((lay.n, D_MODEL), F32)
    return pl.pallas_call(
        functools.partial(_rwkv_proj_kernel, has_vres=has_vres),
        grid=(lay.n_tiles,),
        in_specs=in_specs,
        out_specs=[lay.tok_spec()] * 7,
        out_shape=[shp] * 7,
        compiler_params=_params("arbitrary"),
        name="rwkv_proj",
    )(*args)


def _wkv_kernel(r_ref, km_ref, kk_ref, v_ref, lw_ref, a_ref, g_ref, rk_ref, lnw_ref, lnb_ref, s0_ref,
                y_ref, s_ref, st_ref, *, chunk, heads):
    t = pl.program_id(2)
    L = chunk

    @pl.when(t == 0)
    def _():
        st_ref[...] = s0_ref[0]

    row = lax.broadcasted_iota(jnp.int32, (L, L), 0)
    col = lax.broadcasted_iota(jnp.int32, (L, L), 1)
    strict = row > col
    incl = row >= col
    tri = incl.astype(F32)

    outs = []
    for hd in range(heads):
        sl = slice(hd * HEAD_SIZE, (hd + 1) * HEAD_SIZE)
        r, km, kkr, v, lw, asig, g = (ref[:, sl] for ref in (r_ref, km_ref, kk_ref, v_ref, lw_ref, a_ref, g_ref))
        s0 = st_ref[hd]
        kk = kkr * lax.rsqrt(jnp.maximum(jnp.sum(kkr * kkr, axis=-1, keepdims=True), 1e-24))
        a = -kk
        b = kk * asig
        c = _fdot(tri, lw)
        c_last = c[L - 1:L, :]
        gam = jnp.exp(c)
        ginv = jnp.exp(-c)
        gtail = jnp.exp(c_last - c)
        lhs = jnp.concatenate([a * jnp.exp(c - lw), r * gam], axis=0)
        rhs = jnp.concatenate([b * ginv, km * ginv], axis=0)
        qk = _fdot_nt(lhs, rhs)
        m_ab = jnp.where(strict, qk[:L, :L], 0.0)
        m_ak = jnp.where(strict, qk[:L, L:], 0.0)
        a_rb = jnp.where(incl, qk[L:, :L], 0.0)
        a_rk = jnp.where(incl, qk[L:, L:], 0.0)
        hs0 = _fdot_nt(lhs, s0)
        w_ = hs0[:L] + _fdot(m_ak, v)
        u = w_ + _fdot(m_ab, w_)
        p = m_ab
        for _ in range(int(math.log2(L)) - 1):
            p = _fdot(p, p)
            u = u + _fdot(p, u)
        y = hs0[L:] + _fdot(a_rb, u) + _fdot(a_rk, v)
        st_ref[hd] = s0 * jnp.exp(c_last) + _fdot_tn(u, b * gtail) + _fdot_tn(v, km * gtail)

        mu = jnp.mean(y, axis=-1, keepdims=True)
        yc = y - mu
        var = jnp.mean(yc * yc, axis=-1, keepdims=True)
        yn = yc * lax.rsqrt(var + GN_EPS) * lnw_ref[:, sl] + lnb_ref[:, sl]
        bonus = jnp.sum(r * km * rk_ref[:, sl], axis=-1, keepdims=True) * v
        outs.append((yn + bonus) * g)
    y_ref[...] = jnp.concatenate(outs, axis=1)

    @pl.when(t == pl.num_programs(2) - 1)
    def _():
        s_ref[0] = st_ref[...]


def _wkv(batch, seq, chunk, heads, r, km, kk, v, lw, a, g, r_k, lnw, lnb, s0):
    n_t = seq // chunk
    width = heads * HEAD_SIZE
    tok = pl.BlockSpec((chunk, width), lambda b, h, t: (b * n_t + t, h))
    vec = pl.BlockSpec((1, width), lambda b, h, t: (0, h))
    st = pl.BlockSpec((1, heads, HEAD_SIZE, HEAD_SIZE), lambda b, h, t: (b, h, 0, 0))
    return pl.pallas_call(
        functools.partial(_wkv_kernel, chunk=chunk, heads=heads),
        grid=(batch, N_HEADS // heads, n_t),
        in_specs=[tok] * 7 + [vec] * 3 + [st],
        out_specs=[tok, st],
        out_shape=[jax.ShapeDtypeStruct((batch * seq, D_MODEL), F32),
                   jax.ShapeDtypeStruct((batch, N_HEADS, HEAD_SIZE, HEAD_SIZE), F32)],
        scratch_shapes=[pltpu.VMEM((heads, HEAD_SIZE, HEAD_SIZE), F32)],
        compiler_params=_params("arbitrary", "arbitrary", "arbitrary"),
        name="wkv",
    )(r, km, kk, v, lw, a, g, r_k.reshape(1, D_MODEL), lnw.reshape(1, D_MODEL), lnb.reshape(1, D_MODEL), s0)


def _top16(s):
    work = s
    rank = jnp.full(s.shape, float(PEER_TOPK), F32)
    vals = []
    for k in range(PEER_TOPK):
        m = jnp.max(work, axis=0, keepdims=True)
        hit = work == m
        rank = jnp.where(hit, float(k), rank)
        work = jnp.where(hit, -jnp.inf, work)
        vals.append(m)
    return vals, rank


def _peer_select_chunk(s1, s2):
    v1, rank1 = _top16(s1)
    v2, rank2 = _top16(s2)
    v2s = jnp.concatenate(v2, axis=0)
    cand = jnp.concatenate([v1[a] + v2s for a in range(PEER_TOPK)], axis=0)
    work = cand
    tau = None
    for _ in range(PEER_TOPK):
        tau = jnp.max(work, axis=0, keepdims=True)
        work = jnp.where(work == tau, -jnp.inf, work)
    top = v1[0] + v2[0]
    zsum = jnp.sum(jnp.where(cand >= tau, jnp.exp(cand - top), 0.0), axis=0, keepdims=True)
    cnt = jnp.zeros(s1.shape, F32)
    for b in range(PEER_TOPK):
        cnt = cnt + jnp.where(s1 + v2[b] >= tau, 1.0, 0.0)
    cnt = jnp.where(rank1 < float(PEER_TOPK), cnt, 0.0)
    e1 = jnp.exp(s1 - v1[0])
    e2 = jnp.exp(s2 - v2[0]) / zsum
    return cnt, e1, rank2, e2


def _peer_select_kernel(x_ref, g_ref, sc_ref, sh_ref, wq_ref, k1_ref, k2_ref,
                        ht_ref, c_ref, e1_ref, rk_ref, e2_ref, q_scr):
    h = _rms_mod(x_ref[...], g_ref[...], sc_ref[0], sh_ref[0])
    ht = h.T.astype(BF16)
    ht_ref[...] = ht
    q_scr[...] = _fdot(wq_ref[...], ht)
    half = D_QUERY // 2
    n_chunks = x_ref.shape[0] // PEER_LANE_CHUNK

    def head_body(hd, carry):
        q1 = q_scr[pl.ds(pl.multiple_of(hd * D_QUERY, D_QUERY), half), :]
        q2 = q_scr[pl.ds(pl.multiple_of(hd * D_QUERY + half, half), half), :]
        s1 = _bdot(k1_ref[hd], q1)
        s2 = _bdot(k2_ref[hd], q2)
        for ch in range(n_chunks):
            ls = slice(ch * PEER_LANE_CHUNK, (ch + 1) * PEER_LANE_CHUNK)
            cnt, e1, rank2, e2 = _peer_select_chunk(s1[:, ls], s2[:, ls])
            c_ref[hd, :, ls] = cnt
            e1_ref[hd, :, ls] = e1
            rk_ref[hd, :, ls] = rank2
            e2_ref[hd, :, ls] = e2
        return carry

    lax.fori_loop(0, PEER_HEADS, head_body, 0)


def _peer_select(lay, x, g, sc, sh, wq_t, k1, k2):
    tt = lay.tile
    sel_spec = pl.BlockSpec((PEER_HEADS, N_KEYS, tt), lambda i: (0, 0, i))
    sel_shape = jax.ShapeDtypeStruct((PEER_HEADS, N_KEYS, lay.n), F32)
    return pl.pallas_call(
        _peer_select_kernel,
        grid=(lay.n_tiles,),
        in_specs=[lay.tok_spec(), _full_spec((1, D_MODEL)), lay.mod_spec(), lay.mod_spec(),
                  _full_spec((PEER_HEADS * D_QUERY, D_MODEL)),
                  _full_spec((PEER_HEADS, N_KEYS, D_QUERY // 2)), _full_spec((PEER_HEADS, N_KEYS, D_QUERY // 2))],
        out_specs=[pl.BlockSpec((D_MODEL, tt), lambda i: (0, i))] + [sel_spec] * 4,
        out_shape=[jax.ShapeDtypeStruct((D_MODEL, lay.n), BF16)] + [sel_shape] * 4,
        scratch_shapes=[pltpu.VMEM((PEER_HEADS * D_QUERY, tt), F32)],
        compiler_params=_params("arbitrary"),
        name="peer_select",
    )(x, g.reshape(1, D_MODEL), sc, sh, wq_t, k1, k2)


def _peer_dense_kernel(ht_ref, c_ref, e1_ref, rk_ref, e2_ref, u_ref, vt_ref, x_ref, gt_ref, o_ref, acc_ref):
    e = pl.program_id(1)
    rows_per_tile = PEER_EXPERT_TILE // N_KEYS

    @pl.when(e == 0)
    def _():
        acc_ref[...] = jnp.zeros_like(acc_ref)

    z = _fdot(u_ref[...], ht_ref[...])
    act = 0.5 * z * (1.0 + lax.erf(z * (1.0 / math.sqrt(2.0))))
    coefs = []
    for ii in range(rows_per_tile):
        i = e * rows_per_tile + ii
        gate = None
        for hd in range(PEER_HEADS):
            cnt = c_ref[hd, pl.ds(i, 1), :]
            e1 = e1_ref[hd, pl.ds(i, 1), :]
            term = jnp.where(rk_ref[hd] < cnt, e2_ref[hd], 0.0) * e1
            gate = term if gate is None else gate + term
        coefs.append((gate * act[ii * N_KEYS:(ii + 1) * N_KEYS]).astype(BF16))
    coef = jnp.concatenate(coefs, axis=0)
    acc_ref[...] += _fdot(vt_ref[...], coef)

    @pl.when(e == pl.num_programs(1) - 1)
    def _():
        o_ref[...] = x_ref[...] + gt_ref[0] * acc_ref[...].T


def _peer_dense(lay, ht, cnt, e1, rank2, e2, u_b, vt_b, x, gt):
    tt = lay.tile
    te = PEER_EXPERT_TILE
    tpg = lay.tiles_per_group
    sel_spec = pl.BlockSpec((PEER_HEADS, N_KEYS, tt), lambda i, e: (0, 0, i))
    return pl.pallas_call(
        _peer_dense_kernel,
        grid=(lay.n_tiles, N_EXPERTS // te),
        in_specs=[pl.BlockSpec((D_MODEL, tt), lambda i, e: (0, i))] + [sel_spec] * 4 + [
            pl.BlockSpec((te, D_MODEL), lambda i, e: (e, 0)),
            pl.BlockSpec((D_MODEL, te), lambda i, e: (0, e)),
            pl.BlockSpec((tt, D_MODEL), lambda i, e: (i, 0)),
            pl.BlockSpec((1, lay.rows, D_MODEL), lambda i, e: (i // tpg, 0, 0))],
        out_specs=pl.BlockSpec((tt, D_MODEL), lambda i, e: (i, 0)),
        out_shape=jax.ShapeDtypeStruct((lay.n, D_MODEL), F32),
        scratch_shapes=[pltpu.VMEM((D_MODEL, tt), F32)],
        compiler_params=_params("arbitrary", "arbitrary"),
        name="peer_dense",
    )(ht, cnt, e1, rank2, e2, u_b, vt_b, x, gt)


def _run(batch, seq, x, m_all, st_conv, st_shift, st_wkv, p):
    lay = _Layout(batch, seq)
    x = x.reshape(lay.n, D_MODEL)
    new_conv, new_shift, new_wkv = [], [], []
    v_first = None
    short = seq < WKV_CHUNK
    for i in range(DEPTH):
        sh_m, sc_m, gt_m, sh_f, sc_f, gt_f = (lay.expand(t) for t in jnp.split(m_all[i], 6, axis=-1))
        j = i // 2
        if i % 2 == 0:
            bg, u = _conv_in(lay, x, p["g_mix"][i], sc_m, sh_m, p["conv_w_in"][j])
            up = jnp.concatenate([st_conv[j], u.reshape(batch, seq, D_MODEL)], axis=1)
            u0 = up[:, 0:seq].reshape(lay.n, D_MODEL)
            u1 = up[:, 1:seq + 1].reshape(lay.n, D_MODEL)
            new_conv.append(up[:, -(CONV_WIDTH - 1):])
            x = _conv_out(lay, bg, u0, u1, u, p["conv_k"][j], p["conv_w_out"][j], x, gt_m)
        else:
            h = _norm_mod(lay, x, p["g_mix"][i], sc_m, sh_m)
            h3 = h.reshape(batch, seq, D_MODEL)
            hs = jnp.concatenate([st_shift[j][:, None, :], h3[:, :-1]], axis=1).reshape(lay.n, D_MODEL)
            new_shift.append(h3[:, -1])
            zero = jnp.zeros((D_MODEL,), F32)
            v0 = p["rwkv_v0"][j - 1] if j > 0 else zero
            vecs = jnp.stack([p["rwkv_w0"][j], p["rwkv_a0"][j], v0, p["rwkv_k_k"][j], p["rwkv_k_a"][j],
                              zero, zero, zero])
            v1 = p["rwkv_v1"][j - 1] if j > 0 else None
            v2 = p["rwkv_v2"][j - 1] if j > 0 else None
            r, km, kk, v, lw, a, g = _rwkv_proj(
                _Layout(batch, seq, RWKV_PROJ_TILE), h, hs, v_first if j > 0 else None, p["rwkv_mix"][j], vecs,
                p["rwkv_wr"][j], p["rwkv_wk"][j], p["rwkv_wv"][j], p["rwkv_w1"][j], p["rwkv_w2"][j],
                p["rwkv_a1"][j], p["rwkv_a2"][j], p["rwkv_g1"][j], p["rwkv_g2"][j], v1, v2)
            if j == 0:
                v_first = v
            scan_in = (r, km, kk, v, lw, a, g)
            if short:
                pad = lambda t: jnp.pad(t.reshape(batch, seq, D_MODEL),
                                        ((0, 0), (0, WKV_SHORT_CHUNK - seq), (0, 0))).reshape(-1, D_MODEL)
                scan_in = tuple(pad(t) for t in scan_in)
                yg, s_new = _wkv(batch, WKV_SHORT_CHUNK, WKV_SHORT_CHUNK, N_HEADS, *scan_in,
                                 p["rwkv_r_k"][j], p["rwkv_lnw"][j], p["rwkv_lnb"][j], st_wkv[j])
                yg = yg.reshape(batch, WKV_SHORT_CHUNK, D_MODEL)[:, :seq].reshape(lay.n, D_MODEL)
            else:
                yg, s_new = _wkv(batch, seq, WKV_CHUNK, 4, *scan_in,
                                 p["rwkv_r_k"][j], p["rwkv_lnw"][j], p["rwkv_lnb"][j], st_wkv[j])
            new_wkv.append(s_new)
            x = _mm_res(lay, yg, p["rwkv_wo"][j], x, gt_m)
        ht, cnt, e1, rank2, e2 = _peer_select(lay, x, p["g_ffn"][i], sc_f, sh_f,
                                              p["wq_t"][i], p["k1_b"][i], p["k2_b"][i])
        x = _peer_dense(lay, ht, cnt, e1, rank2, e2, p["u_b"][i], p["vt_b"][i], x, gt_f)
    y = _final_norm(lay, x, p["g_final"]).reshape(batch, seq, D_MODEL)
    return y, jnp.stack(new_conv), jnp.stack(new_shift), jnp.stack(new_wkv)


def kernel(x_prompt, x_sample, state_conv, state_shift, state_wkv, c_prompt, c_sample, w_ada, b_ada, g_mix, g_ffn, g_final, conv_w_in, conv_k, conv_w_out, rwkv_mix, rwkv_wr, rwkv_wk, rwkv_wv, rwkv_wo, rwkv_w0, rwkv_w1, rwkv_w2, rwkv_a0, rwkv_a1, rwkv_a2, rwkv_v0, rwkv_v1, rwkv_v2, rwkv_g1, rwkv_g2, rwkv_k_k, rwkv_k_a, rwkv_r_k, rwkv_lnw, rwkv_lnb, peer_wq, peer_k1, peer_k2, peer_u, peer_v):
    p = dict(
        g_mix=g_mix, g_ffn=g_ffn, g_final=g_final, conv_w_in=conv_w_in, conv_k=conv_k, conv_w_out=conv_w_out,
        rwkv_mix=rwkv_mix, rwkv_wr=rwkv_wr, rwkv_wk=rwkv_wk, rwkv_wv=rwkv_wv, rwkv_wo=rwkv_wo,
        rwkv_w0=rwkv_w0, rwkv_w1=rwkv_w1, rwkv_w2=rwkv_w2, rwkv_a0=rwkv_a0, rwkv_a1=rwkv_a1, rwkv_a2=rwkv_a2,
        rwkv_v0=rwkv_v0, rwkv_v1=rwkv_v1, rwkv_v2=rwkv_v2, rwkv_g1=rwkv_g1, rwkv_g2=rwkv_g2,
        rwkv_k_k=rwkv_k_k, rwkv_k_a=rwkv_k_a, rwkv_r_k=rwkv_r_k, rwkv_lnw=rwkv_lnw, rwkv_lnb=rwkv_lnb,
        wq_t=jnp.swapaxes(peer_wq, 1, 2).astype(BF16),
        k1_b=peer_k1.astype(BF16), k2_b=peer_k2.astype(BF16),
        u_b=peer_u.astype(BF16), vt_b=jnp.swapaxes(peer_v, 1, 2).astype(BF16),
    )
    n_prompt, seq_prompt = x_prompt.shape[0], x_prompt.shape[1]
    n_sample, seq_sample = x_sample.shape[0], x_sample.shape[1]
    m_all = _adaln(jnp.concatenate([c_prompt, c_sample], axis=0), w_ada, b_ada)
    zeros = lambda *s: jnp.zeros(s, F32)
    n_conv, n_rwkv = (DEPTH + 1) // 2, DEPTH // 2
    y_p, p_conv, p_shift, p_wkv = _run(
        n_prompt, seq_prompt, x_prompt, m_all[:, :n_prompt],
        zeros(n_conv, n_prompt, CONV_WIDTH - 1, D_MODEL), zeros(n_rwkv, n_prompt, D_MODEL),
        zeros(n_rwkv, n_prompt, N_HEADS, HEAD_SIZE, HEAD_SIZE), p)
    y_s, s_conv, s_shift, s_wkv = _run(
        n_sample, seq_sample, x_sample, m_all[:, n_prompt:], state_conv, state_shift, state_wkv, p)
    return (y_p, y_s, p_conv, p_shift, p_wkv, s_conv, s_shift, s_wkv)
```

```python
import functools
import math

import jax
import jax.numpy as jnp
from jax import lax
from jax.experimental import pallas as pl
from jax.experimental.pallas import tpu as pltpu

D_MODEL = 1024
DEPTH = 4
CONV_WIDTH = 3
HEAD_SIZE = 64
N_HEADS = D_MODEL // HEAD_SIZE
N_KEYS = 128
N_EXPERTS = N_KEYS * N_KEYS
PEER_HEADS = 8
PEER_TOPK = 16
D_QUERY = 256
RMS_EPS = 1e-6
GN_EPS = 64e-5

LORA_PAD = 128
TOKEN_TILE = 512
RWKV_PROJ_TILE = 256
PEER_EXPERT_TILE = 2048
PEER_GROUP = 512
PEER_LANE_CHUNK = 128
WKV_CHUNK = 64
WKV_SHORT_CHUNK = 8
WKV_HEADS_PER_STEP = 8
VMEM_LIMIT = 56 * 1024 * 1024

F32 = jnp.float32
BF16 = jnp.bfloat16


def _params(*sem):
    return pltpu.CompilerParams(dimension_semantics=sem, vmem_limit_bytes=VMEM_LIMIT)


def _sigmoid(x):
    return 1.0 / (1.0 + jnp.exp(-x))


def _rms_mod(x, g, sc, sh):
    y = x * lax.rsqrt(jnp.mean(x * x, axis=-1, keepdims=True) + RMS_EPS)
    return (y * g) * (1.0 + sc) + sh


def _bdot(a, b):
    return jnp.dot(a.astype(BF16), b.astype(BF16), preferred_element_type=F32)


def _fdot(a, b):
    return jnp.dot(a, b, preferred_element_type=F32)


def _fdot_nt(a, b):
    return lax.dot_general(a, b, (((1,), (1,)), ((), ())), preferred_element_type=F32)


def _fdot_tn(a, b):
    return lax.dot_general(a, b, (((0,), (0,)), ((), ())), preferred_element_type=F32)


class _Layout:
    def __init__(self, batch, seq, tile=TOKEN_TILE):
        self.batch, self.seq = batch, seq
        self.n = batch * seq
        self.tile = min(tile, self.n)
        assert self.n % self.tile == 0
        self.n_tiles = self.n // self.tile
        if seq % self.tile == 0:
            self.rows = 1
            self.tiles_per_group = seq // self.tile
        else:
            assert self.tile % seq == 0
            self.rows = self.tile
            self.tiles_per_group = 1

    def expand(self, m):
        if self.rows == 1:
            return m[:, None, :]
        return jnp.repeat(m, self.seq, axis=0).reshape(self.n_tiles, self.tile, m.shape[-1])

    def mod_spec(self):
        tpg = self.tiles_per_group
        return pl.BlockSpec((1, self.rows, D_MODEL), lambda i: (i // tpg, 0, 0))

    def tok_spec(self, width=D_MODEL):
        return pl.BlockSpec((self.tile, width), lambda i: (i, 0))


def _full_spec(shape):
    nd = len(shape)
    return pl.BlockSpec(shape, lambda *_: (0,) * nd)


def _adaln_kernel(c_ref, w_ref, b_ref, o_ref):
    c = c_ref[...]
    cs = c * _sigmoid(c)
    o_ref[0] = _fdot(cs, w_ref[0]) + b_ref[0]


def _adaln(c, w_ada, b_ada):
    nb = c.shape[0]
    tn = 1536
    return pl.pallas_call(
        _adaln_kernel,
        grid=(DEPTH, 6 * D_MODEL // tn),
        in_specs=[
            pl.BlockSpec((nb, D_MODEL), lambda l, j: (0, 0)),
            pl.BlockSpec((1, D_MODEL, tn), lambda l, j: (l, 0, j)),
            pl.BlockSpec((1, 1, tn), lambda l, j: (l, 0, j)),
        ],
        out_specs=pl.BlockSpec((1, nb, tn), lambda l, j: (l, 0, j)),
        out_shape=jax.ShapeDtypeStruct((DEPTH, nb, 6 * D_MODEL), F32),
        compiler_params=_params("arbitrary", "arbitrary"),
        name="adaln",
    )(c, w_ada, b_ada.reshape(DEPTH, 1, 6 * D_MODEL))


def _norm_mod_kernel(x_ref, g_ref, sc_ref, sh_ref, o_ref):
    o_ref[...] = _rms_mod(x_ref[...], g_ref[...], sc_ref[0], sh_ref[0])


def _norm_mod(lay, x, g, sc, sh):
    return pl.pallas_call(
        _norm_mod_kernel,
        grid=(lay.n_tiles,),
        in_specs=[lay.tok_spec(), _full_spec((1, D_MODEL)), lay.mod_spec(), lay.mod_spec()],
        out_specs=lay.tok_spec(),
        out_shape=jax.ShapeDtypeStruct((lay.n, D_MODEL), F32),
        compiler_params=_params("arbitrary"),
        name="norm_mod",
    )(x, g.reshape(1, D_MODEL), sc, sh)


def _final_norm_kernel(x_ref, g_ref, o_ref):
    x = x_ref[...]
    o_ref[...] = x * lax.rsqrt(jnp.mean(x * x, axis=-1, keepdims=True) + RMS_EPS) * g_ref[...]


def _final_norm(lay, x, g):
    return pl.pallas_call(
        _final_norm_kernel,
        grid=(lay.n_tiles,),
        in_specs=[lay.tok_spec(), _full_spec((1, D_MODEL))],
        out_specs=lay.tok_spec(),
        out_shape=jax.ShapeDtypeStruct((lay.n, D_MODEL), F32),
        compiler_params=_params("arbitrary"),
        name="final_norm",
    )(x, g.reshape(1, D_MODEL))


def _conv_in_kernel(x_ref, g_ref, sc_ref, sh_ref, w_ref, bg_ref, u_ref):
    h = _rms_mod(x_ref[...], g_ref[...], sc_ref[0], sh_ref[0]).astype(BF16)
    d = D_MODEL
    bg_ref[...] = _fdot(h, w_ref[:, 0:d])
    u_ref[...] = _fdot(h, w_ref[:, d:2 * d]) * _fdot(h, w_ref[:, 2 * d:3 * d])


def _conv_in(lay, x, g, sc, sh, w_in):
    shp = jax.ShapeDtypeStruct((lay.n, D_MODEL), F32)
    return pl.pallas_call(
        _conv_in_kernel,
        grid=(lay.n_tiles,),
        in_specs=[lay.tok_spec(), _full_spec((1, D_MODEL)), lay.mod_spec(), lay.mod_spec(),
                  _full_spec((D_MODEL, 3 * D_MODEL))],
        out_specs=[lay.tok_spec(), lay.tok_spec()],
        out_shape=[shp, shp],
        compiler_params=_params("arbitrary"),
        name="conv_in",
    )(x, g.reshape(1, D_MODEL), sc, sh, w_in.astype(BF16))


def _conv_out_kernel(bg_ref, u0_ref, u1_ref, u2_ref, ck_ref, w_ref, x_ref, gt_ref, o_ref):
    conv = u0_ref[...] * ck_ref[0:1, :] + u1_ref[...] * ck_ref[1:2, :] + u2_ref[...] * ck_ref[2:3, :]
    y = _bdot(bg_ref[...] * conv, w_ref[...])
    o_ref[...] = x_ref[...] + gt_ref[0] * y


def _conv_out(lay, bg, u0, u1, u2, conv_k, w_out, x, gt):
    return pl.pallas_call(
        _conv_out_kernel,
        grid=(lay.n_tiles,),
        in_specs=[lay.tok_spec()] * 4 + [_full_spec((CONV_WIDTH, D_MODEL)), _full_spec((D_MODEL, D_MODEL)),
                                         lay.tok_spec(), lay.mod_spec()],
        out_specs=lay.tok_spec(),
        out_shape=jax.ShapeDtypeStruct((lay.n, D_MODEL), F32),
        compiler_params=_params("arbitrary"),
        name="conv_out",
    )(bg, u0, u1, u2, conv_k, w_out.astype(BF16), x, gt)


def _mm_res_kernel(a_ref, w_ref, x_ref, gt_ref, o_ref):
    o_ref[...] = x_ref[...] + gt_ref[0] * _bdot(a_ref[...], w_ref[...])


def _mm_res(lay, a, w, x, gt):
    return pl.pallas_call(
        _mm_res_kernel,
        grid=(lay.n_tiles,),
        in_specs=[lay.tok_spec(), _full_spec((D_MODEL, D_MODEL)), lay.tok_spec(), lay.mod_spec()],
        out_specs=lay.tok_spec(),
        out_shape=jax.ShapeDtypeStruct((lay.n, D_MODEL), F32),
        compiler_params=_params("arbitrary"),
        name="mm_res",
    )(a, w.astype(BF16), x, gt)


def _rwkv_proj_kernel(*refs, has_vres):
    if has_vres:
        (h_ref, hs_ref, vf_ref, mix_ref, vec_ref, wr_ref, wk_ref, wv_ref, w1_ref, w2_ref, a1_ref, a2_ref,
         g1_ref, g2_ref, v1_ref, v2_ref, r_ref, km_ref, kk_ref, v_ref, lw_ref, a_ref, g_ref) = refs
    else:
        (h_ref, hs_ref, mix_ref, vec_ref, wr_ref, wk_ref, wv_ref, w1_ref, w2_ref, a1_ref, a2_ref,
         g1_ref, g2_ref, r_ref, km_ref, kk_ref, v_ref, lw_ref, a_ref, g_ref) = refs
    h = h_ref[...]
    xx = hs_ref[...] - h
    mixed = lambda i: (h + xx * mix_ref[i:i + 1, :]).astype(BF16)
    w0, a0, v0, k_k, k_a = (vec_ref[i:i + 1, :] for i in range(5))

    r_ref[...] = _fdot(mixed(0), wr_ref[...])
    wl = w0 + _bdot(jnp.tanh(_fdot(mixed(1), w1_ref[...])), w2_ref[...])
    z = -wl
    wlog = -(jnp.maximum(z, 0.0) + jnp.log(1.0 + jnp.exp(-jnp.abs(z)))) - 0.5
    lw_ref[...] = -jnp.exp(wlog)
    k = _fdot(mixed(2), wk_ref[...])
    xv = mixed(3)
    v = _fdot(xv, wv_ref[...])
    if has_vres:
        v = v + (vf_ref[...] - v) * _sigmoid(v0 + _bdot(_fdot(xv, v1_ref[...]), v2_ref[...]))
    v_ref[...] = v
    a = _sigmoid(a0 + _bdot(_fdot(mixed(4), a1_ref[...]), a2_ref[...]))
    a_ref[...] = a
    g_ref[...] = _bdot(_sigmoid(_fdot(mixed(5), g1_ref[...])), g2_ref[...])
    kk_ref[...] = k * k_k
    km_ref[...] = k * (1.0 + (a - 1.0) * k_a)


def _pad_cols(w):
    return jnp.pad(w, ((0, 0), (0, LORA_PAD - w.shape[1]))).astype(BF16)


def _pad_rows(w):
    return jnp.pad(w, ((0, LORA_PAD - w.shape[0]), (0, 0))).astype(BF16)


def _rwkv_proj(lay, h, hs, v_first, mix, vecs, wr, wk, wv, w1, w2, a1, a2, g1, g2, v1, v2):
    has_vres = v_first is not None
    sq = _full_spec((D_MODEL, D_MODEL))
    down = _full_spec((D_MODEL, LORA_PAD))
    up = _full_spec((LORA_PAD, D_MODEL))
    in_specs = [lay.tok_spec(), lay.tok_spec()] + ([lay.tok_spec()] if has_vres else [])
    in_specs += [_full_spec((6, D_MODEL)), _full_spec((8, D_MODEL)), sq, sq, sq, down, up, down, up, down, up]
    args = [h, hs] + ([v_first] if has_vres else [])
    args += [mix, vecs, wr.astype(BF16), wk.astype(BF16), wv.astype(BF16), _pad_cols(w1), _pad_rows(w2),
             _pad_cols(a1), _pad_rows(a2), _pad_cols(g1), _pad_rows(g2)]
    if has_vres:
        in_specs += [down, up]
        args += [_pad_cols(v1), _pad_rows(v2)]
    shp = jax.ShapeDtypeStruct((lay.n, D_MODEL), F32)
    return pl.pallas_call(
        functools.partial(_rwkv_proj_kernel, has_vres=has_vres),
        grid=(lay.n_tiles,),
        in_specs=in_specs,
        out_specs=[lay.tok_spec()] * 7,
        out_shape=[shp] * 7,
        compiler_params=_params("arbitrary"),
        name="rwkv_proj",
    )(*args)


def _wkv_kernel(r_ref, km_ref, kk_ref, v_ref, lw_ref, a_ref, g_ref, rk_ref, lnw_ref, lnb_ref, s0_ref,
                y_ref, s_ref, st_ref, *, chunk, heads):
    t = pl.program_id(2)
    L = chunk

    @pl.when(t == 0)
    def _():
        st_ref[...] = s0_ref[0]

    row = lax.broadcasted_iota(jnp.int32, (L, L), 0)
    col = lax.broadcasted_iota(jnp.int32, (L, L), 1)
    strict = row > col
    incl = row >= col
    tri = incl.astype(F32)

    hds = range(heads)
    each = lambda f, *cols: [f(*xs) for xs in zip(*cols)]
    sls = [slice(hd * HEAD_SIZE, (hd + 1) * HEAD_SIZE) for hd in hds]
    r, km, kkr, v, lw, asig, g = ([ref[:, sl] for sl in sls]
                                  for ref in (r_ref, km_ref, kk_ref, v_ref, lw_ref, a_ref, g_ref))
    s0 = [st_ref[hd] for hd in hds]
    c = each(lambda x: _fdot(tri, x), lw)
    kk = each(lambda x: x * lax.rsqrt(jnp.maximum(jnp.sum(x * x, axis=-1, keepdims=True), 1e-24)), kkr)
    b = each(lambda x, y: x * y, kk, asig)
    c_last = each(lambda x: x[L - 1:L, :], c)
    ginv = each(lambda x: jnp.exp(-x), c)
    lhs = each(lambda kk_, c_, lw_, r_: jnp.concatenate([-kk_ * jnp.exp(c_ - lw_), r_ * jnp.exp(c_)], axis=0),
               kk, c, lw, r)
    rhs = each(lambda b_, km_, gi: jnp.concatenate([b_ * gi, km_ * gi], axis=0), b, km, ginv)
    qk = each(_fdot_nt, lhs, rhs)
    hs0 = each(_fdot_nt, lhs, s0)
    m_ab = each(lambda x: jnp.where(strict, x[:L, :L], 0.0), qk)
    m_ak = each(lambda x: jnp.where(strict, x[:L, L:], 0.0), qk)
    w_ = each(lambda h_, m_, v_: h_[:L] + _fdot(m_, v_), hs0, m_ak, v)
    u = each(lambda w, m_: w + _fdot(m_, w), w_, m_ab)
    p = m_ab
    for _ in range(int(math.log2(L)) - 1):
        p = each(lambda x: _fdot(x, x), p)
        u = each(lambda u_, p_: u_ + _fdot(p_, u_), u, p)
    gtail = each(lambda cl, c_: jnp.exp(cl - c_), c_last, c)
    s_new = each(lambda s_, cl, u_, b_, v_, km_, gt_: s_ * jnp.exp(cl) + _fdot_tn(u_, b_ * gt_)
                 + _fdot_tn(v_, km_ * gt_), s0, c_last, u, b, v, km, gtail)
    for hd in hds:
        st_ref[hd] = s_new[hd]
    y = each(lambda h_, q_, u_, v_: h_[L:] + _fdot(jnp.where(incl, q_[L:, :L], 0.0), u_)
             + _fdot(jnp.where(incl, q_[L:, L:], 0.0), v_), hs0, qk, u, v)

    def finish(y_, r_, km_, v_, g_, sl):
        yc = y_ - jnp.mean(y_, axis=-1, keepdims=True)
        var = jnp.mean(yc * yc, axis=-1, keepdims=True)
        yn = yc * lax.rsqrt(var + GN_EPS) * lnw_ref[:, sl] + lnb_ref[:, sl]
        bonus = jnp.sum(r_ * km_ * rk_ref[:, sl], axis=-1, keepdims=True) * v_
        return (yn + bonus) * g_

    y_ref[...] = jnp.concatenate(each(finish, y, r, km, v, g, sls), axis=1)

    @pl.when(t == pl.num_programs(2) - 1)
    def _():
        s_ref[0] = st_ref[...]


def _wkv(batch, seq, chunk, heads, r, km, kk, v, lw, a, g, r_k, lnw, lnb, s0):
    n_t = seq // chunk
    width = heads * HEAD_SIZE
    tok = pl.BlockSpec((chunk, width), lambda b, h, t: (b * n_t + t, h))
    vec = pl.BlockSpec((1, width), lambda b, h, t: (0, h))
    st = pl.BlockSpec((1, heads, HEAD_SIZE, HEAD_SIZE), lambda b, h, t: (b, h, 0, 0))
    return pl.pallas_call(
        functools.partial(_wkv_kernel, chunk=chunk, heads=heads),
        grid=(batch, N_HEADS // heads, n_t),
        in_specs=[tok] * 7 + [vec] * 3 + [st],
        out_specs=[tok, st],
        out_shape=[jax.ShapeDtypeStruct((batch * seq, D_MODEL), F32),
                   jax.ShapeDtypeStruct((batch, N_HEADS, HEAD_SIZE, HEAD_SIZE), F32)],
        scratch_shapes=[pltpu.VMEM((heads, HEAD_SIZE, HEAD_SIZE), F32)],
        compiler_params=_params("arbitrary", "arbitrary", "arbitrary"),
        name="wkv",
    )(r, km, kk, v, lw, a, g, r_k.reshape(1, D_MODEL), lnw.reshape(1, D_MODEL), lnb.reshape(1, D_MODEL), s0)


def _top16(s):
    work = s
    rank = jnp.full(s.shape, float(PEER_TOPK), F32)
    vals = []
    for k in range(PEER_TOPK):
        m = jnp.max(work, axis=0, keepdims=True)
        hit = work == m
        rank = jnp.where(hit, float(k), rank)
        work = jnp.where(hit, -jnp.inf, work)
        vals.append(m)
    return vals, rank


def _peer_select_chunk(s1, s2):
    v1, rank1 = _top16(s1)
    v2, rank2 = _top16(s2)
    v2s = jnp.concatenate(v2, axis=0)
    cand = jnp.concatenate([v1[a] + v2s for a in range(PEER_TOPK)], axis=0)
    work = cand
    tau = None
    for _ in range(PEER_TOPK):
        tau = jnp.max(work, axis=0, keepdims=True)
        work = jnp.where(work == tau, -jnp.inf, work)
    top = v1[0] + v2[0]
    zsum = jnp.sum(jnp.where(cand >= tau, jnp.exp(cand - top), 0.0), axis=0, keepdims=True)
    cnt = jnp.zeros(s1.shape, F32)
    for b in range(PEER_TOPK):
        cnt = cnt + jnp.where(s1 + v2[b] >= tau, 1.0, 0.0)
    cnt = jnp.where(rank1 < float(PEER_TOPK), cnt, 0.0)
    e1 = jnp.exp(s1 - v1[0])
    e2 = jnp.exp(s2 - v2[0]) / zsum
    return cnt, e1, rank2, e2


def _peer_select_kernel(x_ref, g_ref, sc_ref, sh_ref, wq_ref, k1_ref, k2_ref,
                        ht_ref, c_ref, e1_ref, rk_ref, e2_ref, q_scr):
    h = _rms_mod(x_ref[...], g_ref[...], sc_ref[0], sh_ref[0])
    ht = h.T.astype(BF16)
    ht_ref[...] = ht
    q_scr[...] = _fdot(wq_ref[...], ht)
    half = D_QUERY // 2
    n_chunks = x_ref.shape[0] // PEER_LANE_CHUNK

    def head_body(hd, carry):
        q1 = q_scr[pl.ds(pl.multiple_of(hd * D_QUERY, D_QUERY), half), :]
        q2 = q_scr[pl.ds(pl.multiple_of(hd * D_QUERY + half, half), half), :]
        s1 = _bdot(k1_ref[hd], q1)
        s2 = _bdot(k2_ref[hd], q2)
        for ch in range(n_chunks):
            ls = slice(ch * PEER_LANE_CHUNK, (ch + 1) * PEER_LANE_CHUNK)
            cnt, e1, rank2, e2 = _peer_select_chunk(s1[:, ls], s2[:, ls])
            c_ref[hd, :, ls] = cnt
            e1_ref[hd, :, ls] = e1
            rk_ref[hd, :, ls] = rank2.astype(BF16)
            e2_ref[hd, :, ls] = e2.astype(BF16)
        return carry

    lax.fori_loop(0, PEER_HEADS, head_body, 0)


def _peer_select(lay, x, g, sc, sh, wq_t, k1, k2):
    tt = lay.tile
    sel_spec = pl.BlockSpec((PEER_HEADS, N_KEYS, tt), lambda i: (0, 0, i))
    sel_shape = lambda dt: jax.ShapeDtypeStruct((PEER_HEADS, N_KEYS, lay.n), dt)
    return pl.pallas_call(
        _peer_select_kernel,
        grid=(lay.n_tiles,),
        in_specs=[lay.tok_spec(), _full_spec((1, D_MODEL)), lay.mod_spec(), lay.mod_spec(),
                  _full_spec((PEER_HEADS * D_QUERY, D_MODEL)),
                  _full_spec((PEER_HEADS, N_KEYS, D_QUERY // 2)), _full_spec((PEER_HEADS, N_KEYS, D_QUERY // 2))],
        out_specs=[pl.BlockSpec((D_MODEL, tt), lambda i: (0, i))] + [sel_spec] * 4,
        out_shape=[jax.ShapeDtypeStruct((D_MODEL, lay.n), BF16),
                   sel_shape(F32), sel_shape(F32), sel_shape(BF16), sel_shape(BF16)],
        scratch_shapes=[pltpu.VMEM((PEER_HEADS * D_QUERY, tt), F32)],
        compiler_params=_params("arbitrary"),
        name="peer_select",
    )(x, g.reshape(1, D_MODEL), sc, sh, wq_t, k1, k2)


def _peer_dense_kernel(ht_ref, c_ref, e1_ref, rk_ref, e2_ref, u_ref, vt_ref, x_ref, gt_ref, o_ref,
                       acc_ref, coef_ref):
    e = pl.program_id(1)
    groups = PEER_EXPERT_TILE // PEER_GROUP
    rows_per_group = PEER_GROUP // N_KEYS

    @pl.when(e == 0)
    def _():
        acc_ref[...] = jnp.zeros_like(acc_ref)

    def build_coef(g, slot):
        for ii in range(rows_per_group):
            row = g * rows_per_group + ii
            i = e * (groups * rows_per_group) + row
            z = _fdot(u_ref[pl.ds(pl.multiple_of(row * N_KEYS, N_KEYS), N_KEYS), :], ht_ref[...])
            gate = None
            for hd in range(PEER_HEADS):
                cnt = c_ref[hd, pl.ds(i, 1), :].astype(BF16)
                e1 = e1_ref[hd, pl.ds(i, 1), :].astype(BF16)
                term = jnp.where(rk_ref[hd] < cnt, e2_ref[hd], jnp.zeros((), BF16)) * e1
                gate = term if gate is None else gate + term
            act = 0.5 * z * (1.0 + lax.erf(z * (1.0 / math.sqrt(2.0))))
            coef_ref[slot, ii * N_KEYS:(ii + 1) * N_KEYS, :] = gate * act.astype(BF16)

    def accumulate(g, slot):
        acc_ref[...] += _fdot(vt_ref[g], coef_ref[slot])

    build_coef(0, 0)

    def body(g, carry):
        accumulate(g - 1, (g - 1) % 2)
        build_coef(g, g % 2)
        return carry

    lax.fori_loop(1, groups, body, 0)
    accumulate(groups - 1, (groups - 1) % 2)

    @pl.when(e == pl.num_programs(1) - 1)
    def _():
        o_ref[...] = x_ref[...] + gt_ref[0] * acc_ref[...].T


def _peer_dense(lay, ht, cnt, e1, rank2, e2, u_b, vt_b, x, gt):
    tt = lay.tile
    te = PEER_EXPERT_TILE
    groups = te // PEER_GROUP
    tpg = lay.tiles_per_group
    sel_spec = pl.BlockSpec((PEER_HEADS, N_KEYS, tt), lambda i, e: (0, 0, i))
    return pl.pallas_call(
        _peer_dense_kernel,
        grid=(lay.n_tiles, N_EXPERTS // te),
        in_specs=[pl.BlockSpec((D_MODEL, tt), lambda i, e: (0, i))] + [sel_spec] * 4 + [
            pl.BlockSpec((te, D_MODEL), lambda i, e: (e, 0)),
            pl.BlockSpec((groups, D_MODEL, PEER_GROUP), lambda i, e: (e, 0, 0)),
            pl.BlockSpec((tt, D_MODEL), lambda i, e: (i, 0)),
            pl.BlockSpec((1, lay.rows, D_MODEL), lambda i, e: (i // tpg, 0, 0))],
        out_specs=pl.BlockSpec((tt, D_MODEL), lambda i, e: (i, 0)),
        out_shape=jax.ShapeDtypeStruct((lay.n, D_MODEL), F32),
        scratch_shapes=[pltpu.VMEM((D_MODEL, tt), F32), pltpu.VMEM((2, PEER_GROUP, tt), BF16)],
        compiler_params=_params("arbitrary", "arbitrary"),
        name="peer_dense",
    )(ht, cnt, e1, rank2, e2, u_b, vt_b, x, gt)


def _run(batch, seq, x, m_all, st_conv, st_shift, st_wkv, p):
    lay = _Layout(batch, seq)
    x = x.reshape(lay.n, D_MODEL)
    new_conv, new_shift, new_wkv = [], [], []
    v_first = None
    short = seq < WKV_CHUNK
    for i in range(DEPTH):
        sh_m, sc_m, gt_m, sh_f, sc_f, gt_f = (lay.expand(t) for t in jnp.split(m_all[i], 6, axis=-1))
        j = i // 2
        if i % 2 == 0:
            bg, u = _conv_in(lay, x, p["g_mix"][i], sc_m, sh_m, p["conv_w_in"][j])
            up = jnp.concatenate([st_conv[j], u.reshape(batch, seq, D_MODEL)], axis=1)
            u0 = up[:, 0:seq].reshape(lay.n, D_MODEL)
            u1 = up[:, 1:seq + 1].reshape(lay.n, D_MODEL)
            new_conv.append(up[:, -(CONV_WIDTH - 1):])
            x = _conv_out(lay, bg, u0, u1, u, p["conv_k"][j], p["conv_w_out"][j], x, gt_m)
        else:
            h = _norm_mod(lay, x, p["g_mix"][i], sc_m, sh_m)
            h3 = h.reshape(batch, seq, D_MODEL)
            hs = jnp.concatenate([st_shift[j][:, None, :], h3[:, :-1]], axis=1).reshape(lay.n, D_MODEL)
            new_shift.append(h3[:, -1])
            zero = jnp.zeros((D_MODEL,), F32)
            v0 = p["rwkv_v0"][j - 1] if j > 0 else zero
            vecs = jnp.stack([p["rwkv_w0"][j], p["rwkv_a0"][j], v0, p["rwkv_k_k"][j], p["rwkv_k_a"][j],
                              zero, zero, zero])
            v1 = p["rwkv_v1"][j - 1] if j > 0 else None
            v2 = p["rwkv_v2"][j - 1] if j > 0 else None
            r, km, kk, v, lw, a, g = _rwkv_proj(
                _Layout(batch, seq, RWKV_PROJ_TILE), h, hs, v_first if j > 0 else None, p["rwkv_mix"][j], vecs,
                p["rwkv_wr"][j], p["rwkv_wk"][j], p["rwkv_wv"][j], p["rwkv_w1"][j], p["rwkv_w2"][j],
                p["rwkv_a1"][j], p["rwkv_a2"][j], p["rwkv_g1"][j], p["rwkv_g2"][j], v1, v2)
            if j == 0:
                v_first = v
            scan_in = (r, km, kk, v, lw, a, g)
            if short:
                pad = lambda t: jnp.pad(t.reshape(batch, seq, D_MODEL),
                                        ((0, 0), (0, WKV_SHORT_CHUNK - seq), (0, 0))).reshape(-1, D_MODEL)
                scan_in = tuple(pad(t) for t in scan_in)
                yg, s_new = _wkv(batch, WKV_SHORT_CHUNK, WKV_SHORT_CHUNK, N_HEADS, *scan_in,
                                 p["rwkv_r_k"][j], p["rwkv_lnw"][j], p["rwkv_lnb"][j], st_wkv[j])
                yg = yg.reshape(batch, WKV_SHORT_CHUNK, D_MODEL)[:, :seq].reshape(lay.n, D_MODEL)
            else:
                yg, s_new = _wkv(batch, seq, WKV_CHUNK, WKV_HEADS_PER_STEP, *scan_in,
                                 p["rwkv_r_k"][j], p["rwkv_lnw"][j], p["rwkv_lnb"][j], st_wkv[j])
            new_wkv.append(s_new)
            x = _mm_res(lay, yg, p["rwkv_wo"][j], x, gt_m)
        ht, cnt, e1, rank2, e2 = _peer_select(lay, x, p["g_ffn"][i], sc_f, sh_f,
                                              p["wq_t"][i], p["k1_b"][i], p["k2_b"][i])
        x = _peer_dense(lay, ht, cnt, e1, rank2, e2, p["u_b"][i], p["vt_b"][i], x, gt_f)
    y = _final_norm(lay, x, p["g_final"]).reshape(batch, seq, D_MODEL)
    return y, jnp.stack(new_conv), jnp.stack(new_shift), jnp.stack(new_wkv)


def kernel(x_prompt, x_sample, state_conv, state_shift, state_wkv, c_prompt, c_sample, w_ada, b_ada, g_mix, g_ffn, g_final, conv_w_in, conv_k, conv_w_out, rwkv_mix, rwkv_wr, rwkv_wk, rwkv_wv, rwkv_wo, rwkv_w0, rwkv_w1, rwkv_w2, rwkv_a0, rwkv_a1, rwkv_a2, rwkv_v0, rwkv_v1, rwkv_v2, rwkv_g1, rwkv_g2, rwkv_k_k, rwkv_k_a, rwkv_r_k, rwkv_lnw, rwkv_lnb, peer_wq, peer_k1, peer_k2, peer_u, peer_v):
    p = dict(
        g_mix=g_mix, g_ffn=g_ffn, g_final=g_final, conv_w_in=conv_w_in, conv_k=conv_k, conv_w_out=conv_w_out,
        rwkv_mix=rwkv_mix, rwkv_wr=rwkv_wr, rwkv_wk=rwkv_wk, rwkv_wv=rwkv_wv, rwkv_wo=rwkv_wo,
        rwkv_w0=rwkv_w0, rwkv_w1=rwkv_w1, rwkv_w2=rwkv_w2, rwkv_a0=rwkv_a0, rwkv_a1=rwkv_a1, rwkv_a2=rwkv_a2,
        rwkv_v0=rwkv_v0, rwkv_v1=rwkv_v1, rwkv_v2=rwkv_v2, rwkv_g1=rwkv_g1, rwkv_g2=rwkv_g2,
        rwkv_k_k=rwkv_k_k, rwkv_k_a=rwkv_k_a, rwkv_r_k=rwkv_r_k, rwkv_lnw=rwkv_lnw, rwkv_lnb=rwkv_lnb,
        wq_t=jnp.swapaxes(peer_wq, 1, 2).astype(BF16),
        k1_b=peer_k1.astype(BF16), k2_b=peer_k2.astype(BF16),
        u_b=peer_u.astype(BF16),
        vt_b=jnp.swapaxes(peer_v.astype(BF16).reshape(DEPTH, N_EXPERTS // PEER_GROUP, PEER_GROUP, D_MODEL), 2, 3),
    )
    n_prompt, seq_prompt = x_prompt.shape[0], x_prompt.shape[1]
    n_sample, seq_sample = x_sample.shape[0], x_sample.shape[1]
    m_all = _adaln(jnp.concatenate([c_prompt, c_sample], axis=0), w_ada, b_ada)
    zeros = lambda *s: jnp.zeros(s, F32)
    n_conv, n_rwkv = (DEPTH + 1) // 2, DEPTH // 2
    y_p, p_conv, p_shift, p_wkv = _run(
        n_prompt, seq_prompt, x_prompt, m_all[:, :n_prompt],
        zeros(n_conv, n_prompt, CONV_WIDTH - 1, D_MODEL), zeros(n_rwkv, n_prompt, D_MODEL),
        zeros(n_rwkv, n_prompt, N_HEADS, HEAD_SIZE, HEAD_SIZE), p)
    y_s, s_conv, s_shift, s_wkv = _run(
        n_sample, seq_sample, x_sample, m_all[:, n_prompt:], state_conv, state_shift, state_wkv, p)
    return (y_p, y_s, p_conv, p_shift, p_wkv, s_conv, s_shift, s_wkv)
```

```python
import functools
import math

import jax
import jax.numpy as jnp
from jax import lax
from jax.experimental import pallas as pl
from jax.experimental.pallas import tpu as pltpu

D_MODEL = 1024
DEPTH = 4
CONV_WIDTH = 3
HEAD_SIZE = 64
N_HEADS = D_MODEL // HEAD_SIZE
N_KEYS = 128
N_EXPERTS = N_KEYS * N_KEYS
PEER_HEADS = 8
PEER_TOPK = 16
D_QUERY = 256
RMS_EPS = 1e-6
GN_EPS = 64e-5

LORA_PAD = 128
TOKEN_TILE = 512
RWKV_PROJ_TILE = 256
PEER_EXPERT_TILE = 2048
PEER_GROUP = 512
PEER_STEP_ROWS = 4
PEER_LANE_CHUNK = 256
WKV_CHUNK = 64
WKV_SHORT_CHUNK = 8
WKV_HEADS_PER_STEP = 8
VMEM_LIMIT = 56 * 1024 * 1024

F32 = jnp.float32
BF16 = jnp.bfloat16


def _params(*sem):
    return pltpu.CompilerParams(dimension_semantics=sem, vmem_limit_bytes=VMEM_LIMIT)


def _sigmoid(x):
    return 1.0 / (1.0 + jnp.exp(-x))


def _rms_mod(x, g, sc, sh):
    y = x * lax.rsqrt(jnp.mean(x * x, axis=-1, keepdims=True) + RMS_EPS)
    return (y * g) * (1.0 + sc) + sh


def _bdot(a, b):
    return jnp.dot(a.astype(BF16), b.astype(BF16), preferred_element_type=F32)


def _fdot(a, b):
    return jnp.dot(a, b, preferred_element_type=F32)


def _fdot_nt(a, b):
    return lax.dot_general(a, b, (((1,), (1,)), ((), ())), preferred_element_type=F32)


def _fdot_tn(a, b):
    return lax.dot_general(a, b, (((0,), (0,)), ((), ())), preferred_element_type=F32)


class _Layout:
    def __init__(self, batch, seq, tile=TOKEN_TILE):
        self.batch, self.seq = batch, seq
        self.n = batch * seq
        self.tile = min(tile, self.n)
        assert self.n % self.tile == 0
        self.n_tiles = self.n // self.tile
        if seq % self.tile == 0:
            self.rows = 1
            self.tiles_per_group = seq // self.tile
        else:
            assert self.tile % seq == 0
            self.rows = self.tile
            self.tiles_per_group = 1

    def expand(self, m):
        if self.rows == 1:
            return m[:, None, :]
        return jnp.repeat(m, self.seq, axis=0).reshape(self.n_tiles, self.tile, m.shape[-1])

    def mod_spec(self):
        tpg = self.tiles_per_group
        return pl.BlockSpec((1, self.rows, D_MODEL), lambda i: (i // tpg, 0, 0))

    def tok_spec(self, width=D_MODEL):
        return pl.BlockSpec((self.tile, width), lambda i: (i, 0))


def _full_spec(shape):
    nd = len(shape)
    return pl.BlockSpec(shape, lambda *_: (0,) * nd)


def _adaln_kernel(c_ref, w_ref, b_ref, o_ref):
    c = c_ref[...]
    cs = c * _sigmoid(c)
    o_ref[0] = _fdot(cs, w_ref[0]) + b_ref[0]


def _adaln(c, w_ada, b_ada):
    nb = c.shape[0]
    tn = 1536
    return pl.pallas_call(
        _adaln_kernel,
        grid=(DEPTH, 6 * D_MODEL // tn),
        in_specs=[
            pl.BlockSpec((nb, D_MODEL), lambda l, j: (0, 0)),
            pl.BlockSpec((1, D_MODEL, tn), lambda l, j: (l, 0, j)),
            pl.BlockSpec((1, 1, tn), lambda l, j: (l, 0, j)),
        ],
        out_specs=pl.BlockSpec((1, nb, tn), lambda l, j: (l, 0, j)),
        out_shape=jax.ShapeDtypeStruct((DEPTH, nb, 6 * D_MODEL), F32),
        compiler_params=_params("arbitrary", "arbitrary"),
        name="adaln",
    )(c, w_ada, b_ada.reshape(DEPTH, 1, 6 * D_MODEL))


def _norm_mod_kernel(x_ref, g_ref, sc_ref, sh_ref, o_ref):
    o_ref[...] = _rms_mod(x_ref[...], g_ref[...], sc_ref[0], sh_ref[0])


def _norm_mod(lay, x, g, sc, sh):
    return pl.pallas_call(
        _norm_mod_kernel,
        grid=(lay.n_tiles,),
        in_specs=[lay.tok_spec(), _full_spec((1, D_MODEL)), lay.mod_spec(), lay.mod_spec()],
        out_specs=lay.tok_spec(),
        out_shape=jax.ShapeDtypeStruct((lay.n, D_MODEL), F32),
        compiler_params=_params("arbitrary"),
        name="norm_mod",
    )(x, g.reshape(1, D_MODEL), sc, sh)


def _final_norm_kernel(x_ref, g_ref, o_ref):
    x = x_ref[...]
    o_ref[...] = x * lax.rsqrt(jnp.mean(x * x, axis=-1, keepdims=True) + RMS_EPS) * g_ref[...]


def _final_norm(lay, x, g):
    return pl.pallas_call(
        _final_norm_kernel,
        grid=(lay.n_tiles,),
        in_specs=[lay.tok_spec(), _full_spec((1, D_MODEL))],
        out_specs=lay.tok_spec(),
        out_shape=jax.ShapeDtypeStruct((lay.n, D_MODEL), F32),
        compiler_params=_params("arbitrary"),
        name="final_norm",
    )(x, g.reshape(1, D_MODEL))


def _conv_in_kernel(x_ref, g_ref, sc_ref, sh_ref, w_ref, bg_ref, u_ref):
    h = _rms_mod(x_ref[...], g_ref[...], sc_ref[0], sh_ref[0]).astype(BF16)
    d = D_MODEL
    bg_ref[...] = _fdot(h, w_ref[:, 0:d])
    u_ref[...] = _fdot(h, w_ref[:, d:2 * d]) * _fdot(h, w_ref[:, 2 * d:3 * d])


def _conv_in(lay, x, g, sc, sh, w_in):
    shp = jax.ShapeDtypeStruct((lay.n, D_MODEL), F32)
    return pl.pallas_call(
        _conv_in_kernel,
        grid=(lay.n_tiles,),
        in_specs=[lay.tok_spec(), _full_spec((1, D_MODEL)), lay.mod_spec(), lay.mod_spec(),
                  _full_spec((D_MODEL, 3 * D_MODEL))],
        out_specs=[lay.tok_spec(), lay.tok_spec()],
        out_shape=[shp, shp],
        compiler_params=_params("arbitrary"),
        name="conv_in",
    )(x, g.reshape(1, D_MODEL), sc, sh, w_in.astype(BF16))


def _conv_out_kernel(bg_ref, u0_ref, u1_ref, u2_ref, ck_ref, w_ref, x_ref, gt_ref, o_ref):
    conv = u0_ref[...] * ck_ref[0:1, :] + u1_ref[...] * ck_ref[1:2, :] + u2_ref[...] * ck_ref[2:3, :]
    y = _bdot(bg_ref[...] * conv, w_ref[...])
    o_ref[...] = x_ref[...] + gt_ref[0] * y


def _conv_out(lay, bg, u0, u1, u2, conv_k, w_out, x, gt):
    return pl.pallas_call(
        _conv_out_kernel,
        grid=(lay.n_tiles,),
        in_specs=[lay.tok_spec()] * 4 + [_full_spec((CONV_WIDTH, D_MODEL)), _full_spec((D_MODEL, D_MODEL)),
                                         lay.tok_spec(), lay.mod_spec()],
        out_specs=lay.tok_spec(),
        out_shape=jax.ShapeDtypeStruct((lay.n, D_MODEL), F32),
        compiler_params=_params("arbitrary"),
        name="conv_out",
    )(bg, u0, u1, u2, conv_k, w_out.astype(BF16), x, gt)


def _mm_res_kernel(a_ref, w_ref, x_ref, gt_ref, o_ref):
    o_ref[...] = x_ref[...] + gt_ref[0] * _bdot(a_ref[...], w_ref[...])


def _mm_res(lay, a, w, x, gt):
    return pl.pallas_call(
        _mm_res_kernel,
        grid=(lay.n_tiles,),
        in_specs=[lay.tok_spec(), _full_spec((D_MODEL, D_MODEL)), lay.tok_spec(), lay.mod_spec()],
        out_specs=lay.tok_spec(),
        out_shape=jax.ShapeDtypeStruct((lay.n, D_MODEL), F32),
        compiler_params=_params("arbitrary"),
        name="mm_res",
    )(a, w.astype(BF16), x, gt)


def _rwkv_proj_kernel(*refs, has_vres):
    if has_vres:
        (h_ref, hs_ref, vf_ref, mix_ref, vec_ref, wr_ref, wk_ref, wv_ref, w1_ref, w2_ref, a1_ref, a2_ref,
         g1_ref, g2_ref, v1_ref, v2_ref, r_ref, km_ref, kk_ref, v_ref, lw_ref, a_ref, g_ref) = refs
    else:
        (h_ref, hs_ref, mix_ref, vec_ref, wr_ref, wk_ref, wv_ref, w1_ref, w2_ref, a1_ref, a2_ref,
         g1_ref, g2_ref, r_ref, km_ref, kk_ref, v_ref, lw_ref, a_ref, g_ref) = refs
    h = h_ref[...]
    xx = hs_ref[...] - h
    mixed = lambda i: (h + xx * mix_ref[i:i + 1, :]).astype(BF16)
    w0, a0, v0, k_k, k_a = (vec_ref[i:i + 1, :] for i in range(5))

    r_ref[...] = _fdot(mixed(0), wr_ref[...])
    wl = w0 + _bdot(jnp.tanh(_fdot(mixed(1), w1_ref[...])), w2_ref[...])
    z = -wl
    wlog = -(jnp.maximum(z, 0.0) + jnp.log(1.0 + jnp.exp(-jnp.abs(z)))) - 0.5
    lw_ref[...] = -jnp.exp(wlog)
    k = _fdot(mixed(2), wk_ref[...])
    xv = mixed(3)
    v = _fdot(xv, wv_ref[...])
    if has_vres:
        v = v + (vf_ref[...] - v) * _sigmoid(v0 + _bdot(_fdot(xv, v1_ref[...]), v2_ref[...]))
    v_ref[...] = v
    a = _sigmoid(a0 + _bdot(_fdot(mixed(4), a1_ref[...]), a2_ref[...]))
    a_ref[...] = a
    g_ref[...] = _bdot(_sigmoid(_fdot(mixed(5), g1_ref[...])), g2_ref[...])
    kk_ref[...] = k * k_k
    km_ref[...] = k * (1.0 + (a - 1.0) * k_a)


def _pad_cols(w):
    return jnp.pad(w, ((0, 0), (0, LORA_PAD - w.shape[1]))).astype(BF16)


def _pad_rows(w):
    return jnp.pad(w, ((0, LORA_PAD - w.shape[0]), (0, 0))).astype(BF16)


def _rwkv_proj(lay, h, hs, v_first, mix, vecs, wr, wk, wv, w1, w2, a1, a2, g1, g2, v1, v2):
    has_vres = v_first is not None
    sq = _full_spec((D_MODEL, D_MODEL))
    down = _full_spec((D_MODEL, LORA_PAD))
    up = _full_spec((LORA_PAD, D_MODEL))
    in_specs = [lay.tok_spec(), lay.tok_spec()] + ([lay.tok_spec()] if has_vres else [])
    in_specs += [_full_spec((6, D_MODEL)), _full_spec((8, D_MODEL)), sq, sq, sq, down, up, down, up, down, up]
    args = [h, hs] + ([v_first] if has_vres else [])
    args += [mix, vecs, wr.astype(BF16), wk.astype(BF16), wv.astype(BF16), _pad_cols(w1), _pad_rows(w2),
             _pad_cols(a1), _pad_rows(a2), _pad_cols(g1), _pad_rows(g2)]
    if has_vres:
        in_specs += [down, up]
        args += [_pad_cols(v1), _pad_rows(v2)]
    shp = jax.ShapeDtypeStruct((lay.n, D_MODEL), F32)
    return pl.pallas_call(
        functools.partial(_rwkv_proj_kernel, has_vres=has_vres),
        grid=(lay.n_tiles,),
        in_specs=in_specs,
        out_specs=[lay.tok_spec()] * 7,
        out_shape=[shp] * 7,
        compiler_params=_params("arbitrary"),
        name="rwkv_proj",
    )(*args)


def _wkv_kernel(r_ref, km_ref, kk_ref, v_ref, lw_ref, a_ref, g_ref, rk_ref, lnw_ref, lnb_ref, s0_ref,
                y_ref, s_ref, st_ref, *, chunk, heads):
    t = pl.program_id(2)
    L = chunk

    @pl.when(t == 0)
    def _():
        st_ref[...] = s0_ref[0]

    row = lax.broadcasted_iota(jnp.int32, (L, L), 0)
    col = lax.broadcasted_iota(jnp.int32, (L, L), 1)
    strict = row > col
    incl = row >= col
    tri = incl.astype(F32)

    hds = range(heads)
    each = lambda f, *cols: [f(*xs) for xs in zip(*cols)]
    sls = [slice(hd * HEAD_SIZE, (hd + 1) * HEAD_SIZE) for hd in hds]
    r, km, kkr, v, lw, asig, g = ([ref[:, sl] for sl in sls]
                                  for ref in (r_ref, km_ref, kk_ref, v_ref, lw_ref, a_ref, g_ref))
    s0 = [st_ref[hd] for hd in hds]
    c = each(lambda x: _fdot(tri, x), lw)
    kk = each(lambda x: x * lax.rsqrt(jnp.maximum(jnp.sum(x * x, axis=-1, keepdims=True), 1e-24)), kkr)
    b = each(lambda x, y: x * y, kk, asig)
    c_last = each(lambda x: x[L - 1:L, :], c)
    ginv = each(lambda x: jnp.exp(-x), c)
    lhs = each(lambda kk_, c_, lw_, r_: jnp.concatenate([-kk_ * jnp.exp(c_ - lw_), r_ * jnp.exp(c_)], axis=0),
               kk, c, lw, r)
    rhs = each(lambda b_, km_, gi: jnp.concatenate([b_ * gi, km_ * gi], axis=0), b, km, ginv)
    qk = each(_fdot_nt, lhs, rhs)
    hs0 = each(_fdot_nt, lhs, s0)
    m_ab = each(lambda x: jnp.where(strict, x[:L, :L], 0.0), qk)
    m_ak = each(lambda x: jnp.where(strict, x[:L, L:], 0.0), qk)
    w_ = each(lambda h_, m_, v_: h_[:L] + _fdot(m_, v_), hs0, m_ak, v)
    u = each(lambda w, m_: w + _fdot(m_, w), w_, m_ab)
    p = m_ab
    for _ in range(int(math.log2(L)) - 1):
        p = each(lambda x: _fdot(x, x), p)
        u = each(lambda u_, p_: u_ + _fdot(p_, u_), u, p)
    gtail = each(lambda cl, c_: jnp.exp(cl - c_), c_last, c)
    s_new = each(lambda s_, cl, u_, b_, v_, km_, gt_: s_ * jnp.exp(cl) + _fdot_tn(u_, b_ * gt_)
                 + _fdot_tn(v_, km_ * gt_), s0, c_last, u, b, v, km, gtail)
    for hd in hds:
        st_ref[hd] = s_new[hd]
    y = each(lambda h_, q_, u_, v_: h_[L:] + _fdot(jnp.where(incl, q_[L:, :L], 0.0), u_)
             + _fdot(jnp.where(incl, q_[L:, L:], 0.0), v_), hs0, qk, u, v)

    def finish(y_, r_, km_, v_, g_, sl):
        yc = y_ - jnp.mean(y_, axis=-1, keepdims=True)
        var = jnp.mean(yc * yc, axis=-1, keepdims=True)
        yn = yc * lax.rsqrt(var + GN_EPS) * lnw_ref[:, sl] + lnb_ref[:, sl]
        bonus = jnp.sum(r_ * km_ * rk_ref[:, sl], axis=-1, keepdims=True) * v_
        return (yn + bonus) * g_

    y_ref[...] = jnp.concatenate(each(finish, y, r, km, v, g, sls), axis=1)

    @pl.when(t == pl.num_programs(2) - 1)
    def _():
        s_ref[0] = st_ref[...]


def _wkv(batch, seq, chunk, heads, r, km, kk, v, lw, a, g, r_k, lnw, lnb, s0):
    n_t = seq // chunk
    width = heads * HEAD_SIZE
    tok = pl.BlockSpec((chunk, width), lambda b, h, t: (b * n_t + t, h))
    vec = pl.BlockSpec((1, width), lambda b, h, t: (0, h))
    st = pl.BlockSpec((1, heads, HEAD_SIZE, HEAD_SIZE), lambda b, h, t: (b, h, 0, 0))
    return pl.pallas_call(
        functools.partial(_wkv_kernel, chunk=chunk, heads=heads),
        grid=(batch, N_HEADS // heads, n_t),
        in_specs=[tok] * 7 + [vec] * 3 + [st],
        out_specs=[tok, st],
        out_shape=[jax.ShapeDtypeStruct((batch * seq, D_MODEL), F32),
                   jax.ShapeDtypeStruct((batch, N_HEADS, HEAD_SIZE, HEAD_SIZE), F32)],
        scratch_shapes=[pltpu.VMEM((heads, HEAD_SIZE, HEAD_SIZE), F32)],
        compiler_params=_params("arbitrary", "arbitrary", "arbitrary"),
        name="wkv",
    )(r, km, kk, v, lw, a, g, r_k.reshape(1, D_MODEL), lnw.reshape(1, D_MODEL), lnb.reshape(1, D_MODEL), s0)


SUBLANES = 8


def _col_reduce(x3, op):
    m = x3[0]
    for r in range(1, x3.shape[0]):
        m = op(m, x3[r])
    for shift in (4, 2, 1):
        m = op(m, pltpu.roll(m, shift, axis=0))
    return m


def _top16(s3, want_rank):
    work = s3
    rank = jnp.full(s3.shape, float(PEER_TOPK), F32) if want_rank else None
    vals = []
    for k in range(PEER_TOPK):
        m = _col_reduce(work, jnp.maximum)
        hit = work == m[None]
        if want_rank:
            rank = jnp.where(hit, float(k), rank)
        work = jnp.where(hit, -jnp.inf, work)
        vals.append(m)
    return vals, work, rank


def _rows_from(vals):
    row = lax.broadcasted_iota(jnp.int32, vals[0].shape, 0)
    out = vals[0]
    for s in range(1, SUBLANES):
        out = jnp.where(row == s, vals[s], out)
    return out


def _peer_select_chunk(s1, s2):
    n_keys, w = s1.shape
    s1 = s1.reshape(n_keys // SUBLANES, SUBLANES, w)
    s2 = s2.reshape(n_keys // SUBLANES, SUBLANES, w)
    v1, left1, _ = _top16(s1, False)
    v2, _, rank2 = _top16(s2, True)
    v2_lo, v2_hi, v1_hi = _rows_from(v2[:SUBLANES]), _rows_from(v2[SUBLANES:]), _rows_from(v1[SUBLANES:])
    cand = jnp.stack([v1[0] + v2_lo, v1[0] + v2_hi] + [v1[a] + v2_lo for a in range(1, SUBLANES)]
                     + [v1_hi + v2[0]])
    work = cand
    tau = None
    for _ in range(PEER_TOPK):
        tau = _col_reduce(work, jnp.maximum)
        work = jnp.where(work == tau[None], -jnp.inf, work)
    top = v1[0] + v2[0]
    zsum = _col_reduce(jnp.where(cand >= tau[None], jnp.exp(cand - top[None]), 0.0), jnp.add)
    cnt = jnp.zeros(s1.shape, F32)
    for b in range(PEER_TOPK):
        cnt = cnt + jnp.where(s1 + v2[b][None] >= tau[None], 1.0, 0.0)
    cnt = jnp.where(left1 == -jnp.inf, cnt, 0.0)
    e1 = jnp.exp(s1 - v1[0][None])
    e2 = jnp.exp(s2 - v2[0][None]) / zsum[None]
    return tuple(t.reshape(n_keys, w) for t in (cnt, e1, rank2, e2))


def _peer_select_kernel(x_ref, g_ref, sc_ref, sh_ref, wq_ref, k1_ref, k2_ref,
                        ht_ref, c_ref, e1_ref, rk_ref, e2_ref, q_scr):
    h = _rms_mod(x_ref[...], g_ref[...], sc_ref[0], sh_ref[0])
    ht = h.T.astype(BF16)
    ht_ref[...] = ht
    q_scr[...] = _fdot(wq_ref[...], ht)
    half = D_QUERY // 2
    n_chunks = x_ref.shape[0] // PEER_LANE_CHUNK

    def head_body(hd, carry):
        q1 = q_scr[pl.ds(pl.multiple_of(hd * D_QUERY, D_QUERY), half), :]
        q2 = q_scr[pl.ds(pl.multiple_of(hd * D_QUERY + half, half), half), :]
        s1 = _bdot(k1_ref[hd], q1)
        s2 = _bdot(k2_ref[hd], q2)
        for ch in range(n_chunks):
            ls = slice(ch * PEER_LANE_CHUNK, (ch + 1) * PEER_LANE_CHUNK)
            cnt, e1, rank2, e2 = _peer_select_chunk(s1[:, ls], s2[:, ls])
            c_ref[hd, :, ls] = cnt
            e1_ref[hd, :, ls] = e1
            rk_ref[hd, :, ls] = rank2.astype(BF16)
            e2_ref[hd, :, ls] = e2.astype(BF16)
        return carry

    lax.fori_loop(0, PEER_HEADS, head_body, 0)


def _peer_select(lay, x, g, sc, sh, wq_t, k1, k2):
    tt = lay.tile
    sel_spec = pl.BlockSpec((PEER_HEADS, N_KEYS, tt), lambda i: (0, 0, i))
    sel_shape = lambda dt: jax.ShapeDtypeStruct((PEER_HEADS, N_KEYS, lay.n), dt)
    return pl.pallas_call(
        _peer_select_kernel,
        grid=(lay.n_tiles,),
        in_specs=[lay.tok_spec(), _full_spec((1, D_MODEL)), lay.mod_spec(), lay.mod_spec(),
                  _full_spec((PEER_HEADS * D_QUERY, D_MODEL)),
                  _full_spec((PEER_HEADS, N_KEYS, D_QUERY // 2)), _full_spec((PEER_HEADS, N_KEYS, D_QUERY // 2))],
        out_specs=[pl.BlockSpec((D_MODEL, tt), lambda i: (0, i))] + [sel_spec] * 4,
        out_shape=[jax.ShapeDtypeStruct((D_MODEL, lay.n), BF16),
                   sel_shape(F32), sel_shape(F32), sel_shape(BF16), sel_shape(BF16)],
        scratch_shapes=[pltpu.VMEM((PEER_HEADS * D_QUERY, tt), F32)],
        compiler_params=_params("arbitrary"),
        name="peer_select",
    )(x, g.reshape(1, D_MODEL), sc, sh, wq_t, k1, k2)


def _peer_dense_kernel(ht_ref, c_ref, e1_ref, rk_ref, e2_ref, u_ref, vt_ref, vtp_ref, x_ref, gt_ref, o_ref,
                       acc_ref, coef_ref, z_ref):
    e = pl.program_id(1)
    groups = PEER_EXPERT_TILE // PEER_GROUP
    sr = PEER_STEP_ROWS
    spg = PEER_GROUP // (N_KEYS * sr)
    steps = groups * spg
    piece = D_MODEL // spg
    last_slot = (groups - 1) % 2

    @pl.when(e == 0)
    def _():
        acc_ref[...] = jnp.zeros_like(acc_ref)
        coef_ref[last_slot] = jnp.zeros(coef_ref.shape[1:], BF16)

    def pre_act(step):
        n = sr * N_KEYS
        return _fdot(u_ref[pl.ds(pl.multiple_of(step * n, n), n), :], ht_ref[...])

    z_ref[0] = pre_act(0)

    def group(g, vt_prev):
        slot = g % 2
        for ii in range(spg):
            step = g * spg + ii
            z_ref[(step + 1) % 2] = pre_act(jnp.minimum(step + 1, steps - 1))
            ps = slice(ii * piece, (ii + 1) * piece)
            acc_ref[ps, :] += _fdot(vt_prev(ps), coef_ref[1 - slot])
            for rr in range(sr):
                i = (e * steps + step) * sr + rr
                gate = None
                for hd in range(PEER_HEADS):
                    cnt = c_ref[hd, pl.ds(i, 1), :].astype(BF16)
                    e1 = e1_ref[hd, pl.ds(i, 1), :].astype(BF16)
                    term = jnp.where(rk_ref[hd] < cnt, e2_ref[hd], jnp.zeros((), BF16)) * e1
                    gate = term if gate is None else gate + term
                z = z_ref[step % 2, rr * N_KEYS:(rr + 1) * N_KEYS, :]
                act = 0.5 * z * (1.0 + lax.erf(z * (1.0 / math.sqrt(2.0))))
                lo = (ii * sr + rr) * N_KEYS
                coef_ref[slot, lo:lo + N_KEYS, :] = gate * act.astype(BF16)

    group(0, lambda ps: vtp_ref[0, ps, :])

    def body(g, carry):
        group(g, lambda ps: vt_ref[g - 1, ps, :])
        return carry

    lax.fori_loop(1, groups, body, 0)

    @pl.when(e == pl.num_programs(1) - 1)
    def _():
        acc = acc_ref[...] + _fdot(vt_ref[groups - 1], coef_ref[last_slot])
        o_ref[...] = x_ref[...] + gt_ref[0] * acc.T


def _peer_dense(lay, ht, cnt, e1, rank2, e2, u_b, vt_b, x, gt):
    tt = lay.tile
    te = PEER_EXPERT_TILE
    groups = te // PEER_GROUP
    tpg = lay.tiles_per_group
    sel_spec = pl.BlockSpec((PEER_HEADS, N_KEYS, tt), lambda i, e: (0, 0, i))
    return pl.pallas_call(
        _peer_dense_kernel,
        grid=(lay.n_tiles, N_EXPERTS // te),
        in_specs=[pl.BlockSpec((D_MODEL, tt), lambda i, e: (0, i))] + [sel_spec] * 4 + [
            pl.BlockSpec((te, D_MODEL), lambda i, e: (e, 0)),
            pl.BlockSpec((groups, D_MODEL, PEER_GROUP), lambda i, e: (e, 0, 0)),
            pl.BlockSpec((1, D_MODEL, PEER_GROUP), lambda i, e: (jnp.maximum(e * groups - 1, 0), 0, 0)),
            pl.BlockSpec((tt, D_MODEL), lambda i, e: (i, 0)),
            pl.BlockSpec((1, lay.rows, D_MODEL), lambda i, e: (i // tpg, 0, 0))],
        out_specs=pl.BlockSpec((tt, D_MODEL), lambda i, e: (i, 0)),
        out_shape=jax.ShapeDtypeStruct((lay.n, D_MODEL), F32),
        scratch_shapes=[pltpu.VMEM((D_MODEL, tt), F32), pltpu.VMEM((2, PEER_GROUP, tt), BF16),
                        pltpu.VMEM((2, PEER_STEP_ROWS * N_KEYS, tt), F32)],
        compiler_params=_params("arbitrary", "arbitrary"),
        name="peer_dense",
    )(ht, cnt, e1, rank2, e2, u_b, vt_b, vt_b, x, gt)


def _run(batch, seq, x, m_all, st_conv, st_shift, st_wkv, p):
    lay = _Layout(batch, seq)
    x = x.reshape(lay.n, D_MODEL)
    new_conv, new_shift, new_wkv = [], [], []
    v_first = None
    short = seq < WKV_CHUNK
    for i in range(DEPTH):
        sh_m, sc_m, gt_m, sh_f, sc_f, gt_f = (lay.expand(t) for t in jnp.split(m_all[i], 6, axis=-1))
        j = i // 2
        if i % 2 == 0:
            bg, u = _conv_in(lay, x, p["g_mix"][i], sc_m, sh_m, p["conv_w_in"][j])
            up = jnp.concatenate([st_conv[j], u.reshape(batch, seq, D_MODEL)], axis=1)
            u0 = up[:, 0:seq].reshape(lay.n, D_MODEL)
            u1 = up[:, 1:seq + 1].reshape(lay.n, D_MODEL)
            new_conv.append(up[:, -(CONV_WIDTH - 1):])
            x = _conv_out(lay, bg, u0, u1, u, p["conv_k"][j], p["conv_w_out"][j], x, gt_m)
        else:
            h = _norm_mod(lay, x, p["g_mix"][i], sc_m, sh_m)
            h3 = h.reshape(batch, seq, D_MODEL)
            hs = jnp.concatenate([st_shift[j][:, None, :], h3[:, :-1]], axis=1).reshape(lay.n, D_MODEL)
            new_shift.append(h3[:, -1])
            zero = jnp.zeros((D_MODEL,), F32)
            v0 = p["rwkv_v0"][j - 1] if j > 0 else zero
            vecs = jnp.stack([p["rwkv_w0"][j], p["rwkv_a0"][j], v0, p["rwkv_k_k"][j], p["rwkv_k_a"][j],
                              zero, zero, zero])
            v1 = p["rwkv_v1"][j - 1] if j > 0 else None
            v2 = p["rwkv_v2"][j - 1] if j > 0 else None
            r, km, kk, v, lw, a, g = _rwkv_proj(
                _Layout(batch, seq, RWKV_PROJ_TILE), h, hs, v_first if j > 0 else None, p["rwkv_mix"][j], vecs,
                p["rwkv_wr"][j], p["rwkv_wk"][j], p["rwkv_wv"][j], p["rwkv_w1"][j], p["rwkv_w2"][j],
                p["rwkv_a1"][j], p["rwkv_a2"][j], p["rwkv_g1"][j], p["rwkv_g2"][j], v1, v2)
            if j == 0:
                v_first = v
            scan_in = (r, km, kk, v, lw, a, g)
            if short:
                pad = lambda t: jnp.pad(t.reshape(batch, seq, D_MODEL),
                                        ((0, 0), (0, WKV_SHORT_CHUNK - seq), (0, 0))).reshape(-1, D_MODEL)
                scan_in = tuple(pad(t) for t in scan_in)
                yg, s_new = _wkv(batch, WKV_SHORT_CHUNK, WKV_SHORT_CHUNK, N_HEADS, *scan_in,
                                 p["rwkv_r_k"][j], p["rwkv_lnw"][j], p["rwkv_lnb"][j], st_wkv[j])
                yg = yg.reshape(batch, WKV_SHORT_CHUNK, D_MODEL)[:, :seq].reshape(lay.n, D_MODEL)
            else:
                yg, s_new = _wkv(batch, seq, WKV_CHUNK, WKV_HEADS_PER_STEP, *scan_in,
                                 p["rwkv_r_k"][j], p["rwkv_lnw"][j], p["rwkv_lnb"][j], st_wkv[j])
            new_wkv.append(s_new)
            x = _mm_res(lay, yg, p["rwkv_wo"][j], x, gt_m)
        ht, cnt, e1, rank2, e2 = _peer_select(lay, x, p["g_ffn"][i], sc_f, sh_f,
                                              p["wq_t"][i], p["k1_b"][i], p["k2_b"][i])
        x = _peer_dense(lay, ht, cnt, e1, rank2, e2, p["u_b"][i], p["vt_b"][i], x, gt_f)
    y = _final_norm(lay, x, p["g_final"]).reshape(batch, seq, D_MODEL)
    return y, jnp.stack(new_conv), jnp.stack(new_shift), jnp.stack(new_wkv)


def kernel(x_prompt, x_sample, state_conv, state_shift, state_wkv, c_prompt, c_sample, w_ada, b_ada, g_mix, g_ffn, g_final, conv_w_in, conv_k, conv_w_out, rwkv_mix, rwkv_wr, rwkv_wk, rwkv_wv, rwkv_wo, rwkv_w0, rwkv_w1, rwkv_w2, rwkv_a0, rwkv_a1, rwkv_a2, rwkv_v0, rwkv_v1, rwkv_v2, rwkv_g1, rwkv_g2, rwkv_k_k, rwkv_k_a, rwkv_r_k, rwkv_lnw, rwkv_lnb, peer_wq, peer_k1, peer_k2, peer_u, peer_v):
    p = dict(
        g_mix=g_mix, g_ffn=g_ffn, g_final=g_final, conv_w_in=conv_w_in, conv_k=conv_k, conv_w_out=conv_w_out,
        rwkv_mix=rwkv_mix, rwkv_wr=rwkv_wr, rwkv_wk=rwkv_wk, rwkv_wv=rwkv_wv, rwkv_wo=rwkv_wo,
        rwkv_w0=rwkv_w0, rwkv_w1=rwkv_w1, rwkv_w2=rwkv_w2, rwkv_a0=rwkv_a0, rwkv_a1=rwkv_a1, rwkv_a2=rwkv_a2,
        rwkv_v0=rwkv_v0, rwkv_v1=rwkv_v1, rwkv_v2=rwkv_v2, rwkv_g1=rwkv_g1, rwkv_g2=rwkv_g2,
        rwkv_k_k=rwkv_k_k, rwkv_k_a=rwkv_k_a, rwkv_r_k=rwkv_r_k, rwkv_lnw=rwkv_lnw, rwkv_lnb=rwkv_lnb,
        wq_t=jnp.swapaxes(peer_wq, 1, 2).astype(BF16),
        k1_b=peer_k1.astype(BF16), k2_b=peer_k2.astype(BF16),
        u_b=peer_u.astype(BF16),
        vt_b=jnp.swapaxes(peer_v.astype(BF16).reshape(DEPTH, N_EXPERTS // PEER_GROUP, PEER_GROUP, D_MODEL), 2, 3),
    )
    n_prompt, seq_prompt = x_prompt.shape[0], x_prompt.shape[1]
    n_sample, seq_sample = x_sample.shape[0], x_sample.shape[1]
    m_all = _adaln(jnp.concatenate([c_prompt, c_sample], axis=0), w_ada, b_ada)
    zeros = lambda *s: jnp.zeros(s, F32)
    n_conv, n_rwkv = (DEPTH + 1) // 2, DEPTH // 2
    y_p, p_conv, p_shift, p_wkv = _run(
        n_prompt, seq_prompt, x_prompt, m_all[:, :n_prompt],
        zeros(n_conv, n_prompt, CONV_WIDTH - 1, D_MODEL), zeros(n_rwkv, n_prompt, D_MODEL),
        zeros(n_rwkv, n_prompt, N_HEADS, HEAD_SIZE, HEAD_SIZE), p)
    y_s, s_conv, s_shift, s_wkv = _run(
        n_sample, seq_sample, x_sample, m_all[:, n_prompt:], state_conv, state_shift, state_wkv, p)
    return (y_p, y_s, p_conv, p_shift, p_wkv, s_conv, s_shift, s_wkv)
```

```python
import functools
import math

import jax
import jax.numpy as jnp
from jax import lax
from jax.experimental import pallas as pl
from jax.experimental.pallas import tpu as pltpu

D_MODEL = 1024
DEPTH = 4
CONV_WIDTH = 3
HEAD_SIZE = 64
N_HEADS = D_MODEL // HEAD_SIZE
N_KEYS = 128
N_EXPERTS = N_KEYS * N_KEYS
PEER_HEADS = 8
PEER_TOPK = 16
D_QUERY = 256
RMS_EPS = 1e-6
GN_EPS = 64e-5

LORA_PAD = 128
TOKEN_TILE = 512
RWKV_PROJ_TILE = 256
PEER_EXPERT_TILE = 2048
PEER_GROUP = 512
PEER_STEP_ROWS = 4
PEER_LANE_CHUNK = 256
WKV_CHUNK = 64
WKV_SHORT_CHUNK = 8
WKV_HEADS_PER_STEP = 8
VMEM_LIMIT = 56 * 1024 * 1024

F32 = jnp.float32
BF16 = jnp.bfloat16


def _params(*sem):
    return pltpu.CompilerParams(dimension_semantics=sem, vmem_limit_bytes=VMEM_LIMIT)


def _sigmoid(x):
    return 1.0 / (1.0 + jnp.exp(-x))


def _rms_mod(x, g, sc, sh):
    y = x * lax.rsqrt(jnp.mean(x * x, axis=-1, keepdims=True) + RMS_EPS)
    return (y * g) * (1.0 + sc) + sh


def _bdot(a, b):
    return jnp.dot(a.astype(BF16), b.astype(BF16), preferred_element_type=F32)


def _fdot(a, b):
    return jnp.dot(a, b, preferred_element_type=F32)


def _fdot_nt(a, b):
    return lax.dot_general(a, b, (((1,), (1,)), ((), ())), preferred_element_type=F32)


def _fdot_tn(a, b):
    return lax.dot_general(a, b, (((0,), (0,)), ((), ())), preferred_element_type=F32)


class _Layout:
    def __init__(self, batch, seq, tile=TOKEN_TILE):
        self.batch, self.seq = batch, seq
        self.n = batch * seq
        self.tile = min(tile, self.n)
        assert self.n % self.tile == 0
        self.n_tiles = self.n // self.tile
        if seq % self.tile == 0:
            self.rows = 1
            self.tiles_per_group = seq // self.tile
        else:
            assert self.tile % seq == 0
            self.rows = self.tile
            self.tiles_per_group = 1

    def expand(self, m):
        if self.rows == 1:
            return m[:, None, :]
        return jnp.repeat(m, self.seq, axis=0).reshape(self.n_tiles, self.tile, m.shape[-1])

    def mod_spec(self):
        tpg = self.tiles_per_group
        return pl.BlockSpec((1, self.rows, D_MODEL), lambda i: (i // tpg, 0, 0))

    def tok_spec(self, width=D_MODEL):
        return pl.BlockSpec((self.tile, width), lambda i: (i, 0))


def _full_spec(shape):
    nd = len(shape)
    return pl.BlockSpec(shape, lambda *_: (0,) * nd)


def _adaln_kernel(c_ref, w_ref, b_ref, o_ref):
    c = c_ref[...]
    cs = c * _sigmoid(c)
    o_ref[0] = _fdot(cs, w_ref[0]) + b_ref[0]


def _adaln(c, w_ada, b_ada):
    nb = c.shape[0]
    tn = 1536
    return pl.pallas_call(
        _adaln_kernel,
        grid=(DEPTH, 6 * D_MODEL // tn),
        in_specs=[
            pl.BlockSpec((nb, D_MODEL), lambda l, j: (0, 0)),
            pl.BlockSpec((1, D_MODEL, tn), lambda l, j: (l, 0, j)),
            pl.BlockSpec((1, 1, tn), lambda l, j: (l, 0, j)),
        ],
        out_specs=pl.BlockSpec((1, nb, tn), lambda l, j: (l, 0, j)),
        out_shape=jax.ShapeDtypeStruct((DEPTH, nb, 6 * D_MODEL), F32),
        compiler_params=_params("arbitrary", "arbitrary"),
        name="adaln",
    )(c, w_ada, b_ada.reshape(DEPTH, 1, 6 * D_MODEL))


def _norm_mod_kernel(x_ref, g_ref, sc_ref, sh_ref, o_ref):
    o_ref[...] = _rms_mod(x_ref[...], g_ref[...], sc_ref[0], sh_ref[0])


def _norm_mod(lay, x, g, sc, sh):
    return pl.pallas_call(
        _norm_mod_kernel,
        grid=(lay.n_tiles,),
        in_specs=[lay.tok_spec(), _full_spec((1, D_MODEL)), lay.mod_spec(), lay.mod_spec()],
        out_specs=lay.tok_spec(),
        out_shape=jax.ShapeDtypeStruct((lay.n, D_MODEL), F32),
        compiler_params=_params("arbitrary"),
        name="norm_mod",
    )(x, g.reshape(1, D_MODEL), sc, sh)


def _final_norm_kernel(x_ref, g_ref, o_ref):
    x = x_ref[...]
    o_ref[...] = x * lax.rsqrt(jnp.mean(x * x, axis=-1, keepdims=True) + RMS_EPS) * g_ref[...]


def _final_norm(lay, x, g):
    return pl.pallas_call(
        _final_norm_kernel,
        grid=(lay.n_tiles,),
        in_specs=[lay.tok_spec(), _full_spec((1, D_MODEL))],
        out_specs=lay.tok_spec(),
        out_shape=jax.ShapeDtypeStruct((lay.n, D_MODEL), F32),
        compiler_params=_params("arbitrary"),
        name="final_norm",
    )(x, g.reshape(1, D_MODEL))


def _conv_in_kernel(x_ref, g_ref, sc_ref, sh_ref, w_ref, bg_ref, u_ref):
    h = _rms_mod(x_ref[...], g_ref[...], sc_ref[0], sh_ref[0]).astype(BF16)
    d = D_MODEL
    bg_ref[...] = _fdot(h, w_ref[:, 0:d])
    u_ref[...] = _fdot(h, w_ref[:, d:2 * d]) * _fdot(h, w_ref[:, 2 * d:3 * d])


def _conv_in(lay, x, g, sc, sh, w_in):
    shp = jax.ShapeDtypeStruct((lay.n, D_MODEL), F32)
    return pl.pallas_call(
        _conv_in_kernel,
        grid=(lay.n_tiles,),
        in_specs=[lay.tok_spec(), _full_spec((1, D_MODEL)), lay.mod_spec(), lay.mod_spec(),
                  _full_spec((D_MODEL, 3 * D_MODEL))],
        out_specs=[lay.tok_spec(), lay.tok_spec()],
        out_shape=[shp, shp],
        compiler_params=_params("arbitrary"),
        name="conv_in",
    )(x, g.reshape(1, D_MODEL), sc, sh, w_in.astype(BF16))


def _conv_out_kernel(bg_ref, u0_ref, u1_ref, u2_ref, ck_ref, w_ref, x_ref, gt_ref, o_ref):
    conv = u0_ref[...] * ck_ref[0:1, :] + u1_ref[...] * ck_ref[1:2, :] + u2_ref[...] * ck_ref[2:3, :]
    y = _bdot(bg_ref[...] * conv, w_ref[...])
    o_ref[...] = x_ref[...] + gt_ref[0] * y


def _conv_out(lay, bg, u0, u1, u2, conv_k, w_out, x, gt):
    return pl.pallas_call(
        _conv_out_kernel,
        grid=(lay.n_tiles,),
        in_specs=[lay.tok_spec()] * 4 + [_full_spec((CONV_WIDTH, D_MODEL)), _full_spec((D_MODEL, D_MODEL)),
                                         lay.tok_spec(), lay.mod_spec()],
        out_specs=lay.tok_spec(),
        out_shape=jax.ShapeDtypeStruct((lay.n, D_MODEL), F32),
        compiler_params=_params("arbitrary"),
        name="conv_out",
    )(bg, u0, u1, u2, conv_k, w_out.astype(BF16), x, gt)


def _mm_res_kernel(a_ref, w_ref, x_ref, gt_ref, o_ref):
    o_ref[...] = x_ref[...] + gt_ref[0] * _bdot(a_ref[...], w_ref[...])


def _mm_res(lay, a, w, x, gt):
    return pl.pallas_call(
        _mm_res_kernel,
        grid=(lay.n_tiles,),
        in_specs=[lay.tok_spec(), _full_spec((D_MODEL, D_MODEL)), lay.tok_spec(), lay.mod_spec()],
        out_specs=lay.tok_spec(),
        out_shape=jax.ShapeDtypeStruct((lay.n, D_MODEL), F32),
        compiler_params=_params("arbitrary"),
        name="mm_res",
    )(a, w.astype(BF16), x, gt)


def _rwkv_proj_kernel(*refs, has_vres):
    if has_vres:
        (h_ref, hs_ref, vf_ref, mix_ref, vec_ref, wr_ref, wk_ref, wv_ref, w1_ref, w2_ref, a1_ref, a2_ref,
         g1_ref, g2_ref, v1_ref, v2_ref, r_ref, km_ref, kk_ref, v_ref, lw_ref, a_ref, g_ref) = refs
    else:
        (h_ref, hs_ref, mix_ref, vec_ref, wr_ref, wk_ref, wv_ref, w1_ref, w2_ref, a1_ref, a2_ref,
         g1_ref, g2_ref, r_ref, km_ref, kk_ref, v_ref, lw_ref, a_ref, g_ref) = refs
    h = h_ref[...]
    xx = hs_ref[...] - h
    mixed = lambda i: (h + xx * mix_ref[i:i + 1, :]).astype(BF16)
    w0, a0, v0, k_k, k_a = (vec_ref[i:i + 1, :] for i in range(5))

    r_ref[...] = _fdot(mixed(0), wr_ref[...])
    wl = w0 + _bdot(jnp.tanh(_fdot(mixed(1), w1_ref[...])), w2_ref[...])
    z = -wl
    wlog = -(jnp.maximum(z, 0.0) + jnp.log(1.0 + jnp.exp(-jnp.abs(z)))) - 0.5
    lw_ref[...] = -jnp.exp(wlog)
    k = _fdot(mixed(2), wk_ref[...])
    xv = mixed(3)
    v = _fdot(xv, wv_ref[...])
    if has_vres:
        v = v + (vf_ref[...] - v) * _sigmoid(v0 + _bdot(_fdot(xv, v1_ref[...]), v2_ref[...]))
    v_ref[...] = v
    a = _sigmoid(a0 + _bdot(_fdot(mixed(4), a1_ref[...]), a2_ref[...]))
    a_ref[...] = a
    g_ref[...] = _bdot(_sigmoid(_fdot(mixed(5), g1_ref[...])), g2_ref[...])
    kk_ref[...] = k * k_k
    km_ref[...] = k * (1.0 + (a - 1.0) * k_a)


def _pad_cols(w):
    return jnp.pad(w, ((0, 0), (0, LORA_PAD - w.shape[1]))).astype(BF16)


def _pad_rows(w):
    return jnp.pad(w, ((0, LORA_PAD - w.shape[0]), (0, 0))).astype(BF16)


def _rwkv_proj(lay, h, hs, v_first, mix, vecs, wr, wk, wv, w1, w2, a1, a2, g1, g2, v1, v2):
    has_vres = v_first is not None
    sq = _full_spec((D_MODEL, D_MODEL))
    down = _full_spec((D_MODEL, LORA_PAD))
    up = _full_spec((LORA_PAD, D_MODEL))
    in_specs = [lay.tok_spec(), lay.tok_spec()] + ([lay.tok_spec()] if has_vres else [])
    in_specs += [_full_spec((6, D_MODEL)), _full_spec((8, D_MODEL)), sq, sq, sq, down, up, down, up, down, up]
    args = [h, hs] + ([v_first] if has_vres else [])
    args += [mix, vecs, wr.astype(BF16), wk.astype(BF16), wv.astype(BF16), _pad_cols(w1), _pad_rows(w2),
             _pad_cols(a1), _pad_rows(a2), _pad_cols(g1), _pad_rows(g2)]
    if has_vres:
        in_specs += [down, up]
        args += [_pad_cols(v1), _pad_rows(v2)]
    shp = jax.ShapeDtypeStruct((lay.n, D_MODEL), F32)
    return pl.pallas_call(
        functools.partial(_rwkv_proj_kernel, has_vres=has_vres),
        grid=(lay.n_tiles,),
        in_specs=in_specs,
        out_specs=[lay.tok_spec()] * 7,
        out_shape=[shp] * 7,
        compiler_params=_params("arbitrary"),
        name="rwkv_proj",
    )(*args)


def _wkv_kernel(r_ref, km_ref, kk_ref, v_ref, lw_ref, a_ref, g_ref, rk_ref, lnw_ref, lnb_ref, s0_ref,
                y_ref, s_ref, st_ref, *, chunk, heads):
    t = pl.program_id(2)
    L = chunk

    @pl.when(t == 0)
    def _():
        st_ref[...] = s0_ref[0]

    row = lax.broadcasted_iota(jnp.int32, (L, L), 0)
    col = lax.broadcasted_iota(jnp.int32, (L, L), 1)
    strict = row > col
    incl = row >= col
    tri = incl.astype(F32)

    hds = range(heads)
    each = lambda f, *cols: [f(*xs) for xs in zip(*cols)]
    sls = [slice(hd * HEAD_SIZE, (hd + 1) * HEAD_SIZE) for hd in hds]
    r, km, kkr, v, lw, asig, g = ([ref[:, sl] for sl in sls]
                                  for ref in (r_ref, km_ref, kk_ref, v_ref, lw_ref, a_ref, g_ref))
    s0 = [st_ref[hd] for hd in hds]
    c = each(lambda x: _fdot(tri, x), lw)
    kk = each(lambda x: x * lax.rsqrt(jnp.maximum(jnp.sum(x * x, axis=-1, keepdims=True), 1e-24)), kkr)
    b = each(lambda x, y: x * y, kk, asig)
    c_last = each(lambda x: x[L - 1:L, :], c)
    ginv = each(lambda x: jnp.exp(-x), c)
    lhs = each(lambda kk_, c_, lw_, r_: jnp.concatenate([-kk_ * jnp.exp(c_ - lw_), r_ * jnp.exp(c_)], axis=0),
               kk, c, lw, r)
    rhs = each(lambda b_, km_, gi: jnp.concatenate([b_ * gi, km_ * gi], axis=0), b, km, ginv)
    qk = each(_fdot_nt, lhs, rhs)
    hs0 = each(_fdot_nt, lhs, s0)
    m_ab = each(lambda x: jnp.where(strict, x[:L, :L], 0.0), qk)
    m_ak = each(lambda x: jnp.where(strict, x[:L, L:], 0.0), qk)
    w_ = each(lambda h_, m_, v_: h_[:L] + _fdot(m_, v_), hs0, m_ak, v)
    u = each(lambda w, m_: w + _fdot(m_, w), w_, m_ab)
    p = m_ab
    for _ in range(int(math.log2(L)) - 1):
        p = each(lambda x: _fdot(x, x), p)
        u = each(lambda u_, p_: u_ + _fdot(p_, u_), u, p)
    gtail = each(lambda cl, c_: jnp.exp(cl - c_), c_last, c)
    s_new = each(lambda s_, cl, u_, b_, v_, km_, gt_: s_ * jnp.exp(cl) + _fdot_tn(u_, b_ * gt_)
                 + _fdot_tn(v_, km_ * gt_), s0, c_last, u, b, v, km, gtail)
    for hd in hds:
        st_ref[hd] = s_new[hd]
    y = each(lambda h_, q_, u_, v_: h_[L:] + _fdot(jnp.where(incl, q_[L:, :L], 0.0), u_)
             + _fdot(jnp.where(incl, q_[L:, L:], 0.0), v_), hs0, qk, u, v)

    def finish(y_, r_, km_, v_, g_, sl):
        yc = y_ - jnp.mean(y_, axis=-1, keepdims=True)
        var = jnp.mean(yc * yc, axis=-1, keepdims=True)
        yn = yc * lax.rsqrt(var + GN_EPS) * lnw_ref[:, sl] + lnb_ref[:, sl]
        bonus = jnp.sum(r_ * km_ * rk_ref[:, sl], axis=-1, keepdims=True) * v_
        return (yn + bonus) * g_

    y_ref[...] = jnp.concatenate(each(finish, y, r, km, v, g, sls), axis=1)

    @pl.when(t == pl.num_programs(2) - 1)
    def _():
        s_ref[0] = st_ref[...]


def _wkv(batch, seq, chunk, heads, r, km, kk, v, lw, a, g, r_k, lnw, lnb, s0):
    n_t = seq // chunk
    width = heads * HEAD_SIZE
    tok = pl.BlockSpec((chunk, width), lambda b, h, t: (b * n_t + t, h))
    vec = pl.BlockSpec((1, width), lambda b, h, t: (0, h))
    st = pl.BlockSpec((1, heads, HEAD_SIZE, HEAD_SIZE), lambda b, h, t: (b, h, 0, 0))
    return pl.pallas_call(
        functools.partial(_wkv_kernel, chunk=chunk, heads=heads),
        grid=(batch, N_HEADS // heads, n_t),
        in_specs=[tok] * 7 + [vec] * 3 + [st],
        out_specs=[tok, st],
        out_shape=[jax.ShapeDtypeStruct((batch * seq, D_MODEL), F32),
                   jax.ShapeDtypeStruct((batch, N_HEADS, HEAD_SIZE, HEAD_SIZE), F32)],
        scratch_shapes=[pltpu.VMEM((heads, HEAD_SIZE, HEAD_SIZE), F32)],
        compiler_params=_params("arbitrary", "arbitrary", "arbitrary"),
        name="wkv",
    )(r, km, kk, v, lw, a, g, r_k.reshape(1, D_MODEL), lnw.reshape(1, D_MODEL), lnb.reshape(1, D_MODEL), s0)


SUBLANES = 8


def _col_reduce(x3, op):
    m = x3[0]
    for r in range(1, x3.shape[0]):
        m = op(m, x3[r])
    for shift in (4, 2, 1):
        m = op(m, pltpu.roll(m, shift, axis=0))
    return m


def _top16(s3, want_rank):
    work = s3
    rank = jnp.full(s3.shape, float(PEER_TOPK), F32) if want_rank else None
    vals = []
    for k in range(PEER_TOPK):
        m = _col_reduce(work, jnp.maximum)
        hit = work == m[None]
        if want_rank:
            rank = jnp.where(hit, float(k), rank)
        work = jnp.where(hit, -jnp.inf, work)
        vals.append(m)
    return vals, work, rank


def _rows_from(vals):
    row = lax.broadcasted_iota(jnp.int32, vals[0].shape, 0)
    out = vals[0]
    for s in range(1, SUBLANES):
        out = jnp.where(row == s, vals[s], out)
    return out


def _peer_select_chunk(s1, s2):
    n_keys, w = s1.shape
    s1 = s1.reshape(n_keys // SUBLANES, SUBLANES, w)
    s2 = s2.reshape(n_keys // SUBLANES, SUBLANES, w)
    v1, left1, _ = _top16(s1, False)
    v2, _, rank2 = _top16(s2, True)
    v2_lo, v2_hi, v1_hi = _rows_from(v2[:SUBLANES]), _rows_from(v2[SUBLANES:]), _rows_from(v1[SUBLANES:])
    cand = jnp.stack([v1[0] + v2_lo, v1[0] + v2_hi] + [v1[a] + v2_lo for a in range(1, SUBLANES)]
                     + [v1_hi + v2[0]])
    work = cand
    tau = None
    for _ in range(PEER_TOPK):
        tau = _col_reduce(work, jnp.maximum)
        work = jnp.where(work == tau[None], -jnp.inf, work)
    top = v1[0] + v2[0]
    zsum = _col_reduce(jnp.where(cand >= tau[None], jnp.exp(cand - top[None]), 0.0), jnp.add)
    cnt = jnp.zeros(s1.shape, F32)
    for b in range(PEER_TOPK):
        cnt = cnt + jnp.where(s1 + v2[b][None] >= tau[None], 1.0, 0.0)
    cnt = jnp.where(left1 == -jnp.inf, cnt, 0.0)
    e1 = jnp.exp(s1 - v1[0][None])
    e2 = jnp.exp(s2 - v2[0][None]) / zsum[None]
    return tuple(t.reshape(n_keys, w) for t in (cnt, e1, rank2, e2))


def _peer_select_kernel(x_ref, g_ref, sc_ref, sh_ref, wq_ref, k1_ref, k2_ref,
                        ht_ref, c_ref, e1_ref, rk_ref, e2_ref, q_scr):
    h = _rms_mod(x_ref[...], g_ref[...], sc_ref[0], sh_ref[0])
    ht = h.T.astype(BF16)
    ht_ref[...] = ht
    q_scr[...] = _fdot(wq_ref[...], ht)
    half = D_QUERY // 2
    n_chunks = x_ref.shape[0] // PEER_LANE_CHUNK

    def head_body(hd, carry):
        q1 = q_scr[pl.ds(pl.multiple_of(hd * D_QUERY, D_QUERY), half), :]
        q2 = q_scr[pl.ds(pl.multiple_of(hd * D_QUERY + half, half), half), :]
        s1 = _bdot(k1_ref[hd], q1)
        s2 = _bdot(k2_ref[hd], q2)
        for ch in range(n_chunks):
            ls = slice(ch * PEER_LANE_CHUNK, (ch + 1) * PEER_LANE_CHUNK)
            cnt, e1, rank2, e2 = _peer_select_chunk(s1[:, ls], s2[:, ls])
            c_ref[hd, :, ls] = cnt
            e1_ref[hd, :, ls] = e1
            rk_ref[hd, :, ls] = rank2.astype(BF16)
            e2_ref[hd, :, ls] = e2.astype(BF16)
        return carry

    lax.fori_loop(0, PEER_HEADS, head_body, 0)


def _peer_select(lay, x, g, sc, sh, wq_t, k1, k2):
    tt = lay.tile
    sel_spec = pl.BlockSpec((PEER_HEADS, N_KEYS, tt), lambda i: (0, 0, i))
    sel_shape = lambda dt: jax.ShapeDtypeStruct((PEER_HEADS, N_KEYS, lay.n), dt)
    return pl.pallas_call(
        _peer_select_kernel,
        grid=(lay.n_tiles,),
        in_specs=[lay.tok_spec(), _full_spec((1, D_MODEL)), lay.mod_spec(), lay.mod_spec(),
                  _full_spec((PEER_HEADS * D_QUERY, D_MODEL)),
                  _full_spec((PEER_HEADS, N_KEYS, D_QUERY // 2)), _full_spec((PEER_HEADS, N_KEYS, D_QUERY // 2))],
        out_specs=[pl.BlockSpec((D_MODEL, tt), lambda i: (0, i))] + [sel_spec] * 4,
        out_shape=[jax.ShapeDtypeStruct((D_MODEL, lay.n), BF16),
                   sel_shape(F32), sel_shape(F32), sel_shape(BF16), sel_shape(BF16)],
        scratch_shapes=[pltpu.VMEM((PEER_HEADS * D_QUERY, tt), F32)],
        compiler_params=_params("arbitrary"),
        name="peer_select",
    )(x, g.reshape(1, D_MODEL), sc, sh, wq_t, k1, k2)


def _peer_dense_kernel(*refs):
    groups = PEER_EXPERT_TILE // PEER_GROUP
    ht_ref, c_ref, e1_ref, rk_ref, e2_ref = refs[:5]
    u_refs = refs[5:5 + groups]
    vt_refs = refs[5 + groups:5 + 2 * groups]
    x_ref, gt_ref, o_ref, acc_ref, coef_ref, z_ref = refs[5 + 2 * groups:]
    e = pl.program_id(1)
    sr = PEER_STEP_ROWS
    spg = PEER_GROUP // (N_KEYS * sr)
    piece = D_MODEL // spg
    last_slot = (groups - 1) % 2

    @pl.when(e == 0)
    def _():
        acc_ref[...] = jnp.zeros_like(acc_ref)

    def pre_act(step):
        g, ii = divmod(step, spg)
        return _fdot(u_refs[g][0, ii * sr * N_KEYS:(ii + 1) * sr * N_KEYS, :], ht_ref[...])

    z_ref[0] = pre_act(0)
    for g in range(groups):
        slot = g % 2
        for ii in range(spg):
            step = g * spg + ii
            if step + 1 < groups * spg:
                z_ref[(step + 1) % 2] = pre_act(step + 1)
            if g > 0:
                ps = slice(ii * piece, (ii + 1) * piece)
                acc_ref[ps, :] += _fdot(vt_refs[g - 1][0, ps, :], coef_ref[1 - slot])
            for rr in range(sr):
                row = step * sr + rr
                gate = None
                for hd in range(PEER_HEADS):
                    cnt = c_ref[hd, row:row + 1, :].astype(BF16)
                    e1 = e1_ref[hd, row:row + 1, :].astype(BF16)
                    term = jnp.where(rk_ref[hd] < cnt, e2_ref[hd], jnp.zeros((), BF16)) * e1
                    gate = term if gate is None else gate + term
                z = z_ref[step % 2, rr * N_KEYS:(rr + 1) * N_KEYS, :]
                act = 0.5 * z * (1.0 + lax.erf(z * (1.0 / math.sqrt(2.0))))
                lo = (ii * sr + rr) * N_KEYS
                coef_ref[slot, lo:lo + N_KEYS, :] = gate * act.astype(BF16)

    acc_ref[...] += _fdot(vt_refs[groups - 1][0], coef_ref[last_slot])

    @pl.when(e == pl.num_programs(1) - 1)
    def _():
        o_ref[...] = x_ref[...] + gt_ref[0] * acc_ref[...].T


def _peer_weight_parts(peer_u, peer_v):
    groups = PEER_EXPERT_TILE // PEER_GROUP
    tiles = N_EXPERTS // PEER_EXPERT_TILE
    u5 = peer_u.astype(BF16).reshape(DEPTH, tiles, groups, PEER_GROUP, D_MODEL)
    v5 = jnp.swapaxes(peer_v.astype(BF16).reshape(DEPTH, tiles, groups, PEER_GROUP, D_MODEL), 3, 4)
    u_parts = [u5[:, :, k] for k in range(groups)]
    vt_parts = [v5[:, :, k] for k in range(groups)]
    return u_parts, vt_parts


def _peer_dense(lay, ht, cnt, e1, rank2, e2, u_parts, vt_parts, x, gt):
    tt = lay.tile
    groups = PEER_EXPERT_TILE // PEER_GROUP
    rows_per_tile = PEER_EXPERT_TILE // N_KEYS
    tpg = lay.tiles_per_group
    row_spec = pl.BlockSpec((PEER_HEADS, rows_per_tile, tt), lambda i, e: (0, e, i))
    key_spec = pl.BlockSpec((PEER_HEADS, N_KEYS, tt), lambda i, e: (0, 0, i))
    u_spec = pl.BlockSpec((1, PEER_GROUP, D_MODEL), lambda i, e: (e, 0, 0))
    vt_spec = pl.BlockSpec((1, D_MODEL, PEER_GROUP), lambda i, e: (e, 0, 0))
    return pl.pallas_call(
        _peer_dense_kernel,
        grid=(lay.n_tiles, N_EXPERTS // PEER_EXPERT_TILE),
        in_specs=[pl.BlockSpec((D_MODEL, tt), lambda i, e: (0, i)), row_spec, row_spec, key_spec, key_spec]
        + [u_spec] * groups + [vt_spec] * groups + [
            pl.BlockSpec((tt, D_MODEL), lambda i, e: (i, 0)),
            pl.BlockSpec((1, lay.rows, D_MODEL), lambda i, e: (i // tpg, 0, 0))],
        out_specs=pl.BlockSpec((tt, D_MODEL), lambda i, e: (i, 0)),
        out_shape=jax.ShapeDtypeStruct((lay.n, D_MODEL), F32),
        scratch_shapes=[pltpu.VMEM((D_MODEL, tt), F32), pltpu.VMEM((2, PEER_GROUP, tt), BF16),
                        pltpu.VMEM((2, PEER_STEP_ROWS * N_KEYS, tt), F32)],
        compiler_params=_params("arbitrary", "arbitrary"),
        name="peer_dense",
    )(ht, cnt, e1, rank2, e2, *u_parts, *vt_parts, x, gt)


def _run(batch, seq, x, m_all, st_conv, st_shift, st_wkv, p):
    lay = _Layout(batch, seq)
    x = x.reshape(lay.n, D_MODEL)
    new_conv, new_shift, new_wkv = [], [], []
    v_first = None
    short = seq < WKV_CHUNK
    for i in range(DEPTH):
        sh_m, sc_m, gt_m, sh_f, sc_f, gt_f = (lay.expand(t) for t in jnp.split(m_all[i], 6, axis=-1))
        j = i // 2
        if i % 2 == 0:
            bg, u = _conv_in(lay, x, p["g_mix"][i], sc_m, sh_m, p["conv_w_in"][j])
            up = jnp.concatenate([st_conv[j], u.reshape(batch, seq, D_MODEL)], axis=1)
            u0 = up[:, 0:seq].reshape(lay.n, D_MODEL)
            u1 = up[:, 1:seq + 1].reshape(lay.n, D_MODEL)
            new_conv.append(up[:, -(CONV_WIDTH - 1):])
            x = _conv_out(lay, bg, u0, u1, u, p["conv_k"][j], p["conv_w_out"][j], x, gt_m)
        else:
            h = _norm_mod(lay, x, p["g_mix"][i], sc_m, sh_m)
            h3 = h.reshape(batch, seq, D_MODEL)
            hs = jnp.concatenate([st_shift[j][:, None, :], h3[:, :-1]], axis=1).reshape(lay.n, D_MODEL)
            new_shift.append(h3[:, -1])
            zero = jnp.zeros((D_MODEL,), F32)
            v0 = p["rwkv_v0"][j - 1] if j > 0 else zero
            vecs = jnp.stack([p["rwkv_w0"][j], p["rwkv_a0"][j], v0, p["rwkv_k_k"][j], p["rwkv_k_a"][j],
                              zero, zero, zero])
            v1 = p["rwkv_v1"][j - 1] if j > 0 else None
            v2 = p["rwkv_v2"][j - 1] if j > 0 else None
            r, km, kk, v, lw, a, g = _rwkv_proj(
                _Layout(batch, seq, RWKV_PROJ_TILE), h, hs, v_first if j > 0 else None, p["rwkv_mix"][j], vecs,
                p["rwkv_wr"][j], p["rwkv_wk"][j], p["rwkv_wv"][j], p["rwkv_w1"][j], p["rwkv_w2"][j],
                p["rwkv_a1"][j], p["rwkv_a2"][j], p["rwkv_g1"][j], p["rwkv_g2"][j], v1, v2)
            if j == 0:
                v_first = v
            scan_in = (r, km, kk, v, lw, a, g)
            if short:
                pad = lambda t: jnp.pad(t.reshape(batch, seq, D_MODEL),
                                        ((0, 0), (0, WKV_SHORT_CHUNK - seq), (0, 0))).reshape(-1, D_MODEL)
                scan_in = tuple(pad(t) for t in scan_in)
                yg, s_new = _wkv(batch, WKV_SHORT_CHUNK, WKV_SHORT_CHUNK, N_HEADS, *scan_in,
                                 p["rwkv_r_k"][j], p["rwkv_lnw"][j], p["rwkv_lnb"][j], st_wkv[j])
                yg = yg.reshape(batch, WKV_SHORT_CHUNK, D_MODEL)[:, :seq].reshape(lay.n, D_MODEL)
            else:
                yg, s_new = _wkv(batch, seq, WKV_CHUNK, WKV_HEADS_PER_STEP, *scan_in,
                                 p["rwkv_r_k"][j], p["rwkv_lnw"][j], p["rwkv_lnb"][j], st_wkv[j])
            new_wkv.append(s_new)
            x = _mm_res(lay, yg, p["rwkv_wo"][j], x, gt_m)
        ht, cnt, e1, rank2, e2 = _peer_select(lay, x, p["g_ffn"][i], sc_f, sh_f,
                                              p["wq_t"][i], p["k1_b"][i], p["k2_b"][i])
        x = _peer_dense(lay, ht, cnt, e1, rank2, e2, [t[i] for t in p["u_parts"]],
                        [t[i] for t in p["vt_parts"]], x, gt_f)
    y = _final_norm(lay, x, p["g_final"]).reshape(batch, seq, D_MODEL)
    return y, jnp.stack(new_conv), jnp.stack(new_shift), jnp.stack(new_wkv)


def kernel(x_prompt, x_sample, state_conv, state_shift, state_wkv, c_prompt, c_sample, w_ada, b_ada, g_mix, g_ffn, g_final, conv_w_in, conv_k, conv_w_out, rwkv_mix, rwkv_wr, rwkv_wk, rwkv_wv, rwkv_wo, rwkv_w0, rwkv_w1, rwkv_w2, rwkv_a0, rwkv_a1, rwkv_a2, rwkv_v0, rwkv_v1, rwkv_v2, rwkv_g1, rwkv_g2, rwkv_k_k, rwkv_k_a, rwkv_r_k, rwkv_lnw, rwkv_lnb, peer_wq, peer_k1, peer_k2, peer_u, peer_v):
    p = dict(
        g_mix=g_mix, g_ffn=g_ffn, g_final=g_final, conv_w_in=conv_w_in, conv_k=conv_k, conv_w_out=conv_w_out,
        rwkv_mix=rwkv_mix, rwkv_wr=rwkv_wr, rwkv_wk=rwkv_wk, rwkv_wv=rwkv_wv, rwkv_wo=rwkv_wo,
        rwkv_w0=rwkv_w0, rwkv_w1=rwkv_w1, rwkv_w2=rwkv_w2, rwkv_a0=rwkv_a0, rwkv_a1=rwkv_a1, rwkv_a2=rwkv_a2,
        rwkv_v0=rwkv_v0, rwkv_v1=rwkv_v1, rwkv_v2=rwkv_v2, rwkv_g1=rwkv_g1, rwkv_g2=rwkv_g2,
        rwkv_k_k=rwkv_k_k, rwkv_k_a=rwkv_k_a, rwkv_r_k=rwkv_r_k, rwkv_lnw=rwkv_lnw, rwkv_lnb=rwkv_lnb,
        wq_t=jnp.swapaxes(peer_wq, 1, 2).astype(BF16),
        k1_b=peer_k1.astype(BF16), k2_b=peer_k2.astype(BF16),
        **dict(zip(("u_parts", "vt_parts"), _peer_weight_parts(peer_u, peer_v))),
    )
    n_prompt, seq_prompt = x_prompt.shape[0], x_prompt.shape[1]
    n_sample, seq_sample = x_sample.shape[0], x_sample.shape[1]
    m_all = _adaln(jnp.concatenate([c_prompt, c_sample], axis=0), w_ada, b_ada)
    zeros = lambda *s: jnp.zeros(s, F32)
    n_conv, n_rwkv = (DEPTH + 1) // 2, DEPTH // 2
    y_p, p_conv, p_shift, p_wkv = _run(
        n_prompt, seq_prompt, x_prompt, m_all[:, :n_prompt],
        zeros(n_conv, n_prompt, CONV_WIDTH - 1, D_MODEL), zeros(n_rwkv, n_prompt, D_MODEL),
        zeros(n_rwkv, n_prompt, N_HEADS, HEAD_SIZE, HEAD_SIZE), p)
    y_s, s_conv, s_shift, s_wkv = _run(
        n_sample, seq_sample, x_sample, m_all[:, n_prompt:], state_conv, state_shift, state_wkv, p)
    return (y_p, y_s, p_conv, p_shift, p_wkv, s_conv, s_shift, s_wkv)
```

```python
import functools
import math

import jax
import jax.numpy as jnp
from jax import lax
from jax.experimental import pallas as pl
from jax.experimental.pallas import tpu as pltpu

D_MODEL = 1024
DEPTH = 4
CONV_WIDTH = 3
HEAD_SIZE = 64
N_HEADS = D_MODEL // HEAD_SIZE
N_KEYS = 128
N_EXPERTS = N_KEYS * N_KEYS
PEER_HEADS = 8
PEER_TOPK = 16
D_QUERY = 256
RMS_EPS = 1e-6
GN_EPS = 64e-5

LORA_PAD = 128
TOKEN_TILE = 512
RWKV_PROJ_TILE = 256
PEER_EXPERT_TILE = 2048
PEER_GROUP = 512
PEER_STEP_ROWS = 4
PEER_LANE_CHUNK = 256
WKV_CHUNK = 64
WKV_SHORT_CHUNK = 8
WKV_HEADS_PER_STEP = 16
VMEM_LIMIT = 56 * 1024 * 1024

F32 = jnp.float32
BF16 = jnp.bfloat16


def _params(*sem):
    return pltpu.CompilerParams(dimension_semantics=sem, vmem_limit_bytes=VMEM_LIMIT)


def _sigmoid(x):
    return 1.0 / (1.0 + jnp.exp(-x))


def _rms_mod(x, g, sc, sh):
    y = x * lax.rsqrt(jnp.mean(x * x, axis=-1, keepdims=True) + RMS_EPS)
    return (y * g) * (1.0 + sc) + sh


def _bdot(a, b):
    return jnp.dot(a.astype(BF16), b.astype(BF16), preferred_element_type=F32)


def _fdot(a, b):
    return jnp.dot(a, b, preferred_element_type=F32)


def _fdot_nt(a, b):
    return lax.dot_general(a, b, (((1,), (1,)), ((), ())), preferred_element_type=F32)


def _fdot_tn(a, b):
    return lax.dot_general(a, b, (((0,), (0,)), ((), ())), preferred_element_type=F32)


class _Layout:
    def __init__(self, batch, seq, tile=TOKEN_TILE):
        self.batch, self.seq = batch, seq
        self.n = batch * seq
        self.tile = min(tile, self.n)
        assert self.n % self.tile == 0
        self.n_tiles = self.n // self.tile
        if seq % self.tile == 0:
            self.rows = 1
            self.tiles_per_group = seq // self.tile
        else:
            assert self.tile % seq == 0
            self.rows = self.tile
            self.tiles_per_group = 1

    def expand(self, m):
        if self.rows == 1:
            return m[:, None, :]
        return jnp.repeat(m, self.seq, axis=0).reshape(self.n_tiles, self.tile, m.shape[-1])

    def mod_spec(self):
        tpg = self.tiles_per_group
        return pl.BlockSpec((1, self.rows, D_MODEL), lambda i: (i // tpg, 0, 0))

    def tok_spec(self, width=D_MODEL):
        return pl.BlockSpec((self.tile, width), lambda i: (i, 0))


def _full_spec(shape):
    nd = len(shape)
    return pl.BlockSpec(shape, lambda *_: (0,) * nd)


def _adaln_kernel(c_ref, w_ref, b_ref, o_ref):
    c = c_ref[...]
    cs = c * _sigmoid(c)
    o_ref[0] = _fdot(cs, w_ref[0]) + b_ref[0]


def _adaln(c, w_ada, b_ada):
    nb = c.shape[0]
    tn = 1536
    return pl.pallas_call(
        _adaln_kernel,
        grid=(DEPTH, 6 * D_MODEL // tn),
        in_specs=[
            pl.BlockSpec((nb, D_MODEL), lambda l, j: (0, 0)),
            pl.BlockSpec((1, D_MODEL, tn), lambda l, j: (l, 0, j)),
            pl.BlockSpec((1, 1, tn), lambda l, j: (l, 0, j)),
        ],
        out_specs=pl.BlockSpec((1, nb, tn), lambda l, j: (l, 0, j)),
        out_shape=jax.ShapeDtypeStruct((DEPTH, nb, 6 * D_MODEL), F32),
        compiler_params=_params("arbitrary", "arbitrary"),
        name="adaln",
    )(c, w_ada, b_ada.reshape(DEPTH, 1, 6 * D_MODEL))


def _norm_mod_kernel(x_ref, g_ref, sc_ref, sh_ref, o_ref):
    o_ref[...] = _rms_mod(x_ref[...], g_ref[...], sc_ref[0], sh_ref[0])


def _norm_mod(lay, x, g, sc, sh):
    return pl.pallas_call(
        _norm_mod_kernel,
        grid=(lay.n_tiles,),
        in_specs=[lay.tok_spec(), _full_spec((1, D_MODEL)), lay.mod_spec(), lay.mod_spec()],
        out_specs=lay.tok_spec(),
        out_shape=jax.ShapeDtypeStruct((lay.n, D_MODEL), F32),
        compiler_params=_params("arbitrary"),
        name="norm_mod",
    )(x, g.reshape(1, D_MODEL), sc, sh)


def _final_norm_kernel(x_ref, g_ref, o_ref):
    x = x_ref[...]
    o_ref[...] = x * lax.rsqrt(jnp.mean(x * x, axis=-1, keepdims=True) + RMS_EPS) * g_ref[...]


def _final_norm(lay, x, g):
    return pl.pallas_call(
        _final_norm_kernel,
        grid=(lay.n_tiles,),
        in_specs=[lay.tok_spec(), _full_spec((1, D_MODEL))],
        out_specs=lay.tok_spec(),
        out_shape=jax.ShapeDtypeStruct((lay.n, D_MODEL), F32),
        compiler_params=_params("arbitrary"),
        name="final_norm",
    )(x, g.reshape(1, D_MODEL))


def _conv_in_kernel(x_ref, g_ref, sc_ref, sh_ref, w_ref, bg_ref, u_ref):
    h = _rms_mod(x_ref[...], g_ref[...], sc_ref[0], sh_ref[0]).astype(BF16)
    d = D_MODEL
    bg_ref[...] = _fdot(h, w_ref[:, 0:d])
    u_ref[...] = _fdot(h, w_ref[:, d:2 * d]) * _fdot(h, w_ref[:, 2 * d:3 * d])


def _conv_in(lay, x, g, sc, sh, w_in):
    shp = jax.ShapeDtypeStruct((lay.n, D_MODEL), F32)
    return pl.pallas_call(
        _conv_in_kernel,
        grid=(lay.n_tiles,),
        in_specs=[lay.tok_spec(), _full_spec((1, D_MODEL)), lay.mod_spec(), lay.mod_spec(),
                  _full_spec((D_MODEL, 3 * D_MODEL))],
        out_specs=[lay.tok_spec(), lay.tok_spec()],
        out_shape=[shp, shp],
        compiler_params=_params("arbitrary"),
        name="conv_in",
    )(x, g.reshape(1, D_MODEL), sc, sh, w_in.astype(BF16))


def _conv_out_kernel(bg_ref, u0_ref, u1_ref, u2_ref, ck_ref, w_ref, x_ref, gt_ref, o_ref):
    conv = u0_ref[...] * ck_ref[0:1, :] + u1_ref[...] * ck_ref[1:2, :] + u2_ref[...] * ck_ref[2:3, :]
    y = _bdot(bg_ref[...] * conv, w_ref[...])
    o_ref[...] = x_ref[...] + gt_ref[0] * y


def _conv_out(lay, bg, u0, u1, u2, conv_k, w_out, x, gt):
    return pl.pallas_call(
        _conv_out_kernel,
        grid=(lay.n_tiles,),
        in_specs=[lay.tok_spec()] * 4 + [_full_spec((CONV_WIDTH, D_MODEL)), _full_spec((D_MODEL, D_MODEL)),
                                         lay.tok_spec(), lay.mod_spec()],
        out_specs=lay.tok_spec(),
        out_shape=jax.ShapeDtypeStruct((lay.n, D_MODEL), F32),
        compiler_params=_params("arbitrary"),
        name="conv_out",
    )(bg, u0, u1, u2, conv_k, w_out.astype(BF16), x, gt)


def _mm_res_kernel(a_ref, w_ref, x_ref, gt_ref, o_ref):
    o_ref[...] = x_ref[...] + gt_ref[0] * _bdot(a_ref[...], w_ref[...])


def _mm_res(lay, a, w, x, gt):
    return pl.pallas_call(
        _mm_res_kernel,
        grid=(lay.n_tiles,),
        in_specs=[lay.tok_spec(), _full_spec((D_MODEL, D_MODEL)), lay.tok_spec(), lay.mod_spec()],
        out_specs=lay.tok_spec(),
        out_shape=jax.ShapeDtypeStruct((lay.n, D_MODEL), F32),
        compiler_params=_params("arbitrary"),
        name="mm_res",
    )(a, w.astype(BF16), x, gt)


def _rwkv_proj_kernel(*refs, has_vres):
    if has_vres:
        (h_ref, hs_ref, vf_ref, mix_ref, vec_ref, wr_ref, wk_ref, wv_ref, w1_ref, w2_ref, a1_ref, a2_ref,
         g1_ref, g2_ref, v1_ref, v2_ref, r_ref, km_ref, kk_ref, v_ref, lw_ref, a_ref, g_ref) = refs
    else:
        (h_ref, hs_ref, mix_ref, vec_ref, wr_ref, wk_ref, wv_ref, w1_ref, w2_ref, a1_ref, a2_ref,
         g1_ref, g2_ref, r_ref, km_ref, kk_ref, v_ref, lw_ref, a_ref, g_ref) = refs
    h = h_ref[...]
    xx = hs_ref[...] - h
    mixed = lambda i: (h + xx * mix_ref[i:i + 1, :]).astype(BF16)
    w0, a0, v0, k_k, k_a = (vec_ref[i:i + 1, :] for i in range(5))

    r_ref[...] = _fdot(mixed(0), wr_ref[...])
    wl = w0 + _bdot(jnp.tanh(_fdot(mixed(1), w1_ref[...])), w2_ref[...])
    z = -wl
    wlog = -(jnp.maximum(z, 0.0) + jnp.log(1.0 + jnp.exp(-jnp.abs(z)))) - 0.5
    lw_ref[...] = -jnp.exp(wlog)
    k = _fdot(mixed(2), wk_ref[...])
    xv = mixed(3)
    v = _fdot(xv, wv_ref[...])
    if has_vres:
        v = v + (vf_ref[...] - v) * _sigmoid(v0 + _bdot(_fdot(xv, v1_ref[...]), v2_ref[...]))
    v_ref[...] = v
    a = _sigmoid(a0 + _bdot(_fdot(mixed(4), a1_ref[...]), a2_ref[...]))
    a_ref[...] = a
    g_ref[...] = _bdot(_sigmoid(_fdot(mixed(5), g1_ref[...])), g2_ref[...])
    kk_ref[...] = k * k_k
    km_ref[...] = k * (1.0 + (a - 1.0) * k_a)


def _pad_cols(w):
    return jnp.pad(w, ((0, 0), (0, LORA_PAD - w.shape[1]))).astype(BF16)


def _pad_rows(w):
    return jnp.pad(w, ((0, LORA_PAD - w.shape[0]), (0, 0))).astype(BF16)


def _rwkv_proj(lay, h, hs, v_first, mix, vecs, wr, wk, wv, w1, w2, a1, a2, g1, g2, v1, v2):
    has_vres = v_first is not None
    sq = _full_spec((D_MODEL, D_MODEL))
    down = _full_spec((D_MODEL, LORA_PAD))
    up = _full_spec((LORA_PAD, D_MODEL))
    in_specs = [lay.tok_spec(), lay.tok_spec()] + ([lay.tok_spec()] if has_vres else [])
    in_specs += [_full_spec((6, D_MODEL)), _full_spec((8, D_MODEL)), sq, sq, sq, down, up, down, up, down, up]
    args = [h, hs] + ([v_first] if has_vres else [])
    args += [mix, vecs, wr.astype(BF16), wk.astype(BF16), wv.astype(BF16), _pad_cols(w1), _pad_rows(w2),
             _pad_cols(a1), _pad_rows(a2), _pad_cols(g1), _pad_rows(g2)]
    if has_vres:
        in_specs += [down, up]
        args += [_pad_cols(v1), _pad_rows(v2)]
    shp = jax.ShapeDtypeStruct((lay.n, D_MODEL), F32)
    return pl.pallas_call(
        functools.partial(_rwkv_proj_kernel, has_vres=has_vres),
        grid=(lay.n_tiles,),
        in_specs=in_specs,
        out_specs=[lay.tok_spec()] * 7,
        out_shape=[shp] * 7,
        compiler_params=_params("arbitrary"),
        name="rwkv_proj",
    )(*args)


def _wkv_kernel(r_ref, km_ref, kk_ref, v_ref, lw_ref, a_ref, g_ref, rk_ref, lnw_ref, lnb_ref, s0_ref,
                y_ref, s_ref, st_ref, *, chunk, heads):
    t = pl.program_id(2)
    L = chunk

    pairs = heads // 2
    pw = 2 * HEAD_SIZE
    lo, hi = slice(0, HEAD_SIZE), slice(HEAD_SIZE, pw)

    @pl.when(t == 0)
    def _():
        st_ref[...] = jnp.zeros_like(st_ref)
        for p in range(pairs):
            st_ref[p, lo, lo] = s0_ref[0, 2 * p]
            st_ref[p, hi, hi] = s0_ref[0, 2 * p + 1]

    row = lax.broadcasted_iota(jnp.int32, (L, 2 * L), 0)
    col = lax.broadcasted_iota(jnp.int32, (L, 2 * L), 1)
    col = jnp.where(col >= L, col - L, col)
    strict2 = row > col
    incl2 = row >= col
    tri = incl2[:, :L].astype(F32)
    even = lax.broadcasted_iota(jnp.int32, (1, pw), 1) < HEAD_SIZE
    same_head = ((lax.broadcasted_iota(jnp.int32, (pw, pw), 0) < HEAD_SIZE)
                 == (lax.broadcasted_iota(jnp.int32, (pw, pw), 1) < HEAD_SIZE))
    ones_bd = same_head.astype(BF16)

    def head_sum(x):
        x_hi = x.astype(BF16)
        x_lo = (x - x_hi.astype(F32)).astype(BF16)
        return _fdot(x_hi, ones_bd) + _fdot(x_lo, ones_bd)

    pick = lambda x0, x1: jnp.where(even, x0, x1)

    each = lambda f, *cols: [f(*xs) for xs in zip(*cols)]
    sls = [slice(p * pw, (p + 1) * pw) for p in range(pairs)]
    r, km, kkr, v, lw, asig, g = ([ref[:, sl] for sl in sls]
                                  for ref in (r_ref, km_ref, kk_ref, v_ref, lw_ref, a_ref, g_ref))
    s0 = [st_ref[p] for p in range(pairs)]
    c = each(lambda x: _fdot(tri, x), lw)
    kk = each(lambda x: x * lax.rsqrt(jnp.maximum(head_sum(x * x), 1e-24)), kkr)
    b = each(lambda x, y: x * y, kk, asig)
    c_last = each(lambda x: x[L - 1:L, :], c)
    ginv = each(lambda x: jnp.exp(-x), c)
    lhs = each(lambda kk_, c_, lw_, r_: jnp.concatenate([-kk_ * jnp.exp(c_ - lw_), r_ * jnp.exp(c_)], axis=0),
               kk, c, lw, r)
    rhs = each(lambda b_, km_, gi: jnp.concatenate([b_ * gi, km_ * gi], axis=0), b, km, ginv)
    qk0 = each(lambda l_, r_: _fdot_nt(jnp.where(even, l_, 0.0), r_), lhs, rhs)
    qk1 = each(lambda l_, r_: _fdot_nt(jnp.where(even, 0.0, l_), r_), lhs, rhs)
    hs0 = each(_fdot_nt, lhs, s0)
    zv = each(lambda v_: jnp.concatenate([jnp.zeros_like(v_), v_], axis=0), v)
    m0 = each(lambda q: jnp.where(strict2, q[:L, :], 0.0), qk0)
    m1 = each(lambda q: jnp.where(strict2, q[:L, :], 0.0), qk1)
    u = each(lambda h_, a0, a1, z_: h_[:L] + pick(_fdot(a0, z_), _fdot(a1, z_)), hs0, m0, m1, zv)
    p0 = each(lambda m: m[:, :L], m0)
    p1 = each(lambda m: m[:, :L], m1)
    for it in range(int(math.log2(L))):
        if it > 0:
            p0 = each(lambda x: _fdot(x, x), p0)
            p1 = each(lambda x: _fdot(x, x), p1)
        u = each(lambda u_, a0, a1: u_ + pick(_fdot(a0, u_), _fdot(a1, u_)), u, p0, p1)
    uv = each(lambda u_, v_: jnp.concatenate([u_, v_], axis=0), u, v)
    bk = each(lambda cl, c_, b_, km_: jnp.concatenate([b_, km_], axis=0)
              * jnp.exp(cl - jnp.concatenate([c_, c_], axis=0)), c_last, c, b, km)
    s_new = each(lambda s_, cl, uv_, bk_: jnp.where(same_head, s_ * jnp.exp(cl) + _fdot_tn(uv_, bk_), 0.0),
                 s0, c_last, uv, bk)
    for p in range(pairs):
        st_ref[p] = s_new[p]
    y = each(lambda h_, q0, q1, uv_: h_[L:] + pick(_fdot(jnp.where(incl2, q0[L:, :], 0.0), uv_),
                                                   _fdot(jnp.where(incl2, q1[L:, :], 0.0), uv_)),
             hs0, qk0, qk1, uv)

    def finish(y_, r_, km_, v_, g_, sl):
        yc = y_ - head_sum(y_) * (1.0 / HEAD_SIZE)
        var = head_sum(yc * yc) * (1.0 / HEAD_SIZE)
        yn = yc * lax.rsqrt(var + GN_EPS) * lnw_ref[:, sl] + lnb_ref[:, sl]
        bonus = head_sum(r_ * km_ * rk_ref[:, sl]) * v_
        return (yn + bonus) * g_

    for p, out in enumerate(each(finish, y, r, km, v, g, sls)):
        y_ref[:, sls[p]] = out

    @pl.when(t == pl.num_programs(2) - 1)
    def _():
        for p in range(pairs):
            s_ref[0, 2 * p] = st_ref[p, lo, lo]
            s_ref[0, 2 * p + 1] = st_ref[p, hi, hi]


def _wkv(batch, seq, chunk, heads, r, km, kk, v, lw, a, g, r_k, lnw, lnb, s0):
    n_t = seq // chunk
    width = heads * HEAD_SIZE
    tok = pl.BlockSpec((chunk, width), lambda b, h, t: (b * n_t + t, h))
    vec = pl.BlockSpec((1, width), lambda b, h, t: (0, h))
    st = pl.BlockSpec((1, heads, HEAD_SIZE, HEAD_SIZE), lambda b, h, t: (b, h, 0, 0))
    return pl.pallas_call(
        functools.partial(_wkv_kernel, chunk=chunk, heads=heads),
        grid=(batch, N_HEADS // heads, n_t),
        in_specs=[tok] * 7 + [vec] * 3 + [st],
        out_specs=[tok, st],
        out_shape=[jax.ShapeDtypeStruct((batch * seq, D_MODEL), F32),
                   jax.ShapeDtypeStruct((batch, N_HEADS, HEAD_SIZE, HEAD_SIZE), F32)],
        scratch_shapes=[pltpu.VMEM((heads // 2, 2 * HEAD_SIZE, 2 * HEAD_SIZE), F32)],
        compiler_params=_params("arbitrary", "arbitrary", "arbitrary"),
        name="wkv",
    )(r, km, kk, v, lw, a, g, r_k.reshape(1, D_MODEL), lnw.reshape(1, D_MODEL), lnb.reshape(1, D_MODEL), s0)


SUBLANES = 8


def _col_reduce(x3, op):
    m = x3[0]
    for r in range(1, x3.shape[0]):
        m = op(m, x3[r])
    for shift in (4, 2, 1):
        m = op(m, pltpu.roll(m, shift, axis=0))
    return m


def _top16(s3, want_rank):
    work = s3
    rank = jnp.full(s3.shape, float(PEER_TOPK), F32) if want_rank else None
    vals = []
    for k in range(PEER_TOPK):
        m = _col_reduce(work, jnp.maximum)
        hit = work == m[None]
        if want_rank:
            rank = jnp.where(hit, float(k), rank)
        work = jnp.where(hit, -jnp.inf, work)
        vals.append(m)
    return vals, work, rank


def _rows_from(vals):
    row = lax.broadcasted_iota(jnp.int32, vals[0].shape, 0)
    out = vals[0]
    for s in range(1, SUBLANES):
        out = jnp.where(row == s, vals[s], out)
    return out


def _peer_select_chunk(s1, s2):
    n_keys, w = s1.shape
    s1 = s1.reshape(n_keys // SUBLANES, SUBLANES, w)
    s2 = s2.reshape(n_keys // SUBLANES, SUBLANES, w)
    v1, left1, _ = _top16(s1, False)
    v2, _, rank2 = _top16(s2, True)
    v2_lo, v2_hi, v1_hi = _rows_from(v2[:SUBLANES]), _rows_from(v2[SUBLANES:]), _rows_from(v1[SUBLANES:])
    cand = jnp.stack([v1[0] + v2_lo, v1[0] + v2_hi] + [v1[a] + v2_lo for a in range(1, SUBLANES)]
                     + [v1_hi + v2[0]])
    work = cand
    tau = None
    for _ in range(PEER_TOPK):
        tau = _col_reduce(work, jnp.maximum)
        work = jnp.where(work == tau[None], -jnp.inf, work)
    top = v1[0] + v2[0]
    zsum = _col_reduce(jnp.where(cand >= tau[None], jnp.exp(cand - top[None]), 0.0), jnp.add)
    cnt = jnp.zeros(s1.shape, F32)
    for b in range(PEER_TOPK):
        cnt = cnt + jnp.where(s1 + v2[b][None] >= tau[None], 1.0, 0.0)
    cnt = jnp.where(left1 == -jnp.inf, cnt, 0.0)
    e1 = jnp.exp(s1 - v1[0][None])
    e2 = jnp.exp(s2 - v2[0][None]) / zsum[None]
    return tuple(t.reshape(n_keys, w) for t in (cnt, e1, rank2, e2))


def _peer_select_kernel(x_ref, g_ref, sc_ref, sh_ref, wq_ref, k1_ref, k2_ref,
                        ht_ref, c_ref, e1_ref, rk_ref, e2_ref, q_scr):
    h = _rms_mod(x_ref[...], g_ref[...], sc_ref[0], sh_ref[0])
    ht = h.T.astype(BF16)
    ht_ref[...] = ht
    q_scr[...] = _fdot(wq_ref[...], ht)
    half = D_QUERY // 2
    n_chunks = x_ref.shape[0] // PEER_LANE_CHUNK

    def head_body(hd, carry):
        q1 = q_scr[pl.ds(pl.multiple_of(hd * D_QUERY, D_QUERY), half), :]
        q2 = q_scr[pl.ds(pl.multiple_of(hd * D_QUERY + half, half), half), :]
        s1 = _bdot(k1_ref[hd], q1)
        s2 = _bdot(k2_ref[hd], q2)
        for ch in range(n_chunks):
            ls = slice(ch * PEER_LANE_CHUNK, (ch + 1) * PEER_LANE_CHUNK)
            cnt, e1, rank2, e2 = _peer_select_chunk(s1[:, ls], s2[:, ls])
            c_ref[hd, :, ls] = cnt
            e1_ref[hd, :, ls] = e1
            rk_ref[hd, :, ls] = rank2.astype(BF16)
            e2_ref[hd, :, ls] = e2.astype(BF16)
        return carry

    lax.fori_loop(0, PEER_HEADS, head_body, 0)


def _peer_select(lay, x, g, sc, sh, wq_t, k1, k2):
    tt = lay.tile
    sel_spec = pl.BlockSpec((PEER_HEADS, N_KEYS, tt), lambda i: (0, 0, i))
    sel_shape = lambda dt: jax.ShapeDtypeStruct((PEER_HEADS, N_KEYS, lay.n), dt)
    return pl.pallas_call(
        _peer_select_kernel,
        grid=(lay.n_tiles,),
        in_specs=[lay.tok_spec(), _full_spec((1, D_MODEL)), lay.mod_spec(), lay.mod_spec(),
                  _full_spec((PEER_HEADS * D_QUERY, D_MODEL)),
                  _full_spec((PEER_HEADS, N_KEYS, D_QUERY // 2)), _full_spec((PEER_HEADS, N_KEYS, D_QUERY // 2))],
        out_specs=[pl.BlockSpec((D_MODEL, tt), lambda i: (0, i))] + [sel_spec] * 4,
        out_shape=[jax.ShapeDtypeStruct((D_MODEL, lay.n), BF16),
                   sel_shape(F32), sel_shape(F32), sel_shape(BF16), sel_shape(BF16)],
        scratch_shapes=[pltpu.VMEM((PEER_HEADS * D_QUERY, tt), F32)],
        compiler_params=_params("arbitrary"),
        name="peer_select",
    )(x, g.reshape(1, D_MODEL), sc, sh, wq_t, k1, k2)


def _peer_dense_kernel(*refs):
    groups = PEER_EXPERT_TILE // PEER_GROUP
    ht_ref, c_ref, e1_ref, rk_ref, e2_ref = refs[:5]
    u_refs = refs[5:5 + groups]
    vt_refs = refs[5 + groups:5 + 2 * groups]
    x_ref, gt_ref, o_ref, acc_ref, coef_ref, z_ref = refs[5 + 2 * groups:]
    e = pl.program_id(1)
    sr = PEER_STEP_ROWS
    spg = PEER_GROUP // (N_KEYS * sr)
    piece = D_MODEL // spg
    last_slot = (groups - 1) % 2

    @pl.when(e == 0)
    def _():
        acc_ref[...] = jnp.zeros_like(acc_ref)

    def pre_act(step):
        g, ii = divmod(step, spg)
        return _fdot(u_refs[g][0, ii * sr * N_KEYS:(ii + 1) * sr * N_KEYS, :], ht_ref[...])

    z_ref[0] = pre_act(0)
    for g in range(groups):
        slot = g % 2
        for ii in range(spg):
            step = g * spg + ii
            if step + 1 < groups * spg:
                z_ref[(step + 1) % 2] = pre_act(step + 1)
            if g > 0:
                ps = slice(ii * piece, (ii + 1) * piece)
                acc_ref[ps, :] += _fdot(vt_refs[g - 1][0, ps, :], coef_ref[1 - slot])
            for rr in range(sr):
                row = step * sr + rr
                gate = None
                for hd in range(PEER_HEADS):
                    cnt = c_ref[hd, row:row + 1, :].astype(BF16)
                    e1 = e1_ref[hd, row:row + 1, :].astype(BF16)
                    term = jnp.where(rk_ref[hd] < cnt, e2_ref[hd], jnp.zeros((), BF16)) * e1
                    gate = term if gate is None else gate + term
                z = z_ref[step % 2, rr * N_KEYS:(rr + 1) * N_KEYS, :]
                act = 0.5 * z * (1.0 + lax.erf(z * (1.0 / math.sqrt(2.0))))
                lo = (ii * sr + rr) * N_KEYS
                coef_ref[slot, lo:lo + N_KEYS, :] = gate * act.astype(BF16)

    acc_ref[...] += _fdot(vt_refs[groups - 1][0], coef_ref[last_slot])

    @pl.when(e == pl.num_programs(1) - 1)
    def _():
        o_ref[...] = x_ref[...] + gt_ref[0] * acc_ref[...].T


def _peer_weight_parts(peer_u, peer_v):
    groups = PEER_EXPERT_TILE // PEER_GROUP
    tiles = N_EXPERTS // PEER_EXPERT_TILE
    u5 = peer_u.astype(BF16).reshape(DEPTH, tiles, groups, PEER_GROUP, D_MODEL)
    v5 = jnp.swapaxes(peer_v.astype(BF16).reshape(DEPTH, tiles, groups, PEER_GROUP, D_MODEL), 3, 4)
    u_parts = [u5[:, :, k] for k in range(groups)]
    vt_parts = [v5[:, :, k] for k in range(groups)]
    return u_parts, vt_parts


def _peer_dense(lay, ht, cnt, e1, rank2, e2, u_parts, vt_parts, x, gt):
    tt = lay.tile
    groups = PEER_EXPERT_TILE // PEER_GROUP
    rows_per_tile = PEER_EXPERT_TILE // N_KEYS
    tpg = lay.tiles_per_group
    row_spec = pl.BlockSpec((PEER_HEADS, rows_per_tile, tt), lambda i, e: (0, e, i))
    key_spec = pl.BlockSpec((PEER_HEADS, N_KEYS, tt), lambda i, e: (0, 0, i))
    u_spec = pl.BlockSpec((1, PEER_GROUP, D_MODEL), lambda i, e: (e, 0, 0))
    vt_spec = pl.BlockSpec((1, D_MODEL, PEER_GROUP), lambda i, e: (e, 0, 0))
    return pl.pallas_call(
        _peer_dense_kernel,
        grid=(lay.n_tiles, N_EXPERTS // PEER_EXPERT_TILE),
        in_specs=[pl.BlockSpec((D_MODEL, tt), lambda i, e: (0, i)), row_spec, row_spec, key_spec, key_spec]
        + [u_spec] * groups + [vt_spec] * groups + [
            pl.BlockSpec((tt, D_MODEL), lambda i, e: (i, 0)),
            pl.BlockSpec((1, lay.rows, D_MODEL), lambda i, e: (i // tpg, 0, 0))],
        out_specs=pl.BlockSpec((tt, D_MODEL), lambda i, e: (i, 0)),
        out_shape=jax.ShapeDtypeStruct((lay.n, D_MODEL), F32),
        scratch_shapes=[pltpu.VMEM((D_MODEL, tt), F32), pltpu.VMEM((2, PEER_GROUP, tt), BF16),
                        pltpu.VMEM((2, PEER_STEP_ROWS * N_KEYS, tt), F32)],
        compiler_params=_params("arbitrary", "arbitrary"),
        name="peer_dense",
    )(ht, cnt, e1, rank2, e2, *u_parts, *vt_parts, x, gt)


def _run(batch, seq, x, m_all, st_conv, st_shift, st_wkv, p):
    lay = _Layout(batch, seq)
    x = x.reshape(lay.n, D_MODEL)
    new_conv, new_shift, new_wkv = [], [], []
    v_first = None
    short = seq < WKV_CHUNK
    for i in range(DEPTH):
        sh_m, sc_m, gt_m, sh_f, sc_f, gt_f = (lay.expand(t) for t in jnp.split(m_all[i], 6, axis=-1))
        j = i // 2
        if i % 2 == 0:
            bg, u = _conv_in(lay, x, p["g_mix"][i], sc_m, sh_m, p["conv_w_in"][j])
            up = jnp.concatenate([st_conv[j], u.reshape(batch, seq, D_MODEL)], axis=1)
            u0 = up[:, 0:seq].reshape(lay.n, D_MODEL)
            u1 = up[:, 1:seq + 1].reshape(lay.n, D_MODEL)
            new_conv.append(up[:, -(CONV_WIDTH - 1):])
            x = _conv_out(lay, bg, u0, u1, u, p["conv_k"][j], p["conv_w_out"][j], x, gt_m)
        else:
            h = _norm_mod(lay, x, p["g_mix"][i], sc_m, sh_m)
            h3 = h.reshape(batch, seq, D_MODEL)
            hs = jnp.concatenate([st_shift[j][:, None, :], h3[:, :-1]], axis=1).reshape(lay.n, D_MODEL)
            new_shift.append(h3[:, -1])
            zero = jnp.zeros((D_MODEL,), F32)
            v0 = p["rwkv_v0"][j - 1] if j > 0 else zero
            vecs = jnp.stack([p["rwkv_w0"][j], p["rwkv_a0"][j], v0, p["rwkv_k_k"][j], p["rwkv_k_a"][j],
                              zero, zero, zero])
            v1 = p["rwkv_v1"][j - 1] if j > 0 else None
            v2 = p["rwkv_v2"][j - 1] if j > 0 else None
            r, km, kk, v, lw, a, g = _rwkv_proj(
                _Layout(batch, seq, RWKV_PROJ_TILE), h, hs, v_first if j > 0 else None, p["rwkv_mix"][j], vecs,
                p["rwkv_wr"][j], p["rwkv_wk"][j], p["rwkv_wv"][j], p["rwkv_w1"][j], p["rwkv_w2"][j],
                p["rwkv_a1"][j], p["rwkv_a2"][j], p["rwkv_g1"][j], p["rwkv_g2"][j], v1, v2)
            if j == 0:
                v_first = v
            scan_in = (r, km, kk, v, lw, a, g)
            if short:
                pad = lambda t: jnp.pad(t.reshape(batch, seq, D_MODEL),
                                        ((0, 0), (0, WKV_SHORT_CHUNK - seq), (0, 0))).reshape(-1, D_MODEL)
                scan_in = tuple(pad(t) for t in scan_in)
                yg, s_new = _wkv(batch, WKV_SHORT_CHUNK, WKV_SHORT_CHUNK, N_HEADS, *scan_in,
                                 p["rwkv_r_k"][j], p["rwkv_lnw"][j], p["rwkv_lnb"][j], st_wkv[j])
                yg = yg.reshape(batch, WKV_SHORT_CHUNK, D_MODEL)[:, :seq].reshape(lay.n, D_MODEL)
            else:
                yg, s_new = _wkv(batch, seq, WKV_CHUNK, WKV_HEADS_PER_STEP, *scan_in,
                                 p["rwkv_r_k"][j], p["rwkv_lnw"][j], p["rwkv_lnb"][j], st_wkv[j])
            new_wkv.append(s_new)
            x = _mm_res(lay, yg, p["rwkv_wo"][j], x, gt_m)
        ht, cnt, e1, rank2, e2 = _peer_select(lay, x, p["g_ffn"][i], sc_f, sh_f,
                                              p["wq_t"][i], p["k1_b"][i], p["k2_b"][i])
        x = _peer_dense(lay, ht, cnt, e1, rank2, e2, [t[i] for t in p["u_parts"]],
                        [t[i] for t in p["vt_parts"]], x, gt_f)
    y = _final_norm(lay, x, p["g_final"]).reshape(batch, seq, D_MODEL)
    return y, jnp.stack(new_conv), jnp.stack(new_shift), jnp.stack(new_wkv)


def kernel(x_prompt, x_sample, state_conv, state_shift, state_wkv, c_prompt, c_sample, w_ada, b_ada, g_mix, g_ffn, g_final, conv_w_in, conv_k, conv_w_out, rwkv_mix, rwkv_wr, rwkv_wk, rwkv_wv, rwkv_wo, rwkv_w0, rwkv_w1, rwkv_w2, rwkv_a0, rwkv_a1, rwkv_a2, rwkv_v0, rwkv_v1, rwkv_v2, rwkv_g1, rwkv_g2, rwkv_k_k, rwkv_k_a, rwkv_r_k, rwkv_lnw, rwkv_lnb, peer_wq, peer_k1, peer_k2, peer_u, peer_v):
    p = dict(
        g_mix=g_mix, g_ffn=g_ffn, g_final=g_final, conv_w_in=conv_w_in, conv_k=conv_k, conv_w_out=conv_w_out,
        rwkv_mix=rwkv_mix, rwkv_wr=rwkv_wr, rwkv_wk=rwkv_wk, rwkv_wv=rwkv_wv, rwkv_wo=rwkv_wo,
        rwkv_w0=rwkv_w0, rwkv_w1=rwkv_w1, rwkv_w2=rwkv_w2, rwkv_a0=rwkv_a0, rwkv_a1=rwkv_a1, rwkv_a2=rwkv_a2,
        rwkv_v0=rwkv_v0, rwkv_v1=rwkv_v1, rwkv_v2=rwkv_v2, rwkv_g1=rwkv_g1, rwkv_g2=rwkv_g2,
        rwkv_k_k=rwkv_k_k, rwkv_k_a=rwkv_k_a, rwkv_r_k=rwkv_r_k, rwkv_lnw=rwkv_lnw, rwkv_lnb=rwkv_lnb,
        wq_t=jnp.swapaxes(peer_wq, 1, 2).astype(BF16),
        k1_b=peer_k1.astype(BF16), k2_b=peer_k2.astype(BF16),
        **dict(zip(("u_parts", "vt_parts"), _peer_weight_parts(peer_u, peer_v))),
    )
    n_prompt, seq_prompt = x_prompt.shape[0], x_prompt.shape[1]
    n_sample, seq_sample = x_sample.shape[0], x_sample.shape[1]
    m_all = _adaln(jnp.concatenate([c_prompt, c_sample], axis=0), w_ada, b_ada)
    zeros = lambda *s: jnp.zeros(s, F32)
    n_conv, n_rwkv = (DEPTH + 1) // 2, DEPTH // 2
    y_p, p_conv, p_shift, p_wkv = _run(
        n_prompt, seq_prompt, x_prompt, m_all[:, :n_prompt],
        zeros(n_conv, n_prompt, CONV_WIDTH - 1, D_MODEL), zeros(n_rwkv, n_prompt, D_MODEL),
        zeros(n_rwkv, n_prompt, N_HEADS, HEAD_SIZE, HEAD_SIZE), p)
    y_s, s_conv, s_shift, s_wkv = _run(
        n_sample, seq_sample, x_sample, m_all[:, n_prompt:], state_conv, state_shift, state_wkv, p)
    return (y_p, y_s, p_conv, p_shift, p_wkv, s_conv, s_shift, s_wkv)
```

```python
import functools
import math

import jax
import jax.numpy as jnp
from jax import lax
from jax.experimental import pallas as pl
from jax.experimental.pallas import tpu as pltpu

D_MODEL = 1024
DEPTH = 4
CONV_WIDTH = 3
HEAD_SIZE = 64
N_HEADS = D_MODEL // HEAD_SIZE
N_KEYS = 128
N_EXPERTS = N_KEYS * N_KEYS
PEER_HEADS = 8
PEER_TOPK = 16
D_QUERY = 256
RMS_EPS = 1e-6
GN_EPS = 64e-5

LORA_PAD = 128
TOKEN_TILE = 512
RWKV_PROJ_TILE = 256
PEER_EXPERT_TILE = 2048
PEER_GROUP = 512
PEER_STEP_ROWS = 4
PEER_LANE_CHUNK = 256
WKV_CHUNK = 64
WKV_SHORT_CHUNK = 8
WKV_SEQS_PER_STEP = 2
WKV_SHORT_SEQS_PER_STEP = 4
VMEM_LIMIT = 56 * 1024 * 1024

F32 = jnp.float32
BF16 = jnp.bfloat16


def _params(*sem):
    return pltpu.CompilerParams(dimension_semantics=sem, vmem_limit_bytes=VMEM_LIMIT)


def _sigmoid(x):
    return 1.0 / (1.0 + jnp.exp(-x))


def _rms_mod(x, g, sc, sh):
    y = x * lax.rsqrt(jnp.mean(x * x, axis=-1, keepdims=True) + RMS_EPS)
    return (y * g) * (1.0 + sc) + sh


def _bdot(a, b):
    return jnp.dot(a.astype(BF16), b.astype(BF16), preferred_element_type=F32)


def _fdot(a, b):
    return jnp.dot(a, b, preferred_element_type=F32)


def _fdot_nt(a, b):
    return lax.dot_general(a, b, (((1,), (1,)), ((), ())), preferred_element_type=F32)


def _fdot_tn(a, b):
    return lax.dot_general(a, b, (((0,), (0,)), ((), ())), preferred_element_type=F32)


class _Layout:
    def __init__(self, batch, seq, tile=TOKEN_TILE):
        self.batch, self.seq = batch, seq
        self.n = batch * seq
        self.tile = min(tile, self.n)
        assert self.n % self.tile == 0
        self.n_tiles = self.n // self.tile
        if seq % self.tile == 0:
            self.rows = 1
            self.tiles_per_group = seq // self.tile
        else:
            assert self.tile % seq == 0
            self.rows = self.tile
            self.tiles_per_group = 1

    def expand(self, m):
        if self.rows == 1:
            return m[:, None, :]
        return jnp.repeat(m, self.seq, axis=0).reshape(self.n_tiles, self.tile, m.shape[-1])

    def mod_spec(self):
        tpg = self.tiles_per_group
        return pl.BlockSpec((1, self.rows, D_MODEL), lambda i: (i // tpg, 0, 0))

    def tok_spec(self, width=D_MODEL):
        return pl.BlockSpec((self.tile, width), lambda i: (i, 0))


def _full_spec(shape):
    nd = len(shape)
    return pl.BlockSpec(shape, lambda *_: (0,) * nd)


def _adaln_kernel(c_ref, w_ref, b_ref, o_ref):
    c = c_ref[...]
    cs = c * _sigmoid(c)
    o_ref[0] = _fdot(cs, w_ref[0]) + b_ref[0]


def _adaln(c, w_ada, b_ada):
    nb = c.shape[0]
    tn = 1536
    return pl.pallas_call(
        _adaln_kernel,
        grid=(DEPTH, 6 * D_MODEL // tn),
        in_specs=[
            pl.BlockSpec((nb, D_MODEL), lambda l, j: (0, 0)),
            pl.BlockSpec((1, D_MODEL, tn), lambda l, j: (l, 0, j)),
            pl.BlockSpec((1, 1, tn), lambda l, j: (l, 0, j)),
        ],
        out_specs=pl.BlockSpec((1, nb, tn), lambda l, j: (l, 0, j)),
        out_shape=jax.ShapeDtypeStruct((DEPTH, nb, 6 * D_MODEL), F32),
        compiler_params=_params("arbitrary", "arbitrary"),
        name="adaln",
    )(c, w_ada, b_ada.reshape(DEPTH, 1, 6 * D_MODEL))


def _norm_mod_kernel(x_ref, g_ref, sc_ref, sh_ref, o_ref):
    o_ref[...] = _rms_mod(x_ref[...], g_ref[...], sc_ref[0], sh_ref[0])


def _norm_mod(lay, x, g, sc, sh):
    return pl.pallas_call(
        _norm_mod_kernel,
        grid=(lay.n_tiles,),
        in_specs=[lay.tok_spec(), _full_spec((1, D_MODEL)), lay.mod_spec(), lay.mod_spec()],
        out_specs=lay.tok_spec(),
        out_shape=jax.ShapeDtypeStruct((lay.n, D_MODEL), F32),
        compiler_params=_params("arbitrary"),
        name="norm_mod",
    )(x, g.reshape(1, D_MODEL), sc, sh)


def _final_norm_kernel(x_ref, g_ref, o_ref):
    x = x_ref[...]
    o_ref[...] = x * lax.rsqrt(jnp.mean(x * x, axis=-1, keepdims=True) + RMS_EPS) * g_ref[...]


def _final_norm(lay, x, g):
    return pl.pallas_call(
        _final_norm_kernel,
        grid=(lay.n_tiles,),
        in_specs=[lay.tok_spec(), _full_spec((1, D_MODEL))],
        out_specs=lay.tok_spec(),
        out_shape=jax.ShapeDtypeStruct((lay.n, D_MODEL), F32),
        compiler_params=_params("arbitrary"),
        name="final_norm",
    )(x, g.reshape(1, D_MODEL))


def _conv_in_kernel(x_ref, g_ref, sc_ref, sh_ref, w_ref, bg_ref, u_ref):
    h = _rms_mod(x_ref[...], g_ref[...], sc_ref[0], sh_ref[0]).astype(BF16)
    d = D_MODEL
    bg_ref[...] = _fdot(h, w_ref[:, 0:d])
    u_ref[...] = _fdot(h, w_ref[:, d:2 * d]) * _fdot(h, w_ref[:, 2 * d:3 * d])


def _conv_in(lay, x, g, sc, sh, w_in):
    shp = jax.ShapeDtypeStruct((lay.n, D_MODEL), F32)
    return pl.pallas_call(
        _conv_in_kernel,
        grid=(lay.n_tiles,),
        in_specs=[lay.tok_spec(), _full_spec((1, D_MODEL)), lay.mod_spec(), lay.mod_spec(),
                  _full_spec((D_MODEL, 3 * D_MODEL))],
        out_specs=[lay.tok_spec(), lay.tok_spec()],
        out_shape=[shp, shp],
        compiler_params=_params("arbitrary"),
        name="conv_in",
    )(x, g.reshape(1, D_MODEL), sc, sh, w_in.astype(BF16))


def _conv_out_kernel(bg_ref, u0_ref, u1_ref, u2_ref, ck_ref, w_ref, x_ref, gt_ref, o_ref):
    conv = u0_ref[...] * ck_ref[0:1, :] + u1_ref[...] * ck_ref[1:2, :] + u2_ref[...] * ck_ref[2:3, :]
    y = _bdot(bg_ref[...] * conv, w_ref[...])
    o_ref[...] = x_ref[...] + gt_ref[0] * y


def _conv_out(lay, bg, u0, u1, u2, conv_k, w_out, x, gt):
    return pl.pallas_call(
        _conv_out_kernel,
        grid=(lay.n_tiles,),
        in_specs=[lay.tok_spec()] * 4 + [_full_spec((CONV_WIDTH, D_MODEL)), _full_spec((D_MODEL, D_MODEL)),
                                         lay.tok_spec(), lay.mod_spec()],
        out_specs=lay.tok_spec(),
        out_shape=jax.ShapeDtypeStruct((lay.n, D_MODEL), F32),
        compiler_params=_params("arbitrary"),
        name="conv_out",
    )(bg, u0, u1, u2, conv_k, w_out.astype(BF16), x, gt)


def _mm_res_kernel(a_ref, w_ref, x_ref, gt_ref, o_ref):
    o_ref[...] = x_ref[...] + gt_ref[0] * _bdot(a_ref[...], w_ref[...])


def _mm_res(lay, a, w, x, gt):
    return pl.pallas_call(
        _mm_res_kernel,
        grid=(lay.n_tiles,),
        in_specs=[lay.tok_spec(), _full_spec((D_MODEL, D_MODEL)), lay.tok_spec(), lay.mod_spec()],
        out_specs=lay.tok_spec(),
        out_shape=jax.ShapeDtypeStruct((lay.n, D_MODEL), F32),
        compiler_params=_params("arbitrary"),
        name="mm_res",
    )(a, w.astype(BF16), x, gt)


def _rwkv_proj_kernel(*refs, has_vres):
    if has_vres:
        (h_ref, hs_ref, vf_ref, mix_ref, vec_ref, wr_ref, wk_ref, wv_ref, w1_ref, w2_ref, a1_ref, a2_ref,
         g1_ref, g2_ref, v1_ref, v2_ref, r_ref, km_ref, kk_ref, v_ref, lw_ref, a_ref, g_ref) = refs
    else:
        (h_ref, hs_ref, mix_ref, vec_ref, wr_ref, wk_ref, wv_ref, w1_ref, w2_ref, a1_ref, a2_ref,
         g1_ref, g2_ref, r_ref, km_ref, kk_ref, v_ref, lw_ref, a_ref, g_ref) = refs
    h = h_ref[...]
    xx = hs_ref[...] - h
    mixed = lambda i: (h + xx * mix_ref[i:i + 1, :]).astype(BF16)
    w0, a0, v0, k_k, k_a = (vec_ref[i:i + 1, :] for i in range(5))

    r_ref[...] = _fdot(mixed(0), wr_ref[...])
    wl = w0 + _bdot(jnp.tanh(_fdot(mixed(1), w1_ref[...])), w2_ref[...])
    z = -wl
    wlog = -(jnp.maximum(z, 0.0) + jnp.log(1.0 + jnp.exp(-jnp.abs(z)))) - 0.5
    lw_ref[...] = -jnp.exp(wlog)
    k = _fdot(mixed(2), wk_ref[...])
    xv = mixed(3)
    v = _fdot(xv, wv_ref[...])
    if has_vres:
        v = v + (vf_ref[...] - v) * _sigmoid(v0 + _bdot(_fdot(xv, v1_ref[...]), v2_ref[...]))
    v_ref[...] = v
    a = _sigmoid(a0 + _bdot(_fdot(mixed(4), a1_ref[...]), a2_ref[...]))
    a_ref[...] = a
    g_ref[...] = _bdot(_sigmoid(_fdot(mixed(5), g1_ref[...])), g2_ref[...])
    kk_ref[...] = k * k_k
    km_ref[...] = k * (1.0 + (a - 1.0) * k_a)


def _pad_cols(w):
    return jnp.pad(w, ((0, 0), (0, LORA_PAD - w.shape[1]))).astype(BF16)


def _pad_rows(w):
    return jnp.pad(w, ((0, LORA_PAD - w.shape[0]), (0, 0))).astype(BF16)


def _rwkv_proj(lay, h, hs, v_first, mix, vecs, wr, wk, wv, w1, w2, a1, a2, g1, g2, v1, v2):
    has_vres = v_first is not None
    sq = _full_spec((D_MODEL, D_MODEL))
    down = _full_spec((D_MODEL, LORA_PAD))
    up = _full_spec((LORA_PAD, D_MODEL))
    in_specs = [lay.tok_spec(), lay.tok_spec()] + ([lay.tok_spec()] if has_vres else [])
    in_specs += [_full_spec((6, D_MODEL)), _full_spec((8, D_MODEL)), sq, sq, sq, down, up, down, up, down, up]
    args = [h, hs] + ([v_first] if has_vres else [])
    args += [mix, vecs, wr.astype(BF16), wk.astype(BF16), wv.astype(BF16), _pad_cols(w1), _pad_rows(w2),
             _pad_cols(a1), _pad_rows(a2), _pad_cols(g1), _pad_rows(g2)]
    if has_vres:
        in_specs += [down, up]
        args += [_pad_cols(v1), _pad_rows(v2)]
    shp = jax.ShapeDtypeStruct((lay.n, D_MODEL), F32)
    return pl.pallas_call(
        functools.partial(_rwkv_proj_kernel, has_vres=has_vres),
        grid=(lay.n_tiles,),
        in_specs=in_specs,
        out_specs=[lay.tok_spec()] * 7,
        out_shape=[shp] * 7,
        compiler_params=_params("arbitrary"),
        name="rwkv_proj",
    )(*args)


def _wkv_kernel(r_ref, km_ref, kk_ref, v_ref, lw_ref, a_ref, g_ref, rk_ref, lnw_ref, lnb_ref, s0_ref,
                y_ref, s_ref, st_ref, *, chunk, heads, seqs):
    t = pl.program_id(2)
    L = chunk

    pairs = heads // 2
    pw = 2 * HEAD_SIZE
    lo, hi = slice(0, HEAD_SIZE), slice(HEAD_SIZE, pw)
    chains = [(q, p) for q in range(seqs) for p in range(pairs)]

    @pl.when(t == 0)
    def _():
        st_ref[...] = jnp.zeros_like(st_ref)
        for n, (q, p) in enumerate(chains):
            st_ref[n, lo, lo] = s0_ref[q, 2 * p]
            st_ref[n, hi, hi] = s0_ref[q, 2 * p + 1]

    row = lax.broadcasted_iota(jnp.int32, (L, 2 * L), 0)
    col = lax.broadcasted_iota(jnp.int32, (L, 2 * L), 1)
    col = jnp.where(col >= L, col - L, col)
    strict2 = row > col
    incl2 = row >= col
    tri = incl2[:, :L].astype(F32)
    even = lax.broadcasted_iota(jnp.int32, (1, pw), 1) < HEAD_SIZE
    same_head = ((lax.broadcasted_iota(jnp.int32, (pw, pw), 0) < HEAD_SIZE)
                 == (lax.broadcasted_iota(jnp.int32, (pw, pw), 1) < HEAD_SIZE))
    ones_bd = same_head.astype(BF16)

    def head_sum(x):
        x_hi = x.astype(BF16)
        x_lo = (x - x_hi.astype(F32)).astype(BF16)
        return _fdot(x_hi, ones_bd) + _fdot(x_lo, ones_bd)

    pick = lambda x0, x1: jnp.where(even, x0, x1)

    each = lambda f, *cols: [f(*xs) for xs in zip(*cols)]
    sls = [slice(p * pw, (p + 1) * pw) for _, p in chains]
    r, km, kkr, v, lw, asig, g = ([ref[q, :, slice(p * pw, (p + 1) * pw)] for q, p in chains]
                                  for ref in (r_ref, km_ref, kk_ref, v_ref, lw_ref, a_ref, g_ref))
    s0 = [st_ref[n] for n in range(len(chains))]
    c = each(lambda x: _fdot(tri, x), lw)
    kk = each(lambda x: x * lax.rsqrt(jnp.maximum(head_sum(x * x), 1e-24)), kkr)
    b = each(lambda x, y: x * y, kk, asig)
    c_last = each(lambda x: x[L - 1:L, :], c)
    ginv = each(lambda x: jnp.exp(-x), c)
    lhs = each(lambda kk_, c_, lw_, r_: jnp.concatenate([-kk_ * jnp.exp(c_ - lw_), r_ * jnp.exp(c_)], axis=0),
               kk, c, lw, r)
    rhs = each(lambda b_, km_, gi: jnp.concatenate([b_ * gi, km_ * gi], axis=0), b, km, ginv)
    qk0 = each(lambda l_, r_: _fdot_nt(jnp.where(even, l_, 0.0), r_), lhs, rhs)
    qk1 = each(lambda l_, r_: _fdot_nt(jnp.where(even, 0.0, l_), r_), lhs, rhs)
    hs0 = each(_fdot_nt, lhs, s0)
    zv = each(lambda v_: jnp.concatenate([jnp.zeros_like(v_), v_], axis=0), v)
    m0 = each(lambda q: jnp.where(strict2, q[:L, :], 0.0), qk0)
    m1 = each(lambda q: jnp.where(strict2, q[:L, :], 0.0), qk1)
    u = each(lambda h_, a0, a1, z_: h_[:L] + pick(_fdot(a0, z_), _fdot(a1, z_)), hs0, m0, m1, zv)
    p0 = each(lambda m: m[:, :L], m0)
    p1 = each(lambda m: m[:, :L], m1)
    for it in range(int(math.log2(L))):
        if it > 0:
            p0 = each(lambda x: _fdot(x, x), p0)
            p1 = each(lambda x: _fdot(x, x), p1)
        u = each(lambda u_, a0, a1: u_ + pick(_fdot(a0, u_), _fdot(a1, u_)), u, p0, p1)
    uv = each(lambda u_, v_: jnp.concatenate([u_, v_], axis=0), u, v)
    bk = each(lambda cl, c_, b_, km_: jnp.concatenate([b_, km_], axis=0)
              * jnp.exp(cl - jnp.concatenate([c_, c_], axis=0)), c_last, c, b, km)
    s_new = each(lambda s_, cl, uv_, bk_: jnp.where(same_head, s_ * jnp.exp(cl) + _fdot_tn(uv_, bk_), 0.0),
                 s0, c_last, uv, bk)
    for n in range(len(chains)):
        st_ref[n] = s_new[n]
    y = each(lambda h_, q0, q1, uv_: h_[L:] + pick(_fdot(jnp.where(incl2, q0[L:, :], 0.0), uv_),
                                                   _fdot(jnp.where(incl2, q1[L:, :], 0.0), uv_)),
             hs0, qk0, qk1, uv)

    def finish(y_, r_, km_, v_, g_, sl):
        yc = y_ - head_sum(y_) * (1.0 / HEAD_SIZE)
        var = head_sum(yc * yc) * (1.0 / HEAD_SIZE)
        yn = yc * lax.rsqrt(var + GN_EPS) * lnw_ref[:, sl] + lnb_ref[:, sl]
        bonus = head_sum(r_ * km_ * rk_ref[:, sl]) * v_
        return (yn + bonus) * g_

    for (q, p), out in zip(chains, each(finish, y, r, km, v, g, sls)):
        y_ref[q, :, slice(p * pw, (p + 1) * pw)] = out

    @pl.when(t == pl.num_programs(2) - 1)
    def _():
        for n, (q, p) in enumerate(chains):
            s_ref[q, 2 * p] = st_ref[n, lo, lo]
            s_ref[q, 2 * p + 1] = st_ref[n, hi, hi]


def _wkv(batch, seq, chunk, heads, seqs, r, km, kk, v, lw, a, g, r_k, lnw, lnb, s0):
    n_t = seq // chunk
    width = heads * HEAD_SIZE
    tok = pl.BlockSpec((seqs, chunk, width), lambda b, h, t: (b, t, h))
    vec = pl.BlockSpec((1, width), lambda b, h, t: (0, h))
    st = pl.BlockSpec((seqs, heads, HEAD_SIZE, HEAD_SIZE), lambda b, h, t: (b, h, 0, 0))
    as3d = lambda x: x.reshape(batch, seq, D_MODEL)
    y, s_new = pl.pallas_call(
        functools.partial(_wkv_kernel, chunk=chunk, heads=heads, seqs=seqs),
        grid=(batch // seqs, N_HEADS // heads, n_t),
        in_specs=[tok] * 7 + [vec] * 3 + [st],
        out_specs=[tok, st],
        out_shape=[jax.ShapeDtypeStruct((batch, seq, D_MODEL), F32),
                   jax.ShapeDtypeStruct((batch, N_HEADS, HEAD_SIZE, HEAD_SIZE), F32)],
        scratch_shapes=[pltpu.VMEM((seqs * heads // 2, 2 * HEAD_SIZE, 2 * HEAD_SIZE), F32)],
        compiler_params=_params("arbitrary", "arbitrary", "arbitrary"),
        name="wkv",
    )(*(as3d(x) for x in (r, km, kk, v, lw, a, g)),
      r_k.reshape(1, D_MODEL), lnw.reshape(1, D_MODEL), lnb.reshape(1, D_MODEL), s0)
    return y.reshape(batch * seq, D_MODEL), s_new


SUBLANES = 8


def _col_reduce(x3, op):
    m = x3[0]
    for r in range(1, x3.shape[0]):
        m = op(m, x3[r])
    for shift in (4, 2, 1):
        m = op(m, pltpu.roll(m, shift, axis=0))
    return m


def _top16(s3, want_rank):
    work = s3
    rank = jnp.full(s3.shape, float(PEER_TOPK), F32) if want_rank else None
    vals = []
    for k in range(PEER_TOPK):
        m = _col_reduce(work, jnp.maximum)
        hit = work == m[None]
        if want_rank:
            rank = jnp.where(hit, float(k), rank)
        work = jnp.where(hit, -jnp.inf, work)
        vals.append(m)
    return vals, work, rank


def _rows_from(vals):
    row = lax.broadcasted_iota(jnp.int32, vals[0].shape, 0)
    out = vals[0]
    for s in range(1, SUBLANES):
        out = jnp.where(row == s, vals[s], out)
    return out


def _peer_select_chunk(s1, s2):
    n_keys, w = s1.shape
    s1 = s1.reshape(n_keys // SUBLANES, SUBLANES, w)
    s2 = s2.reshape(n_keys // SUBLANES, SUBLANES, w)
    v1, left1, _ = _top16(s1, False)
    v2, _, rank2 = _top16(s2, True)
    v2_lo, v2_hi, v1_hi = _rows_from(v2[:SUBLANES]), _rows_from(v2[SUBLANES:]), _rows_from(v1[SUBLANES:])
    cand = jnp.stack([v1[0] + v2_lo, v1[0] + v2_hi] + [v1[a] + v2_lo for a in range(1, SUBLANES)]
                     + [v1_hi + v2[0]])
    work = cand
    tau = None
    for _ in range(PEER_TOPK):
        tau = _col_reduce(work, jnp.maximum)
        work = jnp.where(work == tau[None], -jnp.inf, work)
    top = v1[0] + v2[0]
    zsum = _col_reduce(jnp.where(cand >= tau[None], jnp.exp(cand - top[None]), 0.0), jnp.add)
    cnt = jnp.zeros(s1.shape, F32)
    for b in range(PEER_TOPK):
        cnt = cnt + jnp.where(s1 + v2[b][None] >= tau[None], 1.0, 0.0)
    cnt = jnp.where(left1 == -jnp.inf, cnt, 0.0)
    e1 = jnp.exp(s1 - v1[0][None])
    e2 = jnp.exp(s2 - v2[0][None]) / zsum[None]
    return tuple(t.reshape(n_keys, w) for t in (cnt, e1, rank2, e2))


def _peer_select_kernel(x_ref, g_ref, sc_ref, sh_ref, wq_ref, k1_ref, k2_ref,
                        ht_ref, c_ref, e1_ref, rk_ref, e2_ref, q_scr):
    h = _rms_mod(x_ref[...], g_ref[...], sc_ref[0], sh_ref[0])
    ht = h.T.astype(BF16)
    ht_ref[...] = ht
    q_scr[...] = _fdot(wq_ref[...], ht)
    half = D_QUERY // 2
    n_chunks = x_ref.shape[0] // PEER_LANE_CHUNK

    def head_body(hd, carry):
        q1 = q_scr[pl.ds(pl.multiple_of(hd * D_QUERY, D_QUERY), half), :]
        q2 = q_scr[pl.ds(pl.multiple_of(hd * D_QUERY + half, half), half), :]
        s1 = _bdot(k1_ref[hd], q1)
        s2 = _bdot(k2_ref[hd], q2)
        for ch in range(n_chunks):
            ls = slice(ch * PEER_LANE_CHUNK, (ch + 1) * PEER_LANE_CHUNK)
            cnt, e1, rank2, e2 = _peer_select_chunk(s1[:, ls], s2[:, ls])
            c_ref[hd, :, ls] = cnt
            e1_ref[hd, :, ls] = e1
            rk_ref[hd, :, ls] = rank2.astype(BF16)
            e2_ref[hd, :, ls] = e2.astype(BF16)
        return carry

    lax.fori_loop(0, PEER_HEADS, head_body, 0)


def _peer_select(lay, x, g, sc, sh, wq_t, k1, k2):
    tt = lay.tile
    sel_spec = pl.BlockSpec((PEER_HEADS, N_KEYS, tt), lambda i: (0, 0, i))
    sel_shape = lambda dt: jax.ShapeDtypeStruct((PEER_HEADS, N_KEYS, lay.n), dt)
    return pl.pallas_call(
        _peer_select_kernel,
        grid=(lay.n_tiles,),
        in_specs=[lay.tok_spec(), _full_spec((1, D_MODEL)), lay.mod_spec(), lay.mod_spec(),
                  _full_spec((PEER_HEADS * D_QUERY, D_MODEL)),
                  _full_spec((PEER_HEADS, N_KEYS, D_QUERY // 2)), _full_spec((PEER_HEADS, N_KEYS, D_QUERY // 2))],
        out_specs=[pl.BlockSpec((D_MODEL, tt), lambda i: (0, i))] + [sel_spec] * 4,
        out_shape=[jax.ShapeDtypeStruct((D_MODEL, lay.n), BF16),
                   sel_shape(F32), sel_shape(F32), sel_shape(BF16), sel_shape(BF16)],
        scratch_shapes=[pltpu.VMEM((PEER_HEADS * D_QUERY, tt), F32)],
        compiler_params=_params("arbitrary"),
        name="peer_select",
    )(x, g.reshape(1, D_MODEL), sc, sh, wq_t, k1, k2)


def _peer_dense_kernel(*refs):
    groups = PEER_EXPERT_TILE // PEER_GROUP
    ht_ref, c_ref, e1_ref, rk_ref, e2_ref = refs[:5]
    u_refs = refs[5:5 + groups]
    vt_refs = refs[5 + groups:5 + 2 * groups]
    x_ref, gt_ref, o_ref, acc_ref, coef_ref, z_ref = refs[5 + 2 * groups:]
    e = pl.program_id(1)
    sr = PEER_STEP_ROWS
    spg = PEER_GROUP // (N_KEYS * sr)
    piece = D_MODEL // spg
    last_slot = (groups - 1) % 2

    @pl.when(e == 0)
    def _():
        acc_ref[...] = jnp.zeros_like(acc_ref)

    def pre_act(step):
        g, ii = divmod(step, spg)
        return _fdot(u_refs[g][0, ii * sr * N_KEYS:(ii + 1) * sr * N_KEYS, :], ht_ref[...])

    z_ref[0] = pre_act(0)
    for g in range(groups):
        slot = g % 2
        for ii in range(spg):
            step = g * spg + ii
            if step + 1 < groups * spg:
                z_ref[(step + 1) % 2] = pre_act(step + 1)
            if g > 0:
                ps = slice(ii * piece, (ii + 1) * piece)
                acc_ref[ps, :] += _fdot(vt_refs[g - 1][0, ps, :], coef_ref[1 - slot])
            for rr in range(sr):
                row = step * sr + rr
                gate = None
                for hd in range(PEER_HEADS):
                    cnt = c_ref[hd, row:row + 1, :].astype(BF16)
                    e1 = e1_ref[hd, row:row + 1, :].astype(BF16)
                    term = jnp.where(rk_ref[hd] < cnt, e2_ref[hd], jnp.zeros((), BF16)) * e1
                    gate = term if gate is None else gate + term
                z = z_ref[step % 2, rr * N_KEYS:(rr + 1) * N_KEYS, :]
                act = 0.5 * z * (1.0 + lax.erf(z * (1.0 / math.sqrt(2.0))))
                lo = (ii * sr + rr) * N_KEYS
                coef_ref[slot, lo:lo + N_KEYS, :] = gate * act.astype(BF16)

    acc_ref[...] += _fdot(vt_refs[groups - 1][0], coef_ref[last_slot])

    @pl.when(e == pl.num_programs(1) - 1)
    def _():
        o_ref[...] = x_ref[...] + gt_ref[0] * acc_ref[...].T


def _peer_weight_parts(peer_u, peer_v):
    groups = PEER_EXPERT_TILE // PEER_GROUP
    tiles = N_EXPERTS // PEER_EXPERT_TILE
    u5 = peer_u.astype(BF16).reshape(DEPTH, tiles, groups, PEER_GROUP, D_MODEL)
    v5 = jnp.swapaxes(peer_v.astype(BF16).reshape(DEPTH, tiles, groups, PEER_GROUP, D_MODEL), 3, 4)
    u_parts = [u5[:, :, k] for k in range(groups)]
    vt_parts = [v5[:, :, k] for k in range(groups)]
    return u_parts, vt_parts


def _peer_dense(lay, ht, cnt, e1, rank2, e2, u_parts, vt_parts, x, gt):
    tt = lay.tile
    groups = PEER_EXPERT_TILE // PEER_GROUP
    rows_per_tile = PEER_EXPERT_TILE // N_KEYS
    tpg = lay.tiles_per_group
    row_spec = pl.BlockSpec((PEER_HEADS, rows_per_tile, tt), lambda i, e: (0, e, i))
    key_spec = pl.BlockSpec((PEER_HEADS, N_KEYS, tt), lambda i, e: (0, 0, i))
    u_spec = pl.BlockSpec((1, PEER_GROUP, D_MODEL), lambda i, e: (e, 0, 0))
    vt_spec = pl.BlockSpec((1, D_MODEL, PEER_GROUP), lambda i, e: (e, 0, 0))
    return pl.pallas_call(
        _peer_dense_kernel,
        grid=(lay.n_tiles, N_EXPERTS // PEER_EXPERT_TILE),
        in_specs=[pl.BlockSpec((D_MODEL, tt), lambda i, e: (0, i)), row_spec, row_spec, key_spec, key_spec]
        + [u_spec] * groups + [vt_spec] * groups + [
            pl.BlockSpec((tt, D_MODEL), lambda i, e: (i, 0)),
            pl.BlockSpec((1, lay.rows, D_MODEL), lambda i, e: (i // tpg, 0, 0))],
        out_specs=pl.BlockSpec((tt, D_MODEL), lambda i, e: (i, 0)),
        out_shape=jax.ShapeDtypeStruct((lay.n, D_MODEL), F32),
        scratch_shapes=[pltpu.VMEM((D_MODEL, tt), F32), pltpu.VMEM((2, PEER_GROUP, tt), BF16),
                        pltpu.VMEM((2, PEER_STEP_ROWS * N_KEYS, tt), F32)],
        compiler_params=_params("arbitrary", "arbitrary"),
        name="peer_dense",
    )(ht, cnt, e1, rank2, e2, *u_parts, *vt_parts, x, gt)


def _run(batch, seq, x, m_all, st_conv, st_shift, st_wkv, p):
    lay = _Layout(batch, seq)
    x = x.reshape(lay.n, D_MODEL)
    new_conv, new_shift, new_wkv = [], [], []
    v_first = None
    short = seq < WKV_CHUNK
    for i in range(DEPTH):
        sh_m, sc_m, gt_m, sh_f, sc_f, gt_f = (lay.expand(t) for t in jnp.split(m_all[i], 6, axis=-1))
        j = i // 2
        if i % 2 == 0:
            bg, u = _conv_in(lay, x, p["g_mix"][i], sc_m, sh_m, p["conv_w_in"][j])
            up = jnp.concatenate([st_conv[j], u.reshape(batch, seq, D_MODEL)], axis=1)
            u0 = up[:, 0:seq].reshape(lay.n, D_MODEL)
            u1 = up[:, 1:seq + 1].reshape(lay.n, D_MODEL)
            new_conv.append(up[:, -(CONV_WIDTH - 1):])
            x = _conv_out(lay, bg, u0, u1, u, p["conv_k"][j], p["conv_w_out"][j], x, gt_m)
        else:
            h = _norm_mod(lay, x, p["g_mix"][i], sc_m, sh_m)
            h3 = h.reshape(batch, seq, D_MODEL)
            hs = jnp.concatenate([st_shift[j][:, None, :], h3[:, :-1]], axis=1).reshape(lay.n, D_MODEL)
            new_shift.append(h3[:, -1])
            zero = jnp.zeros((D_MODEL,), F32)
            v0 = p["rwkv_v0"][j - 1] if j > 0 else zero
            vecs = jnp.stack([p["rwkv_w0"][j], p["rwkv_a0"][j], v0, p["rwkv_k_k"][j], p["rwkv_k_a"][j],
                              zero, zero, zero])
            v1 = p["rwkv_v1"][j - 1] if j > 0 else None
            v2 = p["rwkv_v2"][j - 1] if j > 0 else None
            r, km, kk, v, lw, a, g = _rwkv_proj(
                _Layout(batch, seq, RWKV_PROJ_TILE), h, hs, v_first if j > 0 else None, p["rwkv_mix"][j], vecs,
                p["rwkv_wr"][j], p["rwkv_wk"][j], p["rwkv_wv"][j], p["rwkv_w1"][j], p["rwkv_w2"][j],
                p["rwkv_a1"][j], p["rwkv_a2"][j], p["rwkv_g1"][j], p["rwkv_g2"][j], v1, v2)
            if j == 0:
                v_first = v
            scan_in = (r, km, kk, v, lw, a, g)
            if short:
                pad = lambda t: jnp.pad(t.reshape(batch, seq, D_MODEL),
                                        ((0, 0), (0, WKV_SHORT_CHUNK - seq), (0, 0))).reshape(-1, D_MODEL)
                scan_in = tuple(pad(t) for t in scan_in)
                yg, s_new = _wkv(batch, WKV_SHORT_CHUNK, WKV_SHORT_CHUNK, N_HEADS, WKV_SHORT_SEQS_PER_STEP, *scan_in,
                                 p["rwkv_r_k"][j], p["rwkv_lnw"][j], p["rwkv_lnb"][j], st_wkv[j])
                yg = yg.reshape(batch, WKV_SHORT_CHUNK, D_MODEL)[:, :seq].reshape(lay.n, D_MODEL)
            else:
                yg, s_new = _wkv(batch, seq, WKV_CHUNK, N_HEADS, WKV_SEQS_PER_STEP, *scan_in,
                                 p["rwkv_r_k"][j], p["rwkv_lnw"][j], p["rwkv_lnb"][j], st_wkv[j])
            new_wkv.append(s_new)
            x = _mm_res(lay, yg, p["rwkv_wo"][j], x, gt_m)
        ht, cnt, e1, rank2, e2 = _peer_select(lay, x, p["g_ffn"][i], sc_f, sh_f,
                                              p["wq_t"][i], p["k1_b"][i], p["k2_b"][i])
        x = _peer_dense(lay, ht, cnt, e1, rank2, e2, [t[i] for t in p["u_parts"]],
                        [t[i] for t in p["vt_parts"]], x, gt_f)
    y = _final_norm(lay, x, p["g_final"]).reshape(batch, seq, D_MODEL)
    return y, jnp.stack(new_conv), jnp.stack(new_shift), jnp.stack(new_wkv)


def kernel(x_prompt, x_sample, state_conv, state_shift, state_wkv, c_prompt, c_sample, w_ada, b_ada, g_mix, g_ffn, g_final, conv_w_in, conv_k, conv_w_out, rwkv_mix, rwkv_wr, rwkv_wk, rwkv_wv, rwkv_wo, rwkv_w0, rwkv_w1, rwkv_w2, rwkv_a0, rwkv_a1, rwkv_a2, rwkv_v0, rwkv_v1, rwkv_v2, rwkv_g1, rwkv_g2, rwkv_k_k, rwkv_k_a, rwkv_r_k, rwkv_lnw, rwkv_lnb, peer_wq, peer_k1, peer_k2, peer_u, peer_v):
    p = dict(
        g_mix=g_mix, g_ffn=g_ffn, g_final=g_final, conv_w_in=conv_w_in, conv_k=conv_k, conv_w_out=conv_w_out,
        rwkv_mix=rwkv_mix, rwkv_wr=rwkv_wr, rwkv_wk=rwkv_wk, rwkv_wv=rwkv_wv, rwkv_wo=rwkv_wo,
        rwkv_w0=rwkv_w0, rwkv_w1=rwkv_w1, rwkv_w2=rwkv_w2, rwkv_a0=rwkv_a0, rwkv_a1=rwkv_a1, rwkv_a2=rwkv_a2,
        rwkv_v0=rwkv_v0, rwkv_v1=rwkv_v1, rwkv_v2=rwkv_v2, rwkv_g1=rwkv_g1, rwkv_g2=rwkv_g2,
        rwkv_k_k=rwkv_k_k, rwkv_k_a=rwkv_k_a, rwkv_r_k=rwkv_r_k, rwkv_lnw=rwkv_lnw, rwkv_lnb=rwkv_lnb,
        wq_t=jnp.swapaxes(peer_wq, 1, 2).astype(BF16),
        k1_b=peer_k1.astype(BF16), k2_b=peer_k2.astype(BF16),
        **dict(zip(("u_parts", "vt_parts"), _peer_weight_parts(peer_u, peer_v))),
    )
    n_prompt, seq_prompt = x_prompt.shape[0], x_prompt.shape[1]
    n_sample, seq_sample = x_sample.shape[0], x_sample.shape[1]
    m_all = _adaln(jnp.concatenate([c_prompt, c_sample], axis=0), w_ada, b_ada)
    zeros = lambda *s: jnp.zeros(s, F32)
    n_conv, n_rwkv = (DEPTH + 1) // 2, DEPTH // 2
    y_p, p_conv, p_shift, p_wkv = _run(
        n_prompt, seq_prompt, x_prompt, m_all[:, :n_prompt],
        zeros(n_conv, n_prompt, CONV_WIDTH - 1, D_MODEL), zeros(n_rwkv, n_prompt, D_MODEL),
        zeros(n_rwkv, n_prompt, N_HEADS, HEAD_SIZE, HEAD_SIZE), p)
    y_s, s_conv, s_shift, s_wkv = _run(
        n_sample, seq_sample, x_sample, m_all[:, n_prompt:], state_conv, state_shift, state_wkv, p)
    return (y_p, y_s, p_conv, p_shift, p_wkv, s_conv, s_shift, s_wkv)
```

```python
import functools
import math

import jax
import jax.numpy as jnp
from jax import lax
from jax.experimental import pallas as pl
from jax.experimental.pallas import tpu as pltpu

D_MODEL = 1024
DEPTH = 4
CONV_WIDTH = 3
HEAD_SIZE = 64
N_HEADS = D_MODEL // HEAD_SIZE
N_KEYS = 128
N_EXPERTS = N_KEYS * N_KEYS
PEER_HEADS = 8
PEER_TOPK = 16
D_QUERY = 256
RMS_EPS = 1e-6
GN_EPS = 64e-5

LORA_PAD = 128
TOKEN_TILE = 512
RWKV_PROJ_TILE = 256
PEER_EXPERT_TILE = 2048
PEER_GROUP = 512
PEER_STEP_ROWS = 4
PEER_LANE_CHUNK = 256
WKV_CHUNK = 64
WKV_SHORT_CHUNK = 8
WKV_SEQS_PER_STEP = 2
WKV_SHORT_SEQS_PER_STEP = 4
VMEM_LIMIT = 56 * 1024 * 1024

F32 = jnp.float32
BF16 = jnp.bfloat16


def _params(*sem):
    return pltpu.CompilerParams(dimension_semantics=sem, vmem_limit_bytes=VMEM_LIMIT)


def _sigmoid(x):
    return 1.0 / (1.0 + jnp.exp(-x))


def _rms_mod(x, g, sc, sh):
    y = x * lax.rsqrt(jnp.mean(x * x, axis=-1, keepdims=True) + RMS_EPS)
    return (y * g) * (1.0 + sc) + sh


def _bdot(a, b):
    return jnp.dot(a.astype(BF16), b.astype(BF16), preferred_element_type=F32)


def _fdot(a, b):
    return jnp.dot(a, b, preferred_element_type=F32)


def _fdot_nt(a, b):
    return lax.dot_general(a, b, (((1,), (1,)), ((), ())), preferred_element_type=F32)


def _fdot_tn(a, b):
    return lax.dot_general(a, b, (((0,), (0,)), ((), ())), preferred_element_type=F32)


class _Layout:
    def __init__(self, batch, seq, tile=TOKEN_TILE):
        self.batch, self.seq = batch, seq
        self.n = batch * seq
        self.tile = min(tile, self.n)
        assert self.n % self.tile == 0
        self.n_tiles = self.n // self.tile
        if seq % self.tile == 0:
            self.rows = 1
            self.tiles_per_group = seq // self.tile
        else:
            assert self.tile % seq == 0
            self.rows = self.tile
            self.tiles_per_group = 1

    def expand(self, m):
        if self.rows == 1:
            return m[:, None, :]
        return jnp.repeat(m, self.seq, axis=0).reshape(self.n_tiles, self.tile, m.shape[-1])

    def mod_spec(self):
        tpg = self.tiles_per_group
        return pl.BlockSpec((1, self.rows, D_MODEL), lambda i: (i // tpg, 0, 0))

    def tok_spec(self, width=D_MODEL):
        return pl.BlockSpec((self.tile, width), lambda i: (i, 0))


def _full_spec(shape):
    nd = len(shape)
    return pl.BlockSpec(shape, lambda *_: (0,) * nd)


def _adaln_kernel(c_ref, w_ref, b_ref, o_ref):
    c = c_ref[...]
    cs = c * _sigmoid(c)
    o_ref[0] = _fdot(cs, w_ref[0]) + b_ref[0]


def _adaln(c, w_ada, b_ada):
    nb = c.shape[0]
    tn = 1536
    return pl.pallas_call(
        _adaln_kernel,
        grid=(DEPTH, 6 * D_MODEL // tn),
        in_specs=[
            pl.BlockSpec((nb, D_MODEL), lambda l, j: (0, 0)),
            pl.BlockSpec((1, D_MODEL, tn), lambda l, j: (l, 0, j)),
            pl.BlockSpec((1, 1, tn), lambda l, j: (l, 0, j)),
        ],
        out_specs=pl.BlockSpec((1, nb, tn), lambda l, j: (l, 0, j)),
        out_shape=jax.ShapeDtypeStruct((DEPTH, nb, 6 * D_MODEL), F32),
        compiler_params=_params("arbitrary", "arbitrary"),
        name="adaln",
    )(c, w_ada, b_ada.reshape(DEPTH, 1, 6 * D_MODEL))


def _norm_mod_kernel(x_ref, g_ref, sc_ref, sh_ref, o_ref):
    o_ref[...] = _rms_mod(x_ref[...], g_ref[...], sc_ref[0], sh_ref[0])


def _norm_mod(lay, x, g, sc, sh):
    return pl.pallas_call(
        _norm_mod_kernel,
        grid=(lay.n_tiles,),
        in_specs=[lay.tok_spec(), _full_spec((1, D_MODEL)), lay.mod_spec(), lay.mod_spec()],
        out_specs=lay.tok_spec(),
        out_shape=jax.ShapeDtypeStruct((lay.n, D_MODEL), F32),
        compiler_params=_params("arbitrary"),
        name="norm_mod",
    )(x, g.reshape(1, D_MODEL), sc, sh)


def _final_norm_kernel(x_ref, g_ref, o_ref):
    x = x_ref[...]
    o_ref[...] = x * lax.rsqrt(jnp.mean(x * x, axis=-1, keepdims=True) + RMS_EPS) * g_ref[...]


def _final_norm(lay, x, g):
    return pl.pallas_call(
        _final_norm_kernel,
        grid=(lay.n_tiles,),
        in_specs=[lay.tok_spec(), _full_spec((1, D_MODEL))],
        out_specs=lay.tok_spec(),
        out_shape=jax.ShapeDtypeStruct((lay.n, D_MODEL), F32),
        compiler_params=_params("arbitrary"),
        name="final_norm",
    )(x, g.reshape(1, D_MODEL))


def _conv_in_kernel(x_ref, g_ref, sc_ref, sh_ref, w_ref, bg_ref, u_ref):
    h = _rms_mod(x_ref[...], g_ref[...], sc_ref[0], sh_ref[0]).astype(BF16)
    d = D_MODEL
    bg_ref[...] = _fdot(h, w_ref[:, 0:d])
    u_ref[...] = _fdot(h, w_ref[:, d:2 * d]) * _fdot(h, w_ref[:, 2 * d:3 * d])


def _conv_in(lay, x, g, sc, sh, w_in):
    shp = jax.ShapeDtypeStruct((lay.n, D_MODEL), F32)
    return pl.pallas_call(
        _conv_in_kernel,
        grid=(lay.n_tiles,),
        in_specs=[lay.tok_spec(), _full_spec((1, D_MODEL)), lay.mod_spec(), lay.mod_spec(),
                  _full_spec((D_MODEL, 3 * D_MODEL))],
        out_specs=[lay.tok_spec(), lay.tok_spec()],
        out_shape=[shp, shp],
        compiler_params=_params("arbitrary"),
        name="conv_in",
    )(x, g.reshape(1, D_MODEL), sc, sh, w_in.astype(BF16))


def _conv_out_kernel(bg_ref, u0_ref, u1_ref, u2_ref, ck_ref, w_ref, x_ref, gt_ref, o_ref):
    conv = u0_ref[...] * ck_ref[0:1, :] + u1_ref[...] * ck_ref[1:2, :] + u2_ref[...] * ck_ref[2:3, :]
    y = _bdot(bg_ref[...] * conv, w_ref[...])
    o_ref[...] = x_ref[...] + gt_ref[0] * y


def _conv_out(lay, bg, u0, u1, u2, conv_k, w_out, x, gt):
    return pl.pallas_call(
        _conv_out_kernel,
        grid=(lay.n_tiles,),
        in_specs=[lay.tok_spec()] * 4 + [_full_spec((CONV_WIDTH, D_MODEL)), _full_spec((D_MODEL, D_MODEL)),
                                         lay.tok_spec(), lay.mod_spec()],
        out_specs=lay.tok_spec(),
        out_shape=jax.ShapeDtypeStruct((lay.n, D_MODEL), F32),
        compiler_params=_params("arbitrary"),
        name="conv_out",
    )(bg, u0, u1, u2, conv_k, w_out.astype(BF16), x, gt)


def _conv_fused_kernel(x_ref, g_ref, sc_ref, sh_ref, win_ref, ck_ref, wout_ref, gt_ref, st_ref,
                       o_ref, ns_ref, carry_ref, *, tiles_per_seq):
    x = x_ref[...]
    h = _rms_mod(x, g_ref[...], sc_ref[0], sh_ref[0]).astype(BF16)
    d = D_MODEL
    rows = x.shape[0]
    u = _fdot(h, win_ref[:, d:2 * d]) * _fdot(h, win_ref[:, 2 * d:3 * d])
    first = pl.program_id(0) % tiles_per_seq == 0
    prev = jnp.where(first, st_ref[0], carry_ref[...])
    tail = u[rows - (CONV_WIDTH - 1):, :]
    carry_ref[...] = tail
    ns_ref[0] = tail
    row = lax.broadcasted_iota(jnp.int32, u.shape, 0)
    um1 = jnp.where(row == 0, prev[1:2, :], pltpu.roll(u, 1, axis=0))
    um2 = jnp.where(row == 0, prev[0:1, :], jnp.where(row == 1, prev[1:2, :], pltpu.roll(u, 2, axis=0)))
    conv = um2 * ck_ref[0:1, :] + um1 * ck_ref[1:2, :] + u * ck_ref[2:3, :]
    bg = _fdot(h, win_ref[:, 0:d])
    o_ref[...] = x + gt_ref[0] * _bdot(bg * conv, wout_ref[...])


def _conv_fused(lay, x, g, sc, sh, w_in, conv_k, w_out, gt, st):
    assert lay.rows == 1 and CONV_WIDTH == 3
    tps = lay.tiles_per_group
    st_spec = pl.BlockSpec((1, CONV_WIDTH - 1, D_MODEL), lambda i: (i // tps, 0, 0))
    return pl.pallas_call(
        functools.partial(_conv_fused_kernel, tiles_per_seq=tps),
        grid=(lay.n_tiles,),
        in_specs=[lay.tok_spec(), _full_spec((1, D_MODEL)), lay.mod_spec(), lay.mod_spec(),
                  _full_spec((D_MODEL, 3 * D_MODEL)), _full_spec((CONV_WIDTH, D_MODEL)),
                  _full_spec((D_MODEL, D_MODEL)), lay.mod_spec(), st_spec],
        out_specs=[lay.tok_spec(), st_spec],
        out_shape=[jax.ShapeDtypeStruct((lay.n, D_MODEL), F32),
                   jax.ShapeDtypeStruct((lay.batch, CONV_WIDTH - 1, D_MODEL), F32)],
        scratch_shapes=[pltpu.VMEM((CONV_WIDTH - 1, D_MODEL), F32)],
        compiler_params=_params("arbitrary"),
        name="conv_fused",
    )(x, g.reshape(1, D_MODEL), sc, sh, w_in.astype(BF16), conv_k, w_out.astype(BF16), gt, st)


def _mm_res_kernel(a_ref, w_ref, x_ref, gt_ref, o_ref):
    o_ref[...] = x_ref[...] + gt_ref[0] * _bdot(a_ref[...], w_ref[...])


def _mm_res(lay, a, w, x, gt):
    return pl.pallas_call(
        _mm_res_kernel,
        grid=(lay.n_tiles,),
        in_specs=[lay.tok_spec(), _full_spec((D_MODEL, D_MODEL)), lay.tok_spec(), lay.mod_spec()],
        out_specs=lay.tok_spec(),
        out_shape=jax.ShapeDtypeStruct((lay.n, D_MODEL), F32),
        compiler_params=_params("arbitrary"),
        name="mm_res",
    )(a, w.astype(BF16), x, gt)


def _rwkv_proj_kernel(*refs, has_vres):
    if has_vres:
        (h_ref, hs_ref, vf_ref, mix_ref, vec_ref, wr_ref, wk_ref, wv_ref, w1_ref, w2_ref, a1_ref, a2_ref,
         g1_ref, g2_ref, v1_ref, v2_ref, r_ref, km_ref, kk_ref, v_ref, lw_ref, a_ref, g_ref) = refs
    else:
        (h_ref, hs_ref, mix_ref, vec_ref, wr_ref, wk_ref, wv_ref, w1_ref, w2_ref, a1_ref, a2_ref,
         g1_ref, g2_ref, r_ref, km_ref, kk_ref, v_ref, lw_ref, a_ref, g_ref) = refs
    h = h_ref[...]
    xx = hs_ref[...] - h
    mixed = lambda i: (h + xx * mix_ref[i:i + 1, :]).astype(BF16)
    w0, a0, v0, k_k, k_a = (vec_ref[i:i + 1, :] for i in range(5))

    r_ref[...] = _fdot(mixed(0), wr_ref[...])
    wl = w0 + _bdot(jnp.tanh(_fdot(mixed(1), w1_ref[...])), w2_ref[...])
    z = -wl
    wlog = -(jnp.maximum(z, 0.0) + jnp.log(1.0 + jnp.exp(-jnp.abs(z)))) - 0.5
    lw_ref[...] = -jnp.exp(wlog)
    k = _fdot(mixed(2), wk_ref[...])
    xv = mixed(3)
    v = _fdot(xv, wv_ref[...])
    if has_vres:
        v = v + (vf_ref[...] - v) * _sigmoid(v0 + _bdot(_fdot(xv, v1_ref[...]), v2_ref[...]))
    v_ref[...] = v
    a = _sigmoid(a0 + _bdot(_fdot(mixed(4), a1_ref[...]), a2_ref[...]))
    a_ref[...] = a
    g_ref[...] = _bdot(_sigmoid(_fdot(mixed(5), g1_ref[...])), g2_ref[...])
    kk_ref[...] = k * k_k
    km_ref[...] = k * (1.0 + (a - 1.0) * k_a)


def _pad_cols(w):
    return jnp.pad(w, ((0, 0), (0, LORA_PAD - w.shape[1]))).astype(BF16)


def _pad_rows(w):
    return jnp.pad(w, ((0, LORA_PAD - w.shape[0]), (0, 0))).astype(BF16)


def _rwkv_proj(lay, h, hs, v_first, mix, vecs, wr, wk, wv, w1, w2, a1, a2, g1, g2, v1, v2):
    has_vres = v_first is not None
    sq = _full_spec((D_MODEL, D_MODEL))
    down = _full_spec((D_MODEL, LORA_PAD))
    up = _full_spec((LORA_PAD, D_MODEL))
    in_specs = [lay.tok_spec(), lay.tok_spec()] + ([lay.tok_spec()] if has_vres else [])
    in_specs += [_full_spec((6, D_MODEL)), _full_spec((8, D_MODEL)), sq, sq, sq, down, up, down, up, down, up]
    args = [h, hs] + ([v_first] if has_vres else [])
    args += [mix, vecs, wr.astype(BF16), wk.astype(BF16), wv.astype(BF16), _pad_cols(w1), _pad_rows(w2),
             _pad_cols(a1), _pad_rows(a2), _pad_cols(g1), _pad_rows(g2)]
    if has_vres:
        in_specs += [down, up]
        args += [_pad_cols(v1), _pad_rows(v2)]
    shp = jax.ShapeDtypeStruct((lay.n, D_MODEL), F32)
    return pl.pallas_call(
        functools.partial(_rwkv_proj_kernel, has_vres=has_vres),
        grid=(lay.n_tiles,),
        in_specs=in_specs,
        out_specs=[lay.tok_spec()] * 7,
        out_shape=[shp] * 7,
        compiler_params=_params("arbitrary"),
        name="rwkv_proj",
    )(*args)


def _wkv_kernel(r_ref, km_ref, kk_ref, v_ref, lw_ref, a_ref, g_ref, rk_ref, lnw_ref, lnb_ref, s0_ref,
                y_ref, s_ref, st_ref, *, chunk, heads, seqs):
    t = pl.program_id(2)
    L = chunk

    pairs = heads // 2
    pw = 2 * HEAD_SIZE
    lo, hi = slice(0, HEAD_SIZE), slice(HEAD_SIZE, pw)
    chains = [(q, p) for q in range(seqs) for p in range(pairs)]

    @pl.when(t == 0)
    def _():
        st_ref[...] = jnp.zeros_like(st_ref)
        for n, (q, p) in enumerate(chains):
            st_ref[n, lo, lo] = s0_ref[q, 2 * p]
            st_ref[n, hi, hi] = s0_ref[q, 2 * p + 1]

    row = lax.broadcasted_iota(jnp.int32, (L, 2 * L), 0)
    col = lax.broadcasted_iota(jnp.int32, (L, 2 * L), 1)
    col = jnp.where(col >= L, col - L, col)
    strict2 = row > col
    incl2 = row >= col
    tri = incl2[:, :L].astype(F32)
    even = lax.broadcasted_iota(jnp.int32, (1, pw), 1) < HEAD_SIZE
    same_head = ((lax.broadcasted_iota(jnp.int32, (pw, pw), 0) < HEAD_SIZE)
                 == (lax.broadcasted_iota(jnp.int32, (pw, pw), 1) < HEAD_SIZE))
    ones_bd = same_head.astype(BF16)

    def head_sum(x):
        x_hi = x.astype(BF16)
        x_lo = (x - x_hi.astype(F32)).astype(BF16)
        return _fdot(x_hi, ones_bd) + _fdot(x_lo, ones_bd)

    pick = lambda x0, x1: jnp.where(even, x0, x1)

    each = lambda f, *cols: [f(*xs) for xs in zip(*cols)]
    sls = [slice(p * pw, (p + 1) * pw) for _, p in chains]
    r, km, kkr, v, lw, asig, g = ([ref[q, :, slice(p * pw, (p + 1) * pw)] for q, p in chains]
                                  for ref in (r_ref, km_ref, kk_ref, v_ref, lw_ref, a_ref, g_ref))
    s0 = [st_ref[n] for n in range(len(chains))]
    c = each(lambda x: _fdot(tri, x), lw)
    kk = each(lambda x: x * lax.rsqrt(jnp.maximum(head_sum(x * x), 1e-24)), kkr)
    b = each(lambda x, y: x * y, kk, asig)
    c_last = each(lambda x: x[L - 1:L, :], c)
    ginv = each(lambda x: jnp.exp(-x), c)
    lhs = each(lambda kk_, c_, lw_, r_: jnp.concatenate([-kk_ * jnp.exp(c_ - lw_), r_ * jnp.exp(c_)], axis=0),
               kk, c, lw, r)
    rhs = each(lambda b_, km_, gi: jnp.concatenate([b_ * gi, km_ * gi], axis=0), b, km, ginv)
    qk0 = each(lambda l_, r_: _fdot_nt(jnp.where(even, l_, 0.0), r_), lhs, rhs)
    qk1 = each(lambda l_, r_: _fdot_nt(jnp.where(even, 0.0, l_), r_), lhs, rhs)
    hs0 = each(_fdot_nt, lhs, s0)
    zv = each(lambda v_: jnp.concatenate([jnp.zeros_like(v_), v_], axis=0), v)
    m0 = each(lambda q: jnp.where(strict2, q[:L, :], 0.0), qk0)
    m1 = each(lambda q: jnp.where(strict2, q[:L, :], 0.0), qk1)
    u = each(lambda h_, a0, a1, z_: h_[:L] + pick(_fdot(a0, z_), _fdot(a1, z_)), hs0, m0, m1, zv)
    p0 = each(lambda m: m[:, :L], m0)
    p1 = each(lambda m: m[:, :L], m1)
    for it in range(int(math.log2(L))):
        if it > 0:
            p0 = each(lambda x: _fdot(x, x), p0)
            p1 = each(lambda x: _fdot(x, x), p1)
        u = each(lambda u_, a0, a1: u_ + pick(_fdot(a0, u_), _fdot(a1, u_)), u, p0, p1)
    uv = each(lambda u_, v_: jnp.concatenate([u_, v_], axis=0), u, v)
    bk = each(lambda cl, c_, b_, km_: jnp.concatenate([b_, km_], axis=0)
              * jnp.exp(cl - jnp.concatenate([c_, c_], axis=0)), c_last, c, b, km)
    s_new = each(lambda s_, cl, uv_, bk_: jnp.where(same_head, s_ * jnp.exp(cl) + _fdot_tn(uv_, bk_), 0.0),
                 s0, c_last, uv, bk)
    for n in range(len(chains)):
        st_ref[n] = s_new[n]
    y = each(lambda h_, q0, q1, uv_: h_[L:] + pick(_fdot(jnp.where(incl2, q0[L:, :], 0.0), uv_),
                                                   _fdot(jnp.where(incl2, q1[L:, :], 0.0), uv_)),
             hs0, qk0, qk1, uv)

    def finish(y_, r_, km_, v_, g_, sl):
        yc = y_ - head_sum(y_) * (1.0 / HEAD_SIZE)
        var = head_sum(yc * yc) * (1.0 / HEAD_SIZE)
        yn = yc * lax.rsqrt(var + GN_EPS) * lnw_ref[:, sl] + lnb_ref[:, sl]
        bonus = head_sum(r_ * km_ * rk_ref[:, sl]) * v_
        return (yn + bonus) * g_

    for (q, p), out in zip(chains, each(finish, y, r, km, v, g, sls)):
        y_ref[q, :, slice(p * pw, (p + 1) * pw)] = out

    @pl.when(t == pl.num_programs(2) - 1)
    def _():
        for n, (q, p) in enumerate(chains):
            s_ref[q, 2 * p] = st_ref[n, lo, lo]
            s_ref[q, 2 * p + 1] = st_ref[n, hi, hi]


def _wkv(batch, seq, chunk, heads, seqs, r, km, kk, v, lw, a, g, r_k, lnw, lnb, s0):
    n_t = seq // chunk
    width = heads * HEAD_SIZE
    tok = pl.BlockSpec((seqs, chunk, width), lambda b, h, t: (b, t, h))
    vec = pl.BlockSpec((1, width), lambda b, h, t: (0, h))
    st = pl.BlockSpec((seqs, heads, HEAD_SIZE, HEAD_SIZE), lambda b, h, t: (b, h, 0, 0))
    as3d = lambda x: x.reshape(batch, seq, D_MODEL)
    y, s_new = pl.pallas_call(
        functools.partial(_wkv_kernel, chunk=chunk, heads=heads, seqs=seqs),
        grid=(batch // seqs, N_HEADS // heads, n_t),
        in_specs=[tok] * 7 + [vec] * 3 + [st],
        out_specs=[tok, st],
        out_shape=[jax.ShapeDtypeStruct((batch, seq, D_MODEL), F32),
                   jax.ShapeDtypeStruct((batch, N_HEADS, HEAD_SIZE, HEAD_SIZE), F32)],
        scratch_shapes=[pltpu.VMEM((seqs * heads // 2, 2 * HEAD_SIZE, 2 * HEAD_SIZE), F32)],
        compiler_params=_params("arbitrary", "arbitrary", "arbitrary"),
        name="wkv",
    )(*(as3d(x) for x in (r, km, kk, v, lw, a, g)),
      r_k.reshape(1, D_MODEL), lnw.reshape(1, D_MODEL), lnb.reshape(1, D_MODEL), s0)
    return y.reshape(batch * seq, D_MODEL), s_new


SUBLANES = 8


def _col_reduce(x3, op):
    m = x3[0]
    for r in range(1, x3.shape[0]):
        m = op(m, x3[r])
    for shift in (4, 2, 1):
        m = op(m, pltpu.roll(m, shift, axis=0))
    return m


def _top16(s3, want_rank):
    work = s3
    rank = jnp.full(s3.shape, float(PEER_TOPK), F32) if want_rank else None
    vals = []
    for k in range(PEER_TOPK):
        m = _col_reduce(work, jnp.maximum)
        hit = work == m[None]
        if want_rank:
            rank = jnp.where(hit, float(k), rank)
        work = jnp.where(hit, -jnp.inf, work)
        vals.append(m)
    return vals, work, rank


def _rows_from(vals):
    row = lax.broadcasted_iota(jnp.int32, vals[0].shape, 0)
    out = vals[0]
    for s in range(1, SUBLANES):
        out = jnp.where(row == s, vals[s], out)
    return out


def _peer_select_chunk(s1, s2):
    n_keys, w = s1.shape
    s1 = s1.reshape(n_keys // SUBLANES, SUBLANES, w)
    s2 = s2.reshape(n_keys // SUBLANES, SUBLANES, w)
    v1, left1, _ = _top16(s1, False)
    v2, _, rank2 = _top16(s2, True)
    v2_lo, v2_hi, v1_hi = _rows_from(v2[:SUBLANES]), _rows_from(v2[SUBLANES:]), _rows_from(v1[SUBLANES:])
    cand = jnp.stack([v1[0] + v2_lo, v1[0] + v2_hi] + [v1[a] + v2_lo for a in range(1, SUBLANES)]
                     + [v1_hi + v2[0]])
    work = cand
    tau = None
    for _ in range(PEER_TOPK):
        tau = _col_reduce(work, jnp.maximum)
        work = jnp.where(work == tau[None], -jnp.inf, work)
    top = v1[0] + v2[0]
    kept = cand >= tau[None]
    zsum = _col_reduce(jnp.where(kept, jnp.exp(cand - top[None]), 0.0), jnp.add)
    few = PEER_TOPK // 4
    cnt = jnp.zeros(s1.shape, F32)
    for b in range(few):
        cnt = cnt + jnp.where(s1 + v2[b][None] >= tau[None], 1.0, 0.0)
    upper = lax.broadcasted_iota(jnp.int32, cand.shape[1:], 0) >= few
    ones = lambda m: jnp.where(m, 1.0, 0.0)
    extra = [_col_reduce((ones(kept[0] & upper) + ones(kept[1]))[None], jnp.add),
             _col_reduce(ones(kept[2] & upper)[None], jnp.add),
             _col_reduce(ones(kept[3] & upper)[None], jnp.add)]
    for a, x in enumerate(extra):
        cnt = jnp.where(s1 == v1[a][None], cnt + x[None], cnt)
    cnt = jnp.where(left1 == -jnp.inf, cnt, 0.0)
    e1 = jnp.exp(s1 - v1[0][None])
    e2 = jnp.exp(s2 - v2[0][None]) / zsum[None]
    return tuple(t.reshape(n_keys, w) for t in (cnt, e1, rank2, e2))


def _peer_select_kernel(x_ref, g_ref, sc_ref, sh_ref, wq_ref, k1_ref, k2_ref,
                        ht_ref, c_ref, e1_ref, rk_ref, e2_ref, q_scr):
    h = _rms_mod(x_ref[...], g_ref[...], sc_ref[0], sh_ref[0])
    ht = h.T.astype(BF16)
    ht_ref[...] = ht
    q_scr[...] = _fdot(wq_ref[...], ht)
    half = D_QUERY // 2
    n_chunks = x_ref.shape[0] // PEER_LANE_CHUNK

    def head_body(hd, carry):
        q1 = q_scr[pl.ds(pl.multiple_of(hd * D_QUERY, D_QUERY), half), :]
        q2 = q_scr[pl.ds(pl.multiple_of(hd * D_QUERY + half, half), half), :]
        s1 = _bdot(k1_ref[hd], q1)
        s2 = _bdot(k2_ref[hd], q2)
        for ch in range(n_chunks):
            ls = slice(ch * PEER_LANE_CHUNK, (ch + 1) * PEER_LANE_CHUNK)
            cnt, e1, rank2, e2 = _peer_select_chunk(s1[:, ls], s2[:, ls])
            c_ref[hd, :, ls] = cnt
            e1_ref[hd, :, ls] = e1
            rk_ref[hd, :, ls] = rank2.astype(BF16)
            e2_ref[hd, :, ls] = e2.astype(BF16)
        return carry

    lax.fori_loop(0, PEER_HEADS, head_body, 0)


def _peer_select(lay, x, g, sc, sh, wq_t, k1, k2):
    tt = lay.tile
    sel_spec = pl.BlockSpec((PEER_HEADS, N_KEYS, tt), lambda i: (0, 0, i))
    sel_shape = lambda dt: jax.ShapeDtypeStruct((PEER_HEADS, N_KEYS, lay.n), dt)
    return pl.pallas_call(
        _peer_select_kernel,
        grid=(lay.n_tiles,),
        in_specs=[lay.tok_spec(), _full_spec((1, D_MODEL)), lay.mod_spec(), lay.mod_spec(),
                  _full_spec((PEER_HEADS * D_QUERY, D_MODEL)),
                  _full_spec((PEER_HEADS, N_KEYS, D_QUERY // 2)), _full_spec((PEER_HEADS, N_KEYS, D_QUERY // 2))],
        out_specs=[pl.BlockSpec((D_MODEL, tt), lambda i: (0, i))] + [sel_spec] * 4,
        out_shape=[jax.ShapeDtypeStruct((D_MODEL, lay.n), BF16),
                   sel_shape(F32), sel_shape(F32), sel_shape(BF16), sel_shape(BF16)],
        scratch_shapes=[pltpu.VMEM((PEER_HEADS * D_QUERY, tt), F32)],
        compiler_params=_params("arbitrary"),
        name="peer_select",
    )(x, g.reshape(1, D_MODEL), sc, sh, wq_t, k1, k2)


def _peer_dense_kernel(*refs):
    groups = PEER_EXPERT_TILE // PEER_GROUP
    ht_ref, c_ref, e1_ref, rk_ref, e2_ref = refs[:5]
    u_refs = refs[5:5 + groups]
    vt_refs = refs[5 + groups:5 + 2 * groups]
    x_ref, gt_ref, o_ref, acc_ref, coef_ref, z_ref = refs[5 + 2 * groups:]
    e = pl.program_id(1)
    sr = PEER_STEP_ROWS
    spg = PEER_GROUP // (N_KEYS * sr)
    piece = D_MODEL // spg
    last_slot = (groups - 1) % 2

    @pl.when(e == 0)
    def _():
        acc_ref[...] = jnp.zeros_like(acc_ref)

    def pre_act(step):
        g, ii = divmod(step, spg)
        return _fdot(u_refs[g][0, ii * sr * N_KEYS:(ii + 1) * sr * N_KEYS, :], ht_ref[...])

    z_ref[0] = pre_act(0)
    for g in range(groups):
        slot = g % 2
        for ii in range(spg):
            step = g * spg + ii
            if step + 1 < groups * spg:
                z_ref[(step + 1) % 2] = pre_act(step + 1)
            if g > 0:
                ps = slice(ii * piece, (ii + 1) * piece)
                acc_ref[ps, :] += _fdot(vt_refs[g - 1][0, ps, :], coef_ref[1 - slot])
            for rr in range(sr):
                row = step * sr + rr
                gate = None
                for hd in range(PEER_HEADS):
                    cnt = c_ref[hd, row:row + 1, :].astype(BF16)
                    e1 = e1_ref[hd, row:row + 1, :].astype(BF16)
                    term = jnp.where(rk_ref[hd] < cnt, e2_ref[hd], jnp.zeros((), BF16)) * e1
                    gate = term if gate is None else gate + term
                z = z_ref[step % 2, rr * N_KEYS:(rr + 1) * N_KEYS, :]
                act = 0.5 * z * (1.0 + lax.erf(z * (1.0 / math.sqrt(2.0))))
                lo = (ii * sr + rr) * N_KEYS
                coef_ref[slot, lo:lo + N_KEYS, :] = gate * act.astype(BF16)

    acc_ref[...] += _fdot(vt_refs[groups - 1][0], coef_ref[last_slot])

    @pl.when(e == pl.num_programs(1) - 1)
    def _():
        o_ref[...] = x_ref[...] + gt_ref[0] * acc_ref[...].T


def _peer_weight_parts(peer_u, peer_v):
    groups = PEER_EXPERT_TILE // PEER_GROUP
    tiles = N_EXPERTS // PEER_EXPERT_TILE
    u5 = peer_u.astype(BF16).reshape(DEPTH, tiles, groups, PEER_GROUP, D_MODEL)
    v5 = jnp.swapaxes(peer_v.astype(BF16).reshape(DEPTH, tiles, groups, PEER_GROUP, D_MODEL), 3, 4)
    u_parts = [u5[:, :, k] for k in range(groups)]
    vt_parts = [v5[:, :, k] for k in range(groups)]
    return u_parts, vt_parts


def _peer_dense(lay, ht, cnt, e1, rank2, e2, u_parts, vt_parts, x, gt):
    tt = lay.tile
    groups = PEER_EXPERT_TILE // PEER_GROUP
    rows_per_tile = PEER_EXPERT_TILE // N_KEYS
    tpg = lay.tiles_per_group
    row_spec = pl.BlockSpec((PEER_HEADS, rows_per_tile, tt), lambda i, e: (0, e, i))
    key_spec = pl.BlockSpec((PEER_HEADS, N_KEYS, tt), lambda i, e: (0, 0, i))
    u_spec = pl.BlockSpec((1, PEER_GROUP, D_MODEL), lambda i, e: (e, 0, 0))
    vt_spec = pl.BlockSpec((1, D_MODEL, PEER_GROUP), lambda i, e: (e, 0, 0))
    return pl.pallas_call(
        _peer_dense_kernel,
        grid=(lay.n_tiles, N_EXPERTS // PEER_EXPERT_TILE),
        in_specs=[pl.BlockSpec((D_MODEL, tt), lambda i, e: (0, i)), row_spec, row_spec, key_spec, key_spec]
        + [u_spec] * groups + [vt_spec] * groups + [
            pl.BlockSpec((tt, D_MODEL), lambda i, e: (i, 0)),
            pl.BlockSpec((1, lay.rows, D_MODEL), lambda i, e: (i // tpg, 0, 0))],
        out_specs=pl.BlockSpec((tt, D_MODEL), lambda i, e: (i, 0)),
        out_shape=jax.ShapeDtypeStruct((lay.n, D_MODEL), F32),
        scratch_shapes=[pltpu.VMEM((D_MODEL, tt), F32), pltpu.VMEM((2, PEER_GROUP, tt), BF16),
                        pltpu.VMEM((2, PEER_STEP_ROWS * N_KEYS, tt), F32)],
        compiler_params=_params("arbitrary", "arbitrary"),
        name="peer_dense",
    )(ht, cnt, e1, rank2, e2, *u_parts, *vt_parts, x, gt)


def _run(batch, seq, x, m_all, st_conv, st_shift, st_wkv, p):
    lay = _Layout(batch, seq)
    x = x.reshape(lay.n, D_MODEL)
    new_conv, new_shift, new_wkv = [], [], []
    v_first = None
    short = seq < WKV_CHUNK
    for i in range(DEPTH):
        sh_m, sc_m, gt_m, sh_f, sc_f, gt_f = (lay.expand(t) for t in jnp.split(m_all[i], 6, axis=-1))
        j = i // 2
        if i % 2 == 0 and lay.rows == 1:
            x, cbuf = _conv_fused(lay, x, p["g_mix"][i], sc_m, sh_m, p["conv_w_in"][j], p["conv_k"][j],
                                  p["conv_w_out"][j], gt_m, st_conv[j])
            new_conv.append(cbuf)
        elif i % 2 == 0:
            bg, u = _conv_in(lay, x, p["g_mix"][i], sc_m, sh_m, p["conv_w_in"][j])
            up = jnp.concatenate([st_conv[j], u.reshape(batch, seq, D_MODEL)], axis=1)
            u0 = up[:, 0:seq].reshape(lay.n, D_MODEL)
            u1 = up[:, 1:seq + 1].reshape(lay.n, D_MODEL)
            new_conv.append(up[:, -(CONV_WIDTH - 1):])
            x = _conv_out(lay, bg, u0, u1, u, p["conv_k"][j], p["conv_w_out"][j], x, gt_m)
        else:
            h = _norm_mod(lay, x, p["g_mix"][i], sc_m, sh_m)
            h3 = h.reshape(batch, seq, D_MODEL)
            hs = jnp.concatenate([st_shift[j][:, None, :], h3[:, :-1]], axis=1).reshape(lay.n, D_MODEL)
            new_shift.append(h3[:, -1])
            zero = jnp.zeros((D_MODEL,), F32)
            v0 = p["rwkv_v0"][j - 1] if j > 0 else zero
            vecs = jnp.stack([p["rwkv_w0"][j], p["rwkv_a0"][j], v0, p["rwkv_k_k"][j], p["rwkv_k_a"][j],
                              zero, zero, zero])
            v1 = p["rwkv_v1"][j - 1] if j > 0 else None
            v2 = p["rwkv_v2"][j - 1] if j > 0 else None
            r, km, kk, v, lw, a, g = _rwkv_proj(
                _Layout(batch, seq, RWKV_PROJ_TILE), h, hs, v_first if j > 0 else None, p["rwkv_mix"][j], vecs,
                p["rwkv_wr"][j], p["rwkv_wk"][j], p["rwkv_wv"][j], p["rwkv_w1"][j], p["rwkv_w2"][j],
                p["rwkv_a1"][j], p["rwkv_a2"][j], p["rwkv_g1"][j], p["rwkv_g2"][j], v1, v2)
            if j == 0:
                v_first = v
            scan_in = (r, km, kk, v, lw, a, g)
            if short:
                pad = lambda t: jnp.pad(t.reshape(batch, seq, D_MODEL),
                                        ((0, 0), (0, WKV_SHORT_CHUNK - seq), (0, 0))).reshape(-1, D_MODEL)
                scan_in = tuple(pad(t) for t in scan_in)
                yg, s_new = _wkv(batch, WKV_SHORT_CHUNK, WKV_SHORT_CHUNK, N_HEADS, WKV_SHORT_SEQS_PER_STEP, *scan_in,
                                 p["rwkv_r_k"][j], p["rwkv_lnw"][j], p["rwkv_lnb"][j], st_wkv[j])
                yg = yg.reshape(batch, WKV_SHORT_CHUNK, D_MODEL)[:, :seq].reshape(lay.n, D_MODEL)
            else:
                yg, s_new = _wkv(batch, seq, WKV_CHUNK, N_HEADS, WKV_SEQS_PER_STEP, *scan_in,
                                 p["rwkv_r_k"][j], p["rwkv_lnw"][j], p["rwkv_lnb"][j], st_wkv[j])
            new_wkv.append(s_new)
            x = _mm_res(lay, yg, p["rwkv_wo"][j], x, gt_m)
        ht, cnt, e1, rank2, e2 = _peer_select(lay, x, p["g_ffn"][i], sc_f, sh_f,
                                              p["wq_t"][i], p["k1_b"][i], p["k2_b"][i])
        x = _peer_dense(lay, ht, cnt, e1, rank2, e2, [t[i] for t in p["u_parts"]],
                        [t[i] for t in p["vt_parts"]], x, gt_f)
    y = _final_norm(lay, x, p["g_final"]).reshape(batch, seq, D_MODEL)
    return y, jnp.stack(new_conv), jnp.stack(new_shift), jnp.stack(new_wkv)


def kernel(x_prompt, x_sample, state_conv, state_shift, state_wkv, c_prompt, c_sample, w_ada, b_ada, g_mix, g_ffn, g_final, conv_w_in, conv_k, conv_w_out, rwkv_mix, rwkv_wr, rwkv_wk, rwkv_wv, rwkv_wo, rwkv_w0, rwkv_w1, rwkv_w2, rwkv_a0, rwkv_a1, rwkv_a2, rwkv_v0, rwkv_v1, rwkv_v2, rwkv_g1, rwkv_g2, rwkv_k_k, rwkv_k_a, rwkv_r_k, rwkv_lnw, rwkv_lnb, peer_wq, peer_k1, peer_k2, peer_u, peer_v):
    p = dict(
        g_mix=g_mix, g_ffn=g_ffn, g_final=g_final, conv_w_in=conv_w_in, conv_k=conv_k, conv_w_out=conv_w_out,
        rwkv_mix=rwkv_mix, rwkv_wr=rwkv_wr, rwkv_wk=rwkv_wk, rwkv_wv=rwkv_wv, rwkv_wo=rwkv_wo,
        rwkv_w0=rwkv_w0, rwkv_w1=rwkv_w1, rwkv_w2=rwkv_w2, rwkv_a0=rwkv_a0, rwkv_a1=rwkv_a1, rwkv_a2=rwkv_a2,
        rwkv_v0=rwkv_v0, rwkv_v1=rwkv_v1, rwkv_v2=rwkv_v2, rwkv_g1=rwkv_g1, rwkv_g2=rwkv_g2,
        rwkv_k_k=rwkv_k_k, rwkv_k_a=rwkv_k_a, rwkv_r_k=rwkv_r_k, rwkv_lnw=rwkv_lnw, rwkv_lnb=rwkv_lnb,
        wq_t=jnp.swapaxes(peer_wq, 1, 2).astype(BF16),
        k1_b=peer_k1.astype(BF16), k2_b=peer_k2.astype(BF16),
        **dict(zip(("u_parts", "vt_parts"), _peer_weight_parts(peer_u, peer_v))),
    )
    n_prompt, seq_prompt = x_prompt.shape[0], x_prompt.shape[1]
    n_sample, seq_sample = x_sample.shape[0], x_sample.shape[1]
    m_all = _adaln(jnp.concatenate([c_prompt, c_sample], axis=0), w_ada, b_ada)
    zeros = lambda *s: jnp.zeros(s, F32)
    n_conv, n_rwkv = (DEPTH + 1) // 2, DEPTH // 2
    y_p, p_conv, p_shift, p_wkv = _run(
        n_prompt, seq_prompt, x_prompt, m_all[:, :n_prompt],
        zeros(n_conv, n_prompt, CONV_WIDTH - 1, D_MODEL), zeros(n_rwkv, n_prompt, D_MODEL),
        zeros(n_rwkv, n_prompt, N_HEADS, HEAD_SIZE, HEAD_SIZE), p)
    y_s, s_conv, s_shift, s_wkv = _run(
        n_sample, seq_sample, x_sample, m_all[:, n_prompt:], state_conv, state_shift, state_wkv, p)
    return (y_p, y_s, p_conv, p_shift, p_wkv, s_conv, s_shift, s_wkv)
```

```python
import functools
import math

import jax
import jax.numpy as jnp
from jax import lax
from jax.experimental import pallas as pl
from jax.experimental.pallas import tpu as pltpu

D_MODEL = 1024
DEPTH = 4
CONV_WIDTH = 3
HEAD_SIZE = 64
N_HEADS = D_MODEL // HEAD_SIZE
N_KEYS = 128
N_EXPERTS = N_KEYS * N_KEYS
PEER_HEADS = 8
PEER_TOPK = 16
D_QUERY = 256
RMS_EPS = 1e-6
GN_EPS = 64e-5

SUBLANES = 8
LORA_PAD = 128
TOKEN_TILE = 512
RWKV_PROJ_TILE = 256
PEER_EXPERT_TILE = 2048
PEER_GROUP = 512
PEER_STEP_ROWS = 4
PEER_LANE_CHUNK = 256
WKV_CHUNK = 64
WKV_SHORT_CHUNK = 8
WKV_SEQS_PER_STEP = 2
WKV_SHORT_SEQS_PER_STEP = 4
VMEM_LIMIT = 56 * 1024 * 1024

F32 = jnp.float32
BF16 = jnp.bfloat16


def _params(*sem):
    return pltpu.CompilerParams(dimension_semantics=sem, vmem_limit_bytes=VMEM_LIMIT)


def _sigmoid(x):
    return 1.0 / (1.0 + jnp.exp(-x))


def _rms_mod(x, g, sc, sh):
    y = x * lax.rsqrt(jnp.mean(x * x, axis=-1, keepdims=True) + RMS_EPS)
    return (y * g) * (1.0 + sc) + sh


def _bdot(a, b):
    return jnp.dot(a.astype(BF16), b.astype(BF16), preferred_element_type=F32)


def _fdot(a, b):
    return jnp.dot(a, b, preferred_element_type=F32)


def _fdot_nt(a, b):
    return lax.dot_general(a, b, (((1,), (1,)), ((), ())), preferred_element_type=F32)


def _fdot_tn(a, b):
    return lax.dot_general(a, b, (((0,), (0,)), ((), ())), preferred_element_type=F32)


class _Layout:
    def __init__(self, batch, seq, tile=TOKEN_TILE):
        self.batch, self.seq = batch, seq
        self.n = batch * seq
        self.tile = min(tile, self.n)
        assert self.n % self.tile == 0
        self.n_tiles = self.n // self.tile
        if seq % self.tile == 0:
            self.rows = 1
            self.tiles_per_group = seq // self.tile
        else:
            assert self.tile % seq == 0
            self.rows = self.tile
            self.tiles_per_group = 1

    def expand(self, m):
        if self.rows == 1:
            return m[:, None, :]
        return jnp.repeat(m, self.seq, axis=0).reshape(self.n_tiles, self.tile, m.shape[-1])

    def mod_spec(self):
        tpg = self.tiles_per_group
        return pl.BlockSpec((1, self.rows, D_MODEL), lambda i: (i // tpg, 0, 0))

    def tok_spec(self, width=D_MODEL):
        return pl.BlockSpec((self.tile, width), lambda i: (i, 0))


def _full_spec(shape):
    nd = len(shape)
    return pl.BlockSpec(shape, lambda *_: (0,) * nd)


def _adaln_kernel(c_ref, w_ref, b_ref, o_ref):
    c = c_ref[...]
    cs = c * _sigmoid(c)
    o_ref[0] = _fdot(cs, w_ref[0]) + b_ref[0]


def _adaln(c, w_ada, b_ada):
    nb = c.shape[0]
    tn = 1536
    return pl.pallas_call(
        _adaln_kernel,
        grid=(DEPTH, 6 * D_MODEL // tn),
        in_specs=[
            pl.BlockSpec((nb, D_MODEL), lambda l, j: (0, 0)),
            pl.BlockSpec((1, D_MODEL, tn), lambda l, j: (l, 0, j)),
            pl.BlockSpec((1, 1, tn), lambda l, j: (l, 0, j)),
        ],
        out_specs=pl.BlockSpec((1, nb, tn), lambda l, j: (l, 0, j)),
        out_shape=jax.ShapeDtypeStruct((DEPTH, nb, 6 * D_MODEL), F32),
        compiler_params=_params("arbitrary", "arbitrary"),
        name="adaln",
    )(c, w_ada, b_ada.reshape(DEPTH, 1, 6 * D_MODEL))


def _norm_mod_kernel(x_ref, g_ref, sc_ref, sh_ref, o_ref):
    o_ref[...] = _rms_mod(x_ref[...], g_ref[...], sc_ref[0], sh_ref[0])


def _norm_mod(lay, x, g, sc, sh):
    return pl.pallas_call(
        _norm_mod_kernel,
        grid=(lay.n_tiles,),
        in_specs=[lay.tok_spec(), _full_spec((1, D_MODEL)), lay.mod_spec(), lay.mod_spec()],
        out_specs=lay.tok_spec(),
        out_shape=jax.ShapeDtypeStruct((lay.n, D_MODEL), F32),
        compiler_params=_params("arbitrary"),
        name="norm_mod",
    )(x, g.reshape(1, D_MODEL), sc, sh)


def _final_norm_kernel(x_ref, g_ref, o_ref):
    x = x_ref[...]
    o_ref[...] = x * lax.rsqrt(jnp.mean(x * x, axis=-1, keepdims=True) + RMS_EPS) * g_ref[...]


def _final_norm(lay, x, g):
    return pl.pallas_call(
        _final_norm_kernel,
        grid=(lay.n_tiles,),
        in_specs=[lay.tok_spec(), _full_spec((1, D_MODEL))],
        out_specs=lay.tok_spec(),
        out_shape=jax.ShapeDtypeStruct((lay.n, D_MODEL), F32),
        compiler_params=_params("arbitrary"),
        name="final_norm",
    )(x, g.reshape(1, D_MODEL))


def _conv_in_kernel(x_ref, g_ref, sc_ref, sh_ref, w_ref, bg_ref, u_ref):
    h = _rms_mod(x_ref[...], g_ref[...], sc_ref[0], sh_ref[0]).astype(BF16)
    d = D_MODEL
    bg_ref[...] = _fdot(h, w_ref[:, 0:d])
    u_ref[...] = _fdot(h, w_ref[:, d:2 * d]) * _fdot(h, w_ref[:, 2 * d:3 * d])


def _conv_in(lay, x, g, sc, sh, w_in):
    shp = jax.ShapeDtypeStruct((lay.n, D_MODEL), F32)
    return pl.pallas_call(
        _conv_in_kernel,
        grid=(lay.n_tiles,),
        in_specs=[lay.tok_spec(), _full_spec((1, D_MODEL)), lay.mod_spec(), lay.mod_spec(),
                  _full_spec((D_MODEL, 3 * D_MODEL))],
        out_specs=[lay.tok_spec(), lay.tok_spec()],
        out_shape=[shp, shp],
        compiler_params=_params("arbitrary"),
        name="conv_in",
    )(x, g.reshape(1, D_MODEL), sc, sh, w_in.astype(BF16))


def _conv_out_kernel(bg_ref, u0_ref, u1_ref, u2_ref, ck_ref, w_ref, x_ref, gt_ref, o_ref):
    conv = u0_ref[...] * ck_ref[0:1, :] + u1_ref[...] * ck_ref[1:2, :] + u2_ref[...] * ck_ref[2:3, :]
    y = _bdot(bg_ref[...] * conv, w_ref[...])
    o_ref[...] = x_ref[...] + gt_ref[0] * y


def _conv_out(lay, bg, u0, u1, u2, conv_k, w_out, x, gt):
    return pl.pallas_call(
        _conv_out_kernel,
        grid=(lay.n_tiles,),
        in_specs=[lay.tok_spec()] * 4 + [_full_spec((CONV_WIDTH, D_MODEL)), _full_spec((D_MODEL, D_MODEL)),
                                         lay.tok_spec(), lay.mod_spec()],
        out_specs=lay.tok_spec(),
        out_shape=jax.ShapeDtypeStruct((lay.n, D_MODEL), F32),
        compiler_params=_params("arbitrary"),
        name="conv_out",
    )(bg, u0, u1, u2, conv_k, w_out.astype(BF16), x, gt)


def _conv_fused_kernel(x_ref, g_ref, sc_ref, sh_ref, win_ref, ck_ref, wout_ref, gt_ref, st_ref,
                       o_ref, ns_ref, carry_ref, *, tiles_per_seq):
    x = x_ref[...]
    h = _rms_mod(x, g_ref[...], sc_ref[0], sh_ref[0]).astype(BF16)
    d = D_MODEL
    rows = x.shape[0]
    u = _fdot(h, win_ref[:, d:2 * d]) * _fdot(h, win_ref[:, 2 * d:3 * d])
    first = pl.program_id(0) % tiles_per_seq == 0
    prev = jnp.where(first, st_ref[0], carry_ref[...])
    tail = u[rows - (CONV_WIDTH - 1):, :]
    carry_ref[...] = tail
    ns_ref[0] = tail
    row = lax.broadcasted_iota(jnp.int32, u.shape, 0)
    um1 = jnp.where(row == 0, prev[1:2, :], pltpu.roll(u, 1, axis=0))
    um2 = jnp.where(row == 0, prev[0:1, :], jnp.where(row == 1, prev[1:2, :], pltpu.roll(u, 2, axis=0)))
    conv = um2 * ck_ref[0:1, :] + um1 * ck_ref[1:2, :] + u * ck_ref[2:3, :]
    bg = _fdot(h, win_ref[:, 0:d])
    o_ref[...] = x + gt_ref[0] * _bdot(bg * conv, wout_ref[...])


def _conv_fused(lay, x, g, sc, sh, w_in, conv_k, w_out, gt, st):
    assert lay.rows == 1 and CONV_WIDTH == 3
    tps = lay.tiles_per_group
    st_spec = pl.BlockSpec((1, CONV_WIDTH - 1, D_MODEL), lambda i: (i // tps, 0, 0))
    return pl.pallas_call(
        functools.partial(_conv_fused_kernel, tiles_per_seq=tps),
        grid=(lay.n_tiles,),
        in_specs=[lay.tok_spec(), _full_spec((1, D_MODEL)), lay.mod_spec(), lay.mod_spec(),
                  _full_spec((D_MODEL, 3 * D_MODEL)), _full_spec((CONV_WIDTH, D_MODEL)),
                  _full_spec((D_MODEL, D_MODEL)), lay.mod_spec(), st_spec],
        out_specs=[lay.tok_spec(), st_spec],
        out_shape=[jax.ShapeDtypeStruct((lay.n, D_MODEL), F32),
                   jax.ShapeDtypeStruct((lay.batch, CONV_WIDTH - 1, D_MODEL), F32)],
        scratch_shapes=[pltpu.VMEM((CONV_WIDTH - 1, D_MODEL), F32)],
        compiler_params=_params("arbitrary"),
        name="conv_fused",
    )(x, g.reshape(1, D_MODEL), sc, sh, w_in.astype(BF16), conv_k, w_out.astype(BF16), gt, st)


def _mm_res_kernel(a_ref, w_ref, x_ref, gt_ref, o_ref):
    o_ref[...] = x_ref[...] + gt_ref[0] * _bdot(a_ref[...], w_ref[...])


def _mm_res(lay, a, w, x, gt):
    return pl.pallas_call(
        _mm_res_kernel,
        grid=(lay.n_tiles,),
        in_specs=[lay.tok_spec(), _full_spec((D_MODEL, D_MODEL)), lay.tok_spec(), lay.mod_spec()],
        out_specs=lay.tok_spec(),
        out_shape=jax.ShapeDtypeStruct((lay.n, D_MODEL), F32),
        compiler_params=_params("arbitrary"),
        name="mm_res",
    )(a, w.astype(BF16), x, gt)


def _rwkv_proj_kernel(*refs, has_vres, tiles_per_seq):
    refs = list(refs)
    fused = tiles_per_seq is not None
    if fused:
        x_ref, gm_ref, sc_ref, sh_ref, st_ref = refs[:5]
        del refs[:5]
        carry_ref = refs.pop()
        ns_ref = refs.pop()
    else:
        h_ref, hs_ref = refs[:2]
        del refs[:2]
    vf_ref = refs.pop(0) if has_vres else None
    mix_ref, vec_ref, wr_ref, wk_ref, wv_ref, w1_ref, w2_ref, a1_ref, a2_ref, g1_ref, g2_ref = refs[:11]
    del refs[:11]
    if has_vres:
        v1_ref, v2_ref = refs[:2]
        del refs[:2]
    r_ref, km_ref, kk_ref, v_ref, lw_ref, a_ref, g_ref = refs
    if fused:
        h = _rms_mod(x_ref[...], gm_ref[...], sc_ref[0], sh_ref[0])
        rows = h.shape[0]
        first = pl.program_id(0) % tiles_per_seq == 0
        prev = jnp.where(first, st_ref[0], carry_ref[0:1, :])
        last = h[rows - 1:rows, :]
        carry_ref[0:1, :] = last
        ns_ref[0] = last
        row = lax.broadcasted_iota(jnp.int32, h.shape, 0)
        hs = jnp.where(row == 0, prev, pltpu.roll(h, 1, axis=0))
    else:
        h = h_ref[...]
        hs = hs_ref[...]
    xx = hs - h
    mixed = lambda i: (h + xx * mix_ref[i:i + 1, :]).astype(BF16)
    w0, a0, v0, k_k, k_a = (vec_ref[i:i + 1, :] for i in range(5))

    r_ref[...] = _fdot(mixed(0), wr_ref[...])
    wl = w0 + _bdot(jnp.tanh(_fdot(mixed(1), w1_ref[...])), w2_ref[...])
    z = -wl
    wlog = -(jnp.maximum(z, 0.0) + jnp.log(1.0 + jnp.exp(-jnp.abs(z)))) - 0.5
    lw_ref[...] = -jnp.exp(wlog)
    k = _fdot(mixed(2), wk_ref[...])
    xv = mixed(3)
    v = _fdot(xv, wv_ref[...])
    if has_vres:
        v = v + (vf_ref[...] - v) * _sigmoid(v0 + _bdot(_fdot(xv, v1_ref[...]), v2_ref[...]))
    v_ref[...] = v
    a = _sigmoid(a0 + _bdot(_fdot(mixed(4), a1_ref[...]), a2_ref[...]))
    a_ref[...] = a
    g_ref[...] = _bdot(_sigmoid(_fdot(mixed(5), g1_ref[...])), g2_ref[...])
    kk_ref[...] = k * k_k
    km_ref[...] = k * (1.0 + (a - 1.0) * k_a)


def _pad_cols(w):
    return jnp.pad(w, ((0, 0), (0, LORA_PAD - w.shape[1]))).astype(BF16)


def _pad_rows(w):
    return jnp.pad(w, ((0, LORA_PAD - w.shape[0]), (0, 0))).astype(BF16)


def _rwkv_proj(lay, front, v_first, mix, vecs, wr, wk, wv, w1, w2, a1, a2, g1, g2, v1, v2):
    has_vres = v_first is not None
    fused = len(front) == 5
    sq = _full_spec((D_MODEL, D_MODEL))
    down = _full_spec((D_MODEL, LORA_PAD))
    up = _full_spec((LORA_PAD, D_MODEL))
    tps = lay.tiles_per_group
    row_spec = pl.BlockSpec((1, 1, D_MODEL), lambda i: (i // tps, 0, 0))
    if fused:
        assert lay.rows == 1
        x, g, sc, sh, st = front
        in_specs = [lay.tok_spec(), _full_spec((1, D_MODEL)), lay.mod_spec(), lay.mod_spec(), row_spec]
        args = [x, g.reshape(1, D_MODEL), sc, sh, st[:, None, :]]
    else:
        in_specs = [lay.tok_spec(), lay.tok_spec()]
        args = list(front)
    in_specs += [lay.tok_spec()] if has_vres else []
    in_specs += [_full_spec((6, D_MODEL)), _full_spec((8, D_MODEL)), sq, sq, sq, down, up, down, up, down, up]
    args += [v_first] if has_vres else []
    args += [mix, vecs, wr.astype(BF16), wk.astype(BF16), wv.astype(BF16), _pad_cols(w1), _pad_rows(w2),
             _pad_cols(a1), _pad_rows(a2), _pad_cols(g1), _pad_rows(g2)]
    if has_vres:
        in_specs += [down, up]
        args += [_pad_cols(v1), _pad_rows(v2)]
    shp = jax.ShapeDtypeStruct((lay.n, D_MODEL), F32)
    outs = pl.pallas_call(
        functools.partial(_rwkv_proj_kernel, has_vres=has_vres, tiles_per_seq=tps if fused else None),
        grid=(lay.n_tiles,),
        in_specs=in_specs,
        out_specs=[lay.tok_spec()] * 7 + ([row_spec] if fused else []),
        out_shape=[shp] * 7 + ([jax.ShapeDtypeStruct((lay.batch, 1, D_MODEL), F32)] if fused else []),
        scratch_shapes=[pltpu.VMEM((SUBLANES, D_MODEL), F32)] if fused else [],
        compiler_params=_params("arbitrary"),
        name="rwkv_proj",
    )(*args)
    return outs if not fused else (*outs[:7], outs[7][:, 0, :])


def _wkv_kernel(r_ref, km_ref, kk_ref, v_ref, lw_ref, a_ref, g_ref, rk_ref, lnw_ref, lnb_ref, s0_ref,
                y_ref, s_ref, st_ref, *, chunk, heads, seqs):
    t = pl.program_id(2)
    L = chunk

    pairs = heads // 2
    pw = 2 * HEAD_SIZE
    lo, hi = slice(0, HEAD_SIZE), slice(HEAD_SIZE, pw)
    chains = [(q, p) for q in range(seqs) for p in range(pairs)]

    @pl.when(t == 0)
    def _():
        st_ref[...] = jnp.zeros_like(st_ref)
        for n, (q, p) in enumerate(chains):
            st_ref[n, lo, lo] = s0_ref[q, 2 * p]
            st_ref[n, hi, hi] = s0_ref[q, 2 * p + 1]

    row = lax.broadcasted_iota(jnp.int32, (L, 2 * L), 0)
    col = lax.broadcasted_iota(jnp.int32, (L, 2 * L), 1)
    col = jnp.where(col >= L, col - L, col)
    strict2 = row > col
    incl2 = row >= col
    tri = incl2[:, :L].astype(F32)
    even = lax.broadcasted_iota(jnp.int32, (1, pw), 1) < HEAD_SIZE
    same_head = ((lax.broadcasted_iota(jnp.int32, (pw, pw), 0) < HEAD_SIZE)
                 == (lax.broadcasted_iota(jnp.int32, (pw, pw), 1) < HEAD_SIZE))
    ones_bd = same_head.astype(BF16)

    def head_sum(x):
        x_hi = x.astype(BF16)
        x_lo = (x - x_hi.astype(F32)).astype(BF16)
        return _fdot(x_hi, ones_bd) + _fdot(x_lo, ones_bd)

    pick = lambda x0, x1: jnp.where(even, x0, x1)

    each = lambda f, *cols: [f(*xs) for xs in zip(*cols)]
    sls = [slice(p * pw, (p + 1) * pw) for _, p in chains]
    r, km, kkr, v, lw, asig, g = ([ref[q, :, slice(p * pw, (p + 1) * pw)] for q, p in chains]
                                  for ref in (r_ref, km_ref, kk_ref, v_ref, lw_ref, a_ref, g_ref))
    s0 = [st_ref[n] for n in range(len(chains))]
    c = each(lambda x: _fdot(tri, x), lw)
    kk = each(lambda x: x * lax.rsqrt(jnp.maximum(head_sum(x * x), 1e-24)), kkr)
    b = each(lambda x, y: x * y, kk, asig)
    c_last = each(lambda x: x[L - 1:L, :], c)
    ginv = each(lambda x: jnp.exp(-x), c)
    lhs = each(lambda kk_, c_, lw_, r_: jnp.concatenate([-kk_ * jnp.exp(c_ - lw_), r_ * jnp.exp(c_)], axis=0),
               kk, c, lw, r)
    rhs = each(lambda b_, km_, gi: jnp.concatenate([b_ * gi, km_ * gi], axis=0), b, km, ginv)
    qk0 = each(lambda l_, r_: _fdot_nt(jnp.where(even, l_, 0.0), r_), lhs, rhs)
    qk1 = each(lambda l_, r_: _fdot_nt(jnp.where(even, 0.0, l_), r_), lhs, rhs)
    hs0 = each(_fdot_nt, lhs, s0)
    zv = each(lambda v_: jnp.concatenate([jnp.zeros_like(v_), v_], axis=0), v)
    m0 = each(lambda q: jnp.where(strict2, q[:L, :], 0.0), qk0)
    m1 = each(lambda q: jnp.where(strict2, q[:L, :], 0.0), qk1)
    u = each(lambda h_, a0, a1, z_: h_[:L] + pick(_fdot(a0, z_), _fdot(a1, z_)), hs0, m0, m1, zv)
    p0 = each(lambda m: m[:, :L], m0)
    p1 = each(lambda m: m[:, :L], m1)
    for it in range(int(math.log2(L))):
        if it > 0:
            p0 = each(lambda x: _fdot(x, x), p0)
            p1 = each(lambda x: _fdot(x, x), p1)
        u = each(lambda u_, a0, a1: u_ + pick(_fdot(a0, u_), _fdot(a1, u_)), u, p0, p1)
    uv = each(lambda u_, v_: jnp.concatenate([u_, v_], axis=0), u, v)
    bk = each(lambda cl, c_, b_, km_: jnp.concatenate([b_, km_], axis=0)
              * jnp.exp(cl - jnp.concatenate([c_, c_], axis=0)), c_last, c, b, km)
    s_new = each(lambda s_, cl, uv_, bk_: jnp.where(same_head, s_ * jnp.exp(cl) + _fdot_tn(uv_, bk_), 0.0),
                 s0, c_last, uv, bk)
    for n in range(len(chains)):
        st_ref[n] = s_new[n]
    y = each(lambda h_, q0, q1, uv_: h_[L:] + pick(_fdot(jnp.where(incl2, q0[L:, :], 0.0), uv_),
                                                   _fdot(jnp.where(incl2, q1[L:, :], 0.0), uv_)),
             hs0, qk0, qk1, uv)

    def finish(y_, r_, km_, v_, g_, sl):
        yc = y_ - head_sum(y_) * (1.0 / HEAD_SIZE)
        var = head_sum(yc * yc) * (1.0 / HEAD_SIZE)
        yn = yc * lax.rsqrt(var + GN_EPS) * lnw_ref[:, sl] + lnb_ref[:, sl]
        bonus = head_sum(r_ * km_ * rk_ref[:, sl]) * v_
        return (yn + bonus) * g_

    for (q, p), out in zip(chains, each(finish, y, r, km, v, g, sls)):
        y_ref[q, :, slice(p * pw, (p + 1) * pw)] = out

    @pl.when(t == pl.num_programs(2) - 1)
    def _():
        for n, (q, p) in enumerate(chains):
            s_ref[q, 2 * p] = st_ref[n, lo, lo]
            s_ref[q, 2 * p + 1] = st_ref[n, hi, hi]


def _wkv(batch, seq, chunk, heads, seqs, r, km, kk, v, lw, a, g, r_k, lnw, lnb, s0):
    n_t = seq // chunk
    width = heads * HEAD_SIZE
    tok = pl.BlockSpec((seqs, chunk, width), lambda b, h, t: (b, t, h))
    vec = pl.BlockSpec((1, width), lambda b, h, t: (0, h))
    st = pl.BlockSpec((seqs, heads, HEAD_SIZE, HEAD_SIZE), lambda b, h, t: (b, h, 0, 0))
    as3d = lambda x: x.reshape(batch, seq, D_MODEL)
    y, s_new = pl.pallas_call(
        functools.partial(_wkv_kernel, chunk=chunk, heads=heads, seqs=seqs),
        grid=(batch // seqs, N_HEADS // heads, n_t),
        in_specs=[tok] * 7 + [vec] * 3 + [st],
        out_specs=[tok, st],
        out_shape=[jax.ShapeDtypeStruct((batch, seq, D_MODEL), F32),
                   jax.ShapeDtypeStruct((batch, N_HEADS, HEAD_SIZE, HEAD_SIZE), F32)],
        scratch_shapes=[pltpu.VMEM((seqs * heads // 2, 2 * HEAD_SIZE, 2 * HEAD_SIZE), F32)],
        compiler_params=_params("arbitrary", "arbitrary", "arbitrary"),
        name="wkv",
    )(*(as3d(x) for x in (r, km, kk, v, lw, a, g)),
      r_k.reshape(1, D_MODEL), lnw.reshape(1, D_MODEL), lnb.reshape(1, D_MODEL), s0)
    return y.reshape(batch * seq, D_MODEL), s_new


def _col_reduce(x3, op):
    m = x3[0]
    for r in range(1, x3.shape[0]):
        m = op(m, x3[r])
    for shift in (4, 2, 1):
        m = op(m, pltpu.roll(m, shift, axis=0))
    return m


def _top16(s3, want_rank):
    work = s3
    rank = jnp.full(s3.shape, float(PEER_TOPK), F32) if want_rank else None
    vals = []
    for k in range(PEER_TOPK):
        m = _col_reduce(work, jnp.maximum)
        hit = work == m[None]
        if want_rank:
            rank = jnp.where(hit, float(k), rank)
        work = jnp.where(hit, -jnp.inf, work)
        vals.append(m)
    return vals, work, rank


def _sorting_network(n):
    pairs = []
    p = 1
    while p < n:
        k = p
        while k >= 1:
            for j in range(k % p, n - k, 2 * k):
                for i in range(min(k, n - j - k)):
                    if (i + j) // (2 * p) == (i + j + k) // (2 * p):
                        pairs.append((i + j, i + j + k))
            k //= 2
        p *= 2
    return pairs


def _top16_values(s3):
    cols = [s3[r] for r in range(s3.shape[0])]
    for i, j in _sorting_network(len(cols)):
        cols[i], cols[j] = jnp.maximum(cols[i], cols[j]), jnp.minimum(cols[i], cols[j])
    vals = []
    for k in range(PEER_TOPK):
        m = _col_reduce(cols[0][None], jnp.maximum)
        vals.append(m)
        if k + 1 < PEER_TOPK:
            hit = cols[0] == m
            for d in range(PEER_TOPK - 1 - k):
                cols[d] = jnp.where(hit, cols[d + 1], cols[d])
    return vals


def _rows_from(vals):
    row = lax.broadcasted_iota(jnp.int32, vals[0].shape, 0)
    out = vals[0]
    for s in range(1, SUBLANES):
        out = jnp.where(row == s, vals[s], out)
    return out


def _peer_select_chunk(s1, s2):
    n_keys, w = s1.shape
    s1 = s1.reshape(n_keys // SUBLANES, SUBLANES, w)
    s2 = s2.reshape(n_keys // SUBLANES, SUBLANES, w)
    v1 = _top16_values(s1)
    v2, _, rank2 = _top16(s2, True)
    v2_lo, v2_hi, v1_hi = _rows_from(v2[:SUBLANES]), _rows_from(v2[SUBLANES:]), _rows_from(v1[SUBLANES:])
    cand = jnp.stack([v1[0] + v2_lo, v1[0] + v2_hi] + [v1[a] + v2_lo for a in range(1, SUBLANES)]
                     + [v1_hi + v2[0]])
    work = cand
    tau = None
    for _ in range(PEER_TOPK):
        tau = _col_reduce(work, jnp.maximum)
        work = jnp.where(work == tau[None], -jnp.inf, work)
    top = v1[0] + v2[0]
    kept = cand >= tau[None]
    zsum = _col_reduce(jnp.where(kept, jnp.exp(cand - top[None]), 0.0), jnp.add)
    few = PEER_TOPK // 4
    cnt = jnp.zeros(s1.shape, F32)
    for b in range(few):
        cnt = cnt + jnp.where(s1 + v2[b][None] >= tau[None], 1.0, 0.0)
    upper = lax.broadcasted_iota(jnp.int32, cand.shape[1:], 0) >= few
    ones = lambda m: jnp.where(m, 1.0, 0.0)
    extra = [_col_reduce((ones(kept[0] & upper) + ones(kept[1]))[None], jnp.add),
             _col_reduce(ones(kept[2] & upper)[None], jnp.add),
             _col_reduce(ones(kept[3] & upper)[None], jnp.add)]
    for a, x in enumerate(extra):
        cnt = jnp.where(s1 == v1[a][None], cnt + x[None], cnt)
    cnt = jnp.where(s1 >= v1[PEER_TOPK - 1][None], cnt, 0.0)
    e1 = jnp.exp(s1 - v1[0][None])
    e2 = jnp.exp(s2 - v2[0][None]) / zsum[None]
    return tuple(t.reshape(n_keys, w) for t in (cnt, e1, rank2, e2))


def _peer_select_kernel(x_ref, g_ref, sc_ref, sh_ref, wq_ref, k1_ref, k2_ref,
                        ht_ref, c_ref, e1_ref, rk_ref, e2_ref, q_scr):
    h = _rms_mod(x_ref[...], g_ref[...], sc_ref[0], sh_ref[0])
    ht = h.T.astype(BF16)
    ht_ref[...] = ht
    q_scr[...] = _fdot(wq_ref[...], ht)
    half = D_QUERY // 2
    n_chunks = x_ref.shape[0] // PEER_LANE_CHUNK

    def head_body(hd, carry):
        q1 = q_scr[pl.ds(pl.multiple_of(hd * D_QUERY, D_QUERY), half), :]
        q2 = q_scr[pl.ds(pl.multiple_of(hd * D_QUERY + half, half), half), :]
        s1 = _bdot(k1_ref[hd], q1)
        s2 = _bdot(k2_ref[hd], q2)
        for ch in range(n_chunks):
            ls = slice(ch * PEER_LANE_CHUNK, (ch + 1) * PEER_LANE_CHUNK)
            cnt, e1, rank2, e2 = _peer_select_chunk(s1[:, ls], s2[:, ls])
            c_ref[hd, :, ls] = cnt
            e1_ref[hd, :, ls] = e1
            rk_ref[hd, :, ls] = rank2.astype(BF16)
            e2_ref[hd, :, ls] = e2.astype(BF16)
        return carry

    lax.fori_loop(0, PEER_HEADS, head_body, 0)


def _peer_select(lay, x, g, sc, sh, wq_t, k1, k2):
    tt = lay.tile
    sel_spec = pl.BlockSpec((PEER_HEADS, N_KEYS, tt), lambda i: (0, 0, i))
    sel_shape = lambda dt: jax.ShapeDtypeStruct((PEER_HEADS, N_KEYS, lay.n), dt)
    return pl.pallas_call(
        _peer_select_kernel,
        grid=(lay.n_tiles,),
        in_specs=[lay.tok_spec(), _full_spec((1, D_MODEL)), lay.mod_spec(), lay.mod_spec(),
                  _full_spec((PEER_HEADS * D_QUERY, D_MODEL)),
                  _full_spec((PEER_HEADS, N_KEYS, D_QUERY // 2)), _full_spec((PEER_HEADS, N_KEYS, D_QUERY // 2))],
        out_specs=[pl.BlockSpec((D_MODEL, tt), lambda i: (0, i))] + [sel_spec] * 4,
        out_shape=[jax.ShapeDtypeStruct((D_MODEL, lay.n), BF16),
                   sel_shape(F32), sel_shape(F32), sel_shape(BF16), sel_shape(BF16)],
        scratch_shapes=[pltpu.VMEM((PEER_HEADS * D_QUERY, tt), F32)],
        compiler_params=_params("arbitrary"),
        name="peer_select",
    )(x, g.reshape(1, D_MODEL), sc, sh, wq_t, k1, k2)


def _peer_dense_kernel(*refs):
    groups = PEER_EXPERT_TILE // PEER_GROUP
    ht_ref, c_ref, e1_ref, rk_ref, e2_ref = refs[:5]
    u_refs = refs[5:5 + groups]
    vt_refs = refs[5 + groups:5 + 2 * groups]
    x_ref, gt_ref, o_ref, acc_ref, coef_ref, z_ref = refs[5 + 2 * groups:]
    e = pl.program_id(1)
    sr = PEER_STEP_ROWS
    spg = PEER_GROUP // (N_KEYS * sr)
    piece = D_MODEL // spg
    last_slot = (groups - 1) % 2

    @pl.when(e == 0)
    def _():
        acc_ref[...] = jnp.zeros_like(acc_ref)

    def pre_act(step):
        g, ii = divmod(step, spg)
        return _fdot(u_refs[g][0, ii * sr * N_KEYS:(ii + 1) * sr * N_KEYS, :], ht_ref[...])

    z_ref[0] = pre_act(0)
    for g in range(groups):
        slot = g % 2
        for ii in range(spg):
            step = g * spg + ii
            if step + 1 < groups * spg:
                z_ref[(step + 1) % 2] = pre_act(step + 1)
            if g > 0:
                ps = slice(ii * piece, (ii + 1) * piece)
                acc_ref[ps, :] += _fdot(vt_refs[g - 1][0, ps, :], coef_ref[1 - slot])
            for rr in range(sr):
                row = step * sr + rr
                gate = None
                for hd in range(PEER_HEADS):
                    cnt = c_ref[hd, row:row + 1, :].astype(BF16)
                    e1 = e1_ref[hd, row:row + 1, :].astype(BF16)
                    term = jnp.where(rk_ref[hd] < cnt, e2_ref[hd], jnp.zeros((), BF16)) * e1
                    gate = term if gate is None else gate + term
                z = z_ref[step % 2, rr * N_KEYS:(rr + 1) * N_KEYS, :]
                act = 0.5 * z * (1.0 + lax.erf(z * (1.0 / math.sqrt(2.0))))
                lo = (ii * sr + rr) * N_KEYS
                coef_ref[slot, lo:lo + N_KEYS, :] = gate * act.astype(BF16)

    acc_ref[...] += _fdot(vt_refs[groups - 1][0], coef_ref[last_slot])

    @pl.when(e == pl.num_programs(1) - 1)
    def _():
        o_ref[...] = x_ref[...] + gt_ref[0] * acc_ref[...].T


def _peer_weight_parts(peer_u, peer_v):
    groups = PEER_EXPERT_TILE // PEER_GROUP
    tiles = N_EXPERTS // PEER_EXPERT_TILE
    u5 = peer_u.astype(BF16).reshape(DEPTH, tiles, groups, PEER_GROUP, D_MODEL)
    v5 = jnp.swapaxes(peer_v.astype(BF16).reshape(DEPTH, tiles, groups, PEER_GROUP, D_MODEL), 3, 4)
    u_parts = [u5[:, :, k] for k in range(groups)]
    vt_parts = [v5[:, :, k] for k in range(groups)]
    return u_parts, vt_parts


def _peer_dense(lay, ht, cnt, e1, rank2, e2, u_parts, vt_parts, x, gt):
    tt = lay.tile
    groups = PEER_EXPERT_TILE // PEER_GROUP
    rows_per_tile = PEER_EXPERT_TILE // N_KEYS
    tpg = lay.tiles_per_group
    row_spec = pl.BlockSpec((PEER_HEADS, rows_per_tile, tt), lambda i, e: (0, e, i))
    key_spec = pl.BlockSpec((PEER_HEADS, N_KEYS, tt), lambda i, e: (0, 0, i))
    u_spec = pl.BlockSpec((1, PEER_GROUP, D_MODEL), lambda i, e: (e, 0, 0))
    vt_spec = pl.BlockSpec((1, D_MODEL, PEER_GROUP), lambda i, e: (e, 0, 0))
    return pl.pallas_call(
        _peer_dense_kernel,
        grid=(lay.n_tiles, N_EXPERTS // PEER_EXPERT_TILE),
        in_specs=[pl.BlockSpec((D_MODEL, tt), lambda i, e: (0, i)), row_spec, row_spec, key_spec, key_spec]
        + [u_spec] * groups + [vt_spec] * groups + [
            pl.BlockSpec((tt, D_MODEL), lambda i, e: (i, 0)),
            pl.BlockSpec((1, lay.rows, D_MODEL), lambda i, e: (i // tpg, 0, 0))],
        out_specs=pl.BlockSpec((tt, D_MODEL), lambda i, e: (i, 0)),
        out_shape=jax.ShapeDtypeStruct((lay.n, D_MODEL), F32),
        scratch_shapes=[pltpu.VMEM((D_MODEL, tt), F32), pltpu.VMEM((2, PEER_GROUP, tt), BF16),
                        pltpu.VMEM((2, PEER_STEP_ROWS * N_KEYS, tt), F32)],
        compiler_params=_params("arbitrary", "arbitrary"),
        name="peer_dense",
    )(ht, cnt, e1, rank2, e2, *u_parts, *vt_parts, x, gt)


def _run(batch, seq, x, m_all, st_conv, st_shift, st_wkv, p):
    lay = _Layout(batch, seq)
    x = x.reshape(lay.n, D_MODEL)
    new_conv, new_shift, new_wkv = [], [], []
    v_first = None
    short = seq < WKV_CHUNK
    for i in range(DEPTH):
        sh_m, sc_m, gt_m, sh_f, sc_f, gt_f = (lay.expand(t) for t in jnp.split(m_all[i], 6, axis=-1))
        j = i // 2
        if i % 2 == 0 and lay.rows == 1:
            x, cbuf = _conv_fused(lay, x, p["g_mix"][i], sc_m, sh_m, p["conv_w_in"][j], p["conv_k"][j],
                                  p["conv_w_out"][j], gt_m, st_conv[j])
            new_conv.append(cbuf)
        elif i % 2 == 0:
            bg, u = _conv_in(lay, x, p["g_mix"][i], sc_m, sh_m, p["conv_w_in"][j])
            up = jnp.concatenate([st_conv[j], u.reshape(batch, seq, D_MODEL)], axis=1)
            u0 = up[:, 0:seq].reshape(lay.n, D_MODEL)
            u1 = up[:, 1:seq + 1].reshape(lay.n, D_MODEL)
            new_conv.append(up[:, -(CONV_WIDTH - 1):])
            x = _conv_out(lay, bg, u0, u1, u, p["conv_k"][j], p["conv_w_out"][j], x, gt_m)
        else:
            lay_p = _Layout(batch, seq, RWKV_PROJ_TILE)
            if lay_p.rows == 1:
                front = (x, p["g_mix"][i], lay_p.expand(m_all[i][:, D_MODEL:2 * D_MODEL]),
                         lay_p.expand(m_all[i][:, 0:D_MODEL]), st_shift[j])
            else:
                h = _norm_mod(lay, x, p["g_mix"][i], sc_m, sh_m)
                h3 = h.reshape(batch, seq, D_MODEL)
                hs = jnp.concatenate([st_shift[j][:, None, :], h3[:, :-1]], axis=1).reshape(lay.n, D_MODEL)
                new_shift.append(h3[:, -1])
                front = (h, hs)
            zero = jnp.zeros((D_MODEL,), F32)
            v0 = p["rwkv_v0"][j - 1] if j > 0 else zero
            vecs = jnp.stack([p["rwkv_w0"][j], p["rwkv_a0"][j], v0, p["rwkv_k_k"][j], p["rwkv_k_a"][j],
                              zero, zero, zero])
            v1 = p["rwkv_v1"][j - 1] if j > 0 else None
            v2 = p["rwkv_v2"][j - 1] if j > 0 else None
            r, km, kk, v, lw, a, g, *shift_out = _rwkv_proj(
                lay_p, front, v_first if j > 0 else None, p["rwkv_mix"][j], vecs,
                p["rwkv_wr"][j], p["rwkv_wk"][j], p["rwkv_wv"][j], p["rwkv_w1"][j], p["rwkv_w2"][j],
                p["rwkv_a1"][j], p["rwkv_a2"][j], p["rwkv_g1"][j], p["rwkv_g2"][j], v1, v2)
            new_shift.extend(shift_out)
            if j == 0:
                v_first = v
            scan_in = (r, km, kk, v, lw, a, g)
            if short:
                pad = lambda t: jnp.pad(t.reshape(batch, seq, D_MODEL),
                                        ((0, 0), (0, WKV_SHORT_CHUNK - seq), (0, 0))).reshape(-1, D_MODEL)
                scan_in = tuple(pad(t) for t in scan_in)
                yg, s_new = _wkv(batch, WKV_SHORT_CHUNK, WKV_SHORT_CHUNK, N_HEADS, WKV_SHORT_SEQS_PER_STEP, *scan_in,
                                 p["rwkv_r_k"][j], p["rwkv_lnw"][j], p["rwkv_lnb"][j], st_wkv[j])
                yg = yg.reshape(batch, WKV_SHORT_CHUNK, D_MODEL)[:, :seq].reshape(lay.n, D_MODEL)
            else:
                yg, s_new = _wkv(batch, seq, WKV_CHUNK, N_HEADS, WKV_SEQS_PER_STEP, *scan_in,
                                 p["rwkv_r_k"][j], p["rwkv_lnw"][j], p["rwkv_lnb"][j], st_wkv[j])
            new_wkv.append(s_new)
            x = _mm_res(lay, yg, p["rwkv_wo"][j], x, gt_m)
        ht, cnt, e1, rank2, e2 = _peer_select(lay, x, p["g_ffn"][i], sc_f, sh_f,
                                              p["wq_t"][i], p["k1_b"][i], p["k2_b"][i])
        x = _peer_dense(lay, ht, cnt, e1, rank2, e2, [t[i] for t in p["u_parts"]],
                        [t[i] for t in p["vt_parts"]], x, gt_f)
    y = _final_norm(lay, x, p["g_final"]).reshape(batch, seq, D_MODEL)
    return y, jnp.stack(new_conv), jnp.stack(new_shift), jnp.stack(new_wkv)


def kernel(x_prompt, x_sample, state_conv, state_shift, state_wkv, c_prompt, c_sample, w_ada, b_ada, g_mix, g_ffn, g_final, conv_w_in, conv_k, conv_w_out, rwkv_mix, rwkv_wr, rwkv_wk, rwkv_wv, rwkv_wo, rwkv_w0, rwkv_w1, rwkv_w2, rwkv_a0, rwkv_a1, rwkv_a2, rwkv_v0, rwkv_v1, rwkv_v2, rwkv_g1, rwkv_g2, rwkv_k_k, rwkv_k_a, rwkv_r_k, rwkv_lnw, rwkv_lnb, peer_wq, peer_k1, peer_k2, peer_u, peer_v):
    p = dict(
        g_mix=g_mix, g_ffn=g_ffn, g_final=g_final, conv_w_in=conv_w_in, conv_k=conv_k, conv_w_out=conv_w_out,
        rwkv_mix=rwkv_mix, rwkv_wr=rwkv_wr, rwkv_wk=rwkv_wk, rwkv_wv=rwkv_wv, rwkv_wo=rwkv_wo,
        rwkv_w0=rwkv_w0, rwkv_w1=rwkv_w1, rwkv_w2=rwkv_w2, rwkv_a0=rwkv_a0, rwkv_a1=rwkv_a1, rwkv_a2=rwkv_a2,
        rwkv_v0=rwkv_v0, rwkv_v1=rwkv_v1, rwkv_v2=rwkv_v2, rwkv_g1=rwkv_g1, rwkv_g2=rwkv_g2,
        rwkv_k_k=rwkv_k_k, rwkv_k_a=rwkv_k_a, rwkv_r_k=rwkv_r_k, rwkv_lnw=rwkv_lnw, rwkv_lnb=rwkv_lnb,
        wq_t=jnp.swapaxes(peer_wq, 1, 2).astype(BF16),
        k1_b=peer_k1.astype(BF16), k2_b=peer_k2.astype(BF16),
        **dict(zip(("u_parts", "vt_parts"), _peer_weight_parts(peer_u, peer_v))),
    )
    n_prompt, seq_prompt = x_prompt.shape[0], x_prompt.shape[1]
    n_sample, seq_sample = x_sample.shape[0], x_sample.shape[1]
    m_all = _adaln(jnp.concatenate([c_prompt, c_sample], axis=0), w_ada, b_ada)
    zeros = lambda *s: jnp.zeros(s, F32)
    n_conv, n_rwkv = (DEPTH + 1) // 2, DEPTH // 2
    y_p, p_conv, p_shift, p_wkv = _run(
        n_prompt, seq_prompt, x_prompt, m_all[:, :n_prompt],
        zeros(n_conv, n_prompt, CONV_WIDTH - 1, D_MODEL), zeros(n_rwkv, n_prompt, D_MODEL),
        zeros(n_rwkv, n_prompt, N_HEADS, HEAD_SIZE, HEAD_SIZE), p)
    y_s, s_conv, s_shift, s_wkv = _run(
        n_sample, seq_sample, x_sample, m_all[:, n_prompt:], state_conv, state_shift, state_wkv, p)
    return (y_p, y_s, p_conv, p_shift, p_wkv, s_conv, s_shift, s_wkv)
```

```python
import functools
import math

import jax
import jax.numpy as jnp
from jax import lax
from jax.experimental import pallas as pl
from jax.experimental.pallas import tpu as pltpu

D_MODEL = 1024
DEPTH = 4
CONV_WIDTH = 3
HEAD_SIZE = 64
N_HEADS = D_MODEL // HEAD_SIZE
N_KEYS = 128
N_EXPERTS = N_KEYS * N_KEYS
PEER_HEADS = 8
PEER_TOPK = 16
D_QUERY = 256
RMS_EPS = 1e-6
GN_EPS = 64e-5

SUBLANES = 8
LORA_PAD = 128
TOKEN_TILE = 512
RWKV_PROJ_TILE = 256
PEER_EXPERT_TILE = 2048
PEER_GROUP = 512
PEER_STEP_ROWS = 4
PEER_Z_AHEAD = 1
PEER_LANE_CHUNK = 256
WKV_CHUNK = 64
WKV_SHORT_CHUNK = 8
WKV_SEQS_PER_STEP = 2
WKV_SHORT_SEQS_PER_STEP = 4
VMEM_LIMIT = 56 * 1024 * 1024

F32 = jnp.float32
BF16 = jnp.bfloat16


def _params(*sem):
    return pltpu.CompilerParams(dimension_semantics=sem, vmem_limit_bytes=VMEM_LIMIT)


def _sigmoid(x):
    return 1.0 / (1.0 + jnp.exp(-x))


def _rms_mod(x, g, sc, sh):
    y = x * lax.rsqrt(jnp.mean(x * x, axis=-1, keepdims=True) + RMS_EPS)
    return (y * g) * (1.0 + sc) + sh


def _bdot(a, b):
    return jnp.dot(a.astype(BF16), b.astype(BF16), preferred_element_type=F32)


def _fdot(a, b):
    return jnp.dot(a, b, preferred_element_type=F32)


def _fdot_nt(a, b):
    return lax.dot_general(a, b, (((1,), (1,)), ((), ())), preferred_element_type=F32)


def _fdot_tn(a, b):
    return lax.dot_general(a, b, (((0,), (0,)), ((), ())), preferred_element_type=F32)


class _Layout:
    def __init__(self, batch, seq, tile=TOKEN_TILE):
        self.batch, self.seq = batch, seq
        self.n = batch * seq
        self.tile = min(tile, self.n)
        assert self.n % self.tile == 0
        self.n_tiles = self.n // self.tile
        if seq % self.tile == 0:
            self.rows = 1
            self.tiles_per_group = seq // self.tile
        else:
            assert self.tile % seq == 0
            self.rows = self.tile
            self.tiles_per_group = 1

    def expand(self, m):
        if self.rows == 1:
            return m[:, None, :]
        return jnp.repeat(m, self.seq, axis=0).reshape(self.n_tiles, self.tile, m.shape[-1])

    def mod_spec(self):
        tpg = self.tiles_per_group
        return pl.BlockSpec((1, self.rows, D_MODEL), lambda i: (i // tpg, 0, 0))

    def tok_spec(self, width=D_MODEL):
        return pl.BlockSpec((self.tile, width), lambda i: (i, 0))


def _full_spec(shape):
    nd = len(shape)
    return pl.BlockSpec(shape, lambda *_: (0,) * nd)


def _adaln_kernel(c_ref, w_ref, b_ref, o_ref):
    c = c_ref[...]
    cs = c * _sigmoid(c)
    o_ref[0] = _fdot(cs, w_ref[0]) + b_ref[0]


def _adaln(c, w_ada, b_ada):
    nb = c.shape[0]
    tn = 1536
    return pl.pallas_call(
        _adaln_kernel,
        grid=(DEPTH, 6 * D_MODEL // tn),
        in_specs=[
            pl.BlockSpec((nb, D_MODEL), lambda l, j: (0, 0)),
            pl.BlockSpec((1, D_MODEL, tn), lambda l, j: (l, 0, j)),
            pl.BlockSpec((1, 1, tn), lambda l, j: (l, 0, j)),
        ],
        out_specs=pl.BlockSpec((1, nb, tn), lambda l, j: (l, 0, j)),
        out_shape=jax.ShapeDtypeStruct((DEPTH, nb, 6 * D_MODEL), F32),
        compiler_params=_params("arbitrary", "arbitrary"),
        name="adaln",
    )(c, w_ada, b_ada.reshape(DEPTH, 1, 6 * D_MODEL))


def _norm_mod_kernel(x_ref, g_ref, sc_ref, sh_ref, o_ref):
    o_ref[...] = _rms_mod(x_ref[...], g_ref[...], sc_ref[0], sh_ref[0])


def _norm_mod(lay, x, g, sc, sh):
    return pl.pallas_call(
        _norm_mod_kernel,
        grid=(lay.n_tiles,),
        in_specs=[lay.tok_spec(), _full_spec((1, D_MODEL)), lay.mod_spec(), lay.mod_spec()],
        out_specs=lay.tok_spec(),
        out_shape=jax.ShapeDtypeStruct((lay.n, D_MODEL), F32),
        compiler_params=_params("arbitrary"),
        name="norm_mod",
    )(x, g.reshape(1, D_MODEL), sc, sh)


def _final_norm_kernel(x_ref, g_ref, o_ref):
    x = x_ref[...]
    o_ref[...] = x * lax.rsqrt(jnp.mean(x * x, axis=-1, keepdims=True) + RMS_EPS) * g_ref[...]


def _final_norm(lay, x, g):
    return pl.pallas_call(
        _final_norm_kernel,
        grid=(lay.n_tiles,),
        in_specs=[lay.tok_spec(), _full_spec((1, D_MODEL))],
        out_specs=lay.tok_spec(),
        out_shape=jax.ShapeDtypeStruct((lay.n, D_MODEL), F32),
        compiler_params=_params("arbitrary"),
        name="final_norm",
    )(x, g.reshape(1, D_MODEL))


def _conv_in_kernel(x_ref, g_ref, sc_ref, sh_ref, w_ref, bg_ref, u_ref):
    h = _rms_mod(x_ref[...], g_ref[...], sc_ref[0], sh_ref[0]).astype(BF16)
    d = D_MODEL
    bg_ref[...] = _fdot(h, w_ref[:, 0:d])
    u_ref[...] = _fdot(h, w_ref[:, d:2 * d]) * _fdot(h, w_ref[:, 2 * d:3 * d])


def _conv_in(lay, x, g, sc, sh, w_in):
    shp = jax.ShapeDtypeStruct((lay.n, D_MODEL), F32)
    return pl.pallas_call(
        _conv_in_kernel,
        grid=(lay.n_tiles,),
        in_specs=[lay.tok_spec(), _full_spec((1, D_MODEL)), lay.mod_spec(), lay.mod_spec(),
                  _full_spec((D_MODEL, 3 * D_MODEL))],
        out_specs=[lay.tok_spec(), lay.tok_spec()],
        out_shape=[shp, shp],
        compiler_params=_params("arbitrary"),
        name="conv_in",
    )(x, g.reshape(1, D_MODEL), sc, sh, w_in.astype(BF16))


def _conv_out_kernel(bg_ref, u0_ref, u1_ref, u2_ref, ck_ref, w_ref, x_ref, gt_ref, o_ref):
    conv = u0_ref[...] * ck_ref[0:1, :] + u1_ref[...] * ck_ref[1:2, :] + u2_ref[...] * ck_ref[2:3, :]
    y = _bdot(bg_ref[...] * conv, w_ref[...])
    o_ref[...] = x_ref[...] + gt_ref[0] * y


def _conv_out(lay, bg, u0, u1, u2, conv_k, w_out, x, gt):
    return pl.pallas_call(
        _conv_out_kernel,
        grid=(lay.n_tiles,),
        in_specs=[lay.tok_spec()] * 4 + [_full_spec((CONV_WIDTH, D_MODEL)), _full_spec((D_MODEL, D_MODEL)),
                                         lay.tok_spec(), lay.mod_spec()],
        out_specs=lay.tok_spec(),
        out_shape=jax.ShapeDtypeStruct((lay.n, D_MODEL), F32),
        compiler_params=_params("arbitrary"),
        name="conv_out",
    )(bg, u0, u1, u2, conv_k, w_out.astype(BF16), x, gt)


def _conv_fused_kernel(x_ref, g_ref, sc_ref, sh_ref, win_ref, ck_ref, wout_ref, gt_ref, st_ref,
                       o_ref, ns_ref, carry_ref, *, tiles_per_seq):
    x = x_ref[...]
    h = _rms_mod(x, g_ref[...], sc_ref[0], sh_ref[0]).astype(BF16)
    d = D_MODEL
    rows = x.shape[0]
    u = _fdot(h, win_ref[:, d:2 * d]) * _fdot(h, win_ref[:, 2 * d:3 * d])
    @pl.when(pl.program_id(0) == 0)
    def _():
        carry_ref[...] = jnp.zeros_like(carry_ref)

    first = pl.program_id(0) % tiles_per_seq == 0
    prev = jnp.where(first, st_ref[0], carry_ref[...])
    tail = u[rows - (CONV_WIDTH - 1):, :]
    carry_ref[...] = tail
    ns_ref[0] = tail
    row = lax.broadcasted_iota(jnp.int32, u.shape, 0)
    um1 = jnp.where(row == 0, prev[1:2, :], pltpu.roll(u, 1, axis=0))
    um2 = jnp.where(row == 0, prev[0:1, :], jnp.where(row == 1, prev[1:2, :], pltpu.roll(u, 2, axis=0)))
    conv = um2 * ck_ref[0:1, :] + um1 * ck_ref[1:2, :] + u * ck_ref[2:3, :]
    bg = _fdot(h, win_ref[:, 0:d])
    o_ref[...] = x + gt_ref[0] * _bdot(bg * conv, wout_ref[...])


def _conv_fused(lay, x, g, sc, sh, w_in, conv_k, w_out, gt, st):
    assert lay.rows == 1 and CONV_WIDTH == 3
    tps = lay.tiles_per_group
    st_spec = pl.BlockSpec((1, CONV_WIDTH - 1, D_MODEL), lambda i: (i // tps, 0, 0))
    return pl.pallas_call(
        functools.partial(_conv_fused_kernel, tiles_per_seq=tps),
        grid=(lay.n_tiles,),
        in_specs=[lay.tok_spec(), _full_spec((1, D_MODEL)), lay.mod_spec(), lay.mod_spec(),
                  _full_spec((D_MODEL, 3 * D_MODEL)), _full_spec((CONV_WIDTH, D_MODEL)),
                  _full_spec((D_MODEL, D_MODEL)), lay.mod_spec(), st_spec],
        out_specs=[lay.tok_spec(), st_spec],
        out_shape=[jax.ShapeDtypeStruct((lay.n, D_MODEL), F32),
                   jax.ShapeDtypeStruct((lay.batch, CONV_WIDTH - 1, D_MODEL), F32)],
        scratch_shapes=[pltpu.VMEM((CONV_WIDTH - 1, D_MODEL), F32)],
        compiler_params=_params("arbitrary"),
        name="conv_fused",
    )(x, g.reshape(1, D_MODEL), sc, sh, w_in.astype(BF16), conv_k, w_out.astype(BF16), gt, st)


def _mm_res_kernel(a_ref, w_ref, x_ref, gt_ref, o_ref):
    o_ref[...] = x_ref[...] + gt_ref[0] * _bdot(a_ref[...], w_ref[...])


def _mm_res(lay, a, w, x, gt):
    return pl.pallas_call(
        _mm_res_kernel,
        grid=(lay.n_tiles,),
        in_specs=[lay.tok_spec(), _full_spec((D_MODEL, D_MODEL)), lay.tok_spec(), lay.mod_spec()],
        out_specs=lay.tok_spec(),
        out_shape=jax.ShapeDtypeStruct((lay.n, D_MODEL), F32),
        compiler_params=_params("arbitrary"),
        name="mm_res",
    )(a, w.astype(BF16), x, gt)


def _rwkv_proj_kernel(*refs, has_vres, tiles_per_seq):
    refs = list(refs)
    fused = tiles_per_seq is not None
    if fused:
        x_ref, gm_ref, sc_ref, sh_ref, st_ref = refs[:5]
        del refs[:5]
        carry_ref = refs.pop()
        ns_ref = refs.pop()
    else:
        h_ref, hs_ref = refs[:2]
        del refs[:2]
    vf_ref = refs.pop(0) if has_vres else None
    mix_ref, vec_ref, wr_ref, wk_ref, wv_ref, w1_ref, w2_ref, a1_ref, a2_ref, g1_ref, g2_ref = refs[:11]
    del refs[:11]
    if has_vres:
        v1_ref, v2_ref = refs[:2]
        del refs[:2]
    r_ref, km_ref, kk_ref, v_ref, lw_ref, a_ref, g_ref = refs
    if fused:
        @pl.when(pl.program_id(0) == 0)
        def _():
            carry_ref[...] = jnp.zeros_like(carry_ref)

        h = _rms_mod(x_ref[...], gm_ref[...], sc_ref[0], sh_ref[0])
        rows = h.shape[0]
        first = pl.program_id(0) % tiles_per_seq == 0
        prev = jnp.where(first, st_ref[0], carry_ref[0:1, :])
        last = h[rows - 1:rows, :]
        carry_ref[0:1, :] = last
        ns_ref[0] = last
        row = lax.broadcasted_iota(jnp.int32, h.shape, 0)
        hs = jnp.where(row == 0, prev, pltpu.roll(h, 1, axis=0))
    else:
        h = h_ref[...]
        hs = hs_ref[...]
    xx = hs - h
    mixed = lambda i: (h + xx * mix_ref[i:i + 1, :]).astype(BF16)
    w0, a0, v0, k_k, k_a = (vec_ref[i:i + 1, :] for i in range(5))

    r_ref[...] = _fdot(mixed(0), wr_ref[...])
    wl = w0 + _bdot(jnp.tanh(_fdot(mixed(1), w1_ref[...])), w2_ref[...])
    z = -wl
    wlog = -(jnp.maximum(z, 0.0) + jnp.log(1.0 + jnp.exp(-jnp.abs(z)))) - 0.5
    lw_ref[...] = -jnp.exp(wlog)
    k = _fdot(mixed(2), wk_ref[...])
    xv = mixed(3)
    v = _fdot(xv, wv_ref[...])
    if has_vres:
        v = v + (vf_ref[...] - v) * _sigmoid(v0 + _bdot(_fdot(xv, v1_ref[...]), v2_ref[...]))
    v_ref[...] = v
    a = _sigmoid(a0 + _bdot(_fdot(mixed(4), a1_ref[...]), a2_ref[...]))
    a_ref[...] = a
    g_ref[...] = _bdot(_sigmoid(_fdot(mixed(5), g1_ref[...])), g2_ref[...])
    kk_ref[...] = k * k_k
    km_ref[...] = k * (1.0 + (a - 1.0) * k_a)


def _pad_cols(w):
    return jnp.pad(w, ((0, 0), (0, LORA_PAD - w.shape[1]))).astype(BF16)


def _pad_rows(w):
    return jnp.pad(w, ((0, LORA_PAD - w.shape[0]), (0, 0))).astype(BF16)


def _rwkv_proj(lay, front, v_first, mix, vecs, wr, wk, wv, w1, w2, a1, a2, g1, g2, v1, v2):
    has_vres = v_first is not None
    fused = len(front) == 5
    sq = _full_spec((D_MODEL, D_MODEL))
    down = _full_spec((D_MODEL, LORA_PAD))
    up = _full_spec((LORA_PAD, D_MODEL))
    tps = lay.tiles_per_group
    row_spec = pl.BlockSpec((1, 1, D_MODEL), lambda i: (i // tps, 0, 0))
    if fused:
        assert lay.rows == 1
        x, g, sc, sh, st = front
        in_specs = [lay.tok_spec(), _full_spec((1, D_MODEL)), lay.mod_spec(), lay.mod_spec(), row_spec]
        args = [x, g.reshape(1, D_MODEL), sc, sh, st[:, None, :]]
    else:
        in_specs = [lay.tok_spec(), lay.tok_spec()]
        args = list(front)
    in_specs += [lay.tok_spec()] if has_vres else []
    in_specs += [_full_spec((6, D_MODEL)), _full_spec((8, D_MODEL)), sq, sq, sq, down, up, down, up, down, up]
    args += [v_first] if has_vres else []
    args += [mix, vecs, wr.astype(BF16), wk.astype(BF16), wv.astype(BF16), _pad_cols(w1), _pad_rows(w2),
             _pad_cols(a1), _pad_rows(a2), _pad_cols(g1), _pad_rows(g2)]
    if has_vres:
        in_specs += [down, up]
        args += [_pad_cols(v1), _pad_rows(v2)]
    shp = jax.ShapeDtypeStruct((lay.n, D_MODEL), F32)
    outs = pl.pallas_call(
        functools.partial(_rwkv_proj_kernel, has_vres=has_vres, tiles_per_seq=tps if fused else None),
        grid=(lay.n_tiles,),
        in_specs=in_specs,
        out_specs=[lay.tok_spec()] * 7 + ([row_spec] if fused else []),
        out_shape=[shp] * 7 + ([jax.ShapeDtypeStruct((lay.batch, 1, D_MODEL), F32)] if fused else []),
        scratch_shapes=[pltpu.VMEM((SUBLANES, D_MODEL), F32)] if fused else [],
        compiler_params=_params("arbitrary"),
        name="rwkv_proj",
    )(*args)
    return outs if not fused else (*outs[:7], outs[7][:, 0, :])


def _wkv_kernel(r_ref, km_ref, kk_ref, v_ref, lw_ref, a_ref, g_ref, rk_ref, lnw_ref, lnb_ref, s0_ref,
                y_ref, s_ref, st_ref, *, chunk, heads, seqs):
    t = pl.program_id(2)
    L = chunk

    pairs = heads // 2
    pw = 2 * HEAD_SIZE
    lo, hi = slice(0, HEAD_SIZE), slice(HEAD_SIZE, pw)
    chains = [(q, p) for q in range(seqs) for p in range(pairs)]

    @pl.when(t == 0)
    def _():
        st_ref[...] = jnp.zeros_like(st_ref)
        for n, (q, p) in enumerate(chains):
            st_ref[n, lo, lo] = s0_ref[q, 2 * p]
            st_ref[n, hi, hi] = s0_ref[q, 2 * p + 1]

    row = lax.broadcasted_iota(jnp.int32, (L, 2 * L), 0)
    col = lax.broadcasted_iota(jnp.int32, (L, 2 * L), 1)
    col = jnp.where(col >= L, col - L, col)
    strict2 = row > col
    incl2 = row >= col
    tri = incl2[:, :L].astype(F32)
    even = lax.broadcasted_iota(jnp.int32, (1, pw), 1) < HEAD_SIZE
    same_head = ((lax.broadcasted_iota(jnp.int32, (pw, pw), 0) < HEAD_SIZE)
                 == (lax.broadcasted_iota(jnp.int32, (pw, pw), 1) < HEAD_SIZE))
    ones_bd = same_head.astype(BF16)

    def head_sum(x):
        x_hi = x.astype(BF16)
        x_lo = (x - x_hi.astype(F32)).astype(BF16)
        return _fdot(x_hi, ones_bd) + _fdot(x_lo, ones_bd)

    pick = lambda x0, x1: jnp.where(even, x0, x1)

    each = lambda f, *cols: [f(*xs) for xs in zip(*cols)]
    sls = [slice(p * pw, (p + 1) * pw) for _, p in chains]
    r, km, kkr, v, lw, asig, g = ([ref[q, :, slice(p * pw, (p + 1) * pw)] for q, p in chains]
                                  for ref in (r_ref, km_ref, kk_ref, v_ref, lw_ref, a_ref, g_ref))
    s0 = [st_ref[n] for n in range(len(chains))]
    c = each(lambda x: _fdot(tri, x), lw)
    kk = each(lambda x: x * lax.rsqrt(jnp.maximum(head_sum(x * x), 1e-24)), kkr)
    b = each(lambda x, y: x * y, kk, asig)
    c_last = each(lambda x: x[L - 1:L, :], c)
    ginv = each(lambda x: jnp.exp(-x), c)
    lhs = each(lambda kk_, c_, lw_, r_: jnp.concatenate([-kk_ * jnp.exp(c_ - lw_), r_ * jnp.exp(c_)], axis=0),
               kk, c, lw, r)
    rhs = each(lambda b_, km_, gi: jnp.concatenate([b_ * gi, km_ * gi], axis=0), b, km, ginv)
    qk0 = each(lambda l_, r_: _fdot_nt(jnp.where(even, l_, 0.0), r_), lhs, rhs)
    qk1 = each(lambda l_, r_: _fdot_nt(jnp.where(even, 0.0, l_), r_), lhs, rhs)
    hs0 = each(_fdot_nt, lhs, s0)
    zv = each(lambda v_: jnp.concatenate([jnp.zeros_like(v_), v_], axis=0), v)
    m0 = each(lambda q: jnp.where(strict2, q[:L, :], 0.0), qk0)
    m1 = each(lambda q: jnp.where(strict2, q[:L, :], 0.0), qk1)
    u = each(lambda h_, a0, a1, z_: h_[:L] + pick(_fdot(a0, z_), _fdot(a1, z_)), hs0, m0, m1, zv)
    p0 = each(lambda m: m[:, :L], m0)
    p1 = each(lambda m: m[:, :L], m1)
    for it in range(int(math.log2(L))):
        if it > 0:
            p0 = each(lambda x: _fdot(x, x), p0)
            p1 = each(lambda x: _fdot(x, x), p1)
        u = each(lambda u_, a0, a1: u_ + pick(_fdot(a0, u_), _fdot(a1, u_)), u, p0, p1)
    uv = each(lambda u_, v_: jnp.concatenate([u_, v_], axis=0), u, v)
    bk = each(lambda cl, c_, b_, km_: jnp.concatenate([b_, km_], axis=0)
              * jnp.exp(cl - jnp.concatenate([c_, c_], axis=0)), c_last, c, b, km)
    s_new = each(lambda s_, cl, uv_, bk_: jnp.where(same_head, s_ * jnp.exp(cl) + _fdot_tn(uv_, bk_), 0.0),
                 s0, c_last, uv, bk)
    for n in range(len(chains)):
        st_ref[n] = s_new[n]
    y = each(lambda h_, q0, q1, uv_: h_[L:] + pick(_fdot(jnp.where(incl2, q0[L:, :], 0.0), uv_),
                                                   _fdot(jnp.where(incl2, q1[L:, :], 0.0), uv_)),
             hs0, qk0, qk1, uv)

    def finish(y_, r_, km_, v_, g_, sl):
        yc = y_ - head_sum(y_) * (1.0 / HEAD_SIZE)
        var = head_sum(yc * yc) * (1.0 / HEAD_SIZE)
        yn = yc * lax.rsqrt(var + GN_EPS) * lnw_ref[:, sl] + lnb_ref[:, sl]
        bonus = head_sum(r_ * km_ * rk_ref[:, sl]) * v_
        return (yn + bonus) * g_

    for (q, p), out in zip(chains, each(finish, y, r, km, v, g, sls)):
        y_ref[q, :, slice(p * pw, (p + 1) * pw)] = out

    @pl.when(t == pl.num_programs(2) - 1)
    def _():
        for n, (q, p) in enumerate(chains):
            s_ref[q, 2 * p] = st_ref[n, lo, lo]
            s_ref[q, 2 * p + 1] = st_ref[n, hi, hi]


def _wkv(batch, seq, chunk, heads, seqs, r, km, kk, v, lw, a, g, r_k, lnw, lnb, s0):
    n_t = seq // chunk
    width = heads * HEAD_SIZE
    tok = pl.BlockSpec((seqs, chunk, width), lambda b, h, t: (b, t, h))
    vec = pl.BlockSpec((1, width), lambda b, h, t: (0, h))
    st = pl.BlockSpec((seqs, heads, HEAD_SIZE, HEAD_SIZE), lambda b, h, t: (b, h, 0, 0))
    as3d = lambda x: x.reshape(batch, seq, D_MODEL)
    y, s_new = pl.pallas_call(
        functools.partial(_wkv_kernel, chunk=chunk, heads=heads, seqs=seqs),
        grid=(batch // seqs, N_HEADS // heads, n_t),
        in_specs=[tok] * 7 + [vec] * 3 + [st],
        out_specs=[tok, st],
        out_shape=[jax.ShapeDtypeStruct((batch, seq, D_MODEL), F32),
                   jax.ShapeDtypeStruct((batch, N_HEADS, HEAD_SIZE, HEAD_SIZE), F32)],
        scratch_shapes=[pltpu.VMEM((seqs * heads // 2, 2 * HEAD_SIZE, 2 * HEAD_SIZE), F32)],
        compiler_params=_params("arbitrary", "arbitrary", "arbitrary"),
        name="wkv",
    )(*(as3d(x) for x in (r, km, kk, v, lw, a, g)),
      r_k.reshape(1, D_MODEL), lnw.reshape(1, D_MODEL), lnb.reshape(1, D_MODEL), s0)
    return y.reshape(batch * seq, D_MODEL), s_new


def _col_reduce(x3, op):
    m = x3[0]
    for r in range(1, x3.shape[0]):
        m = op(m, x3[r])
    for shift in (4, 2, 1):
        m = op(m, pltpu.roll(m, shift, axis=0))
    return m


def _top16(s3, want_rank):
    work = s3
    rank = jnp.full(s3.shape, float(PEER_TOPK), F32) if want_rank else None
    vals = []
    for k in range(PEER_TOPK):
        m = _col_reduce(work, jnp.maximum)
        hit = work == m[None]
        if want_rank:
            rank = jnp.where(hit, float(k), rank)
        work = jnp.where(hit, -jnp.inf, work)
        vals.append(m)
    return vals, work, rank


def _sorting_network(n):
    pairs = []
    p = 1
    while p < n:
        k = p
        while k >= 1:
            for j in range(k % p, n - k, 2 * k):
                for i in range(min(k, n - j - k)):
                    if (i + j) // (2 * p) == (i + j + k) // (2 * p):
                        pairs.append((i + j, i + j + k))
            k //= 2
        p *= 2
    return pairs


def _top16_values(s3):
    cols = [s3[r] for r in range(s3.shape[0])]
    for i, j in _sorting_network(len(cols)):
        cols[i], cols[j] = jnp.maximum(cols[i], cols[j]), jnp.minimum(cols[i], cols[j])
    vals = []
    for k in range(PEER_TOPK):
        m = _col_reduce(cols[0][None], jnp.maximum)
        vals.append(m)
        if k + 1 < PEER_TOPK:
            hit = cols[0] == m
            for d in range(PEER_TOPK - 1 - k):
                cols[d] = jnp.where(hit, cols[d + 1], cols[d])
    return vals


def _rows_from(vals):
    row = lax.broadcasted_iota(jnp.int32, vals[0].shape, 0)
    out = vals[0]
    for s in range(1, SUBLANES):
        out = jnp.where(row == s, vals[s], out)
    return out


def _peer_select_chunk(s1, s2):
    n_keys, w = s1.shape
    s1 = s1.reshape(n_keys // SUBLANES, SUBLANES, w)
    s2 = s2.reshape(n_keys // SUBLANES, SUBLANES, w)
    v1 = _top16_values(s1)
    v2, _, rank2 = _top16(s2, True)
    v2_lo, v2_hi, v1_hi = _rows_from(v2[:SUBLANES]), _rows_from(v2[SUBLANES:]), _rows_from(v1[SUBLANES:])
    cand = jnp.stack([v1[0] + v2_lo, v1[0] + v2_hi] + [v1[a] + v2_lo for a in range(1, SUBLANES)]
                     + [v1_hi + v2[0]])
    work = cand
    tau = None
    for _ in range(PEER_TOPK):
        tau = _col_reduce(work, jnp.maximum)
        work = jnp.where(work == tau[None], -jnp.inf, work)
    top = v1[0] + v2[0]
    kept = cand >= tau[None]
    zsum = _col_reduce(jnp.where(kept, jnp.exp(cand - top[None]), 0.0), jnp.add)
    few = PEER_TOPK // 4
    cnt = jnp.zeros(s1.shape, F32)
    for b in range(few):
        cnt = cnt + jnp.where(s1 + v2[b][None] >= tau[None], 1.0, 0.0)
    upper = lax.broadcasted_iota(jnp.int32, cand.shape[1:], 0) >= few
    ones = lambda m: jnp.where(m, 1.0, 0.0)
    extra = [_col_reduce((ones(kept[0] & upper) + ones(kept[1]))[None], jnp.add),
             _col_reduce(ones(kept[2] & upper)[None], jnp.add),
             _col_reduce(ones(kept[3] & upper)[None], jnp.add)]
    for a, x in enumerate(extra):
        cnt = jnp.where(s1 == v1[a][None], cnt + x[None], cnt)
    cnt = jnp.where(s1 >= v1[PEER_TOPK - 1][None], cnt, 0.0)
    e1 = jnp.exp(s1 - v1[0][None])
    e2 = jnp.exp(s2 - v2[0][None]) / zsum[None]
    return tuple(t.reshape(n_keys, w) for t in (cnt, e1, rank2, e2))


def _peer_select_kernel(x_ref, g_ref, sc_ref, sh_ref, wq_ref, k1_ref, k2_ref,
                        ht_ref, c_ref, e1_ref, rk_ref, e2_ref, q_scr):
    h = _rms_mod(x_ref[...], g_ref[...], sc_ref[0], sh_ref[0])
    ht = h.T.astype(BF16)
    ht_ref[...] = ht
    q_scr[...] = _fdot(wq_ref[...], ht)
    half = D_QUERY // 2
    n_chunks = x_ref.shape[0] // PEER_LANE_CHUNK

    def head_body(hd, carry):
        q1 = q_scr[pl.ds(pl.multiple_of(hd * D_QUERY, D_QUERY), half), :]
        q2 = q_scr[pl.ds(pl.multiple_of(hd * D_QUERY + half, half), half), :]
        s1 = _bdot(k1_ref[hd], q1)
        s2 = _bdot(k2_ref[hd], q2)
        for ch in range(n_chunks):
            ls = slice(ch * PEER_LANE_CHUNK, (ch + 1) * PEER_LANE_CHUNK)
            cnt, e1, rank2, e2 = _peer_select_chunk(s1[:, ls], s2[:, ls])
            c_ref[hd, :, ls] = cnt
            e1_ref[hd, :, ls] = e1
            rk_ref[hd, :, ls] = rank2.astype(BF16)
            e2_ref[hd, :, ls] = e2.astype(BF16)
        return carry

    lax.fori_loop(0, PEER_HEADS, head_body, 0)


def _peer_select(lay, x, g, sc, sh, wq_t, k1, k2):
    tt = lay.tile
    sel_spec = pl.BlockSpec((PEER_HEADS, N_KEYS, tt), lambda i: (0, 0, i))
    sel_shape = lambda dt: jax.ShapeDtypeStruct((PEER_HEADS, N_KEYS, lay.n), dt)
    return pl.pallas_call(
        _peer_select_kernel,
        grid=(lay.n_tiles,),
        in_specs=[lay.tok_spec(), _full_spec((1, D_MODEL)), lay.mod_spec(), lay.mod_spec(),
                  _full_spec((PEER_HEADS * D_QUERY, D_MODEL)),
                  _full_spec((PEER_HEADS, N_KEYS, D_QUERY // 2)), _full_spec((PEER_HEADS, N_KEYS, D_QUERY // 2))],
        out_specs=[pl.BlockSpec((D_MODEL, tt), lambda i: (0, i))] + [sel_spec] * 4,
        out_shape=[jax.ShapeDtypeStruct((D_MODEL, lay.n), BF16),
                   sel_shape(F32), sel_shape(F32), sel_shape(BF16), sel_shape(BF16)],
        scratch_shapes=[pltpu.VMEM((PEER_HEADS * D_QUERY, tt), F32)],
        compiler_params=_params("arbitrary"),
        name="peer_select",
    )(x, g.reshape(1, D_MODEL), sc, sh, wq_t, k1, k2)


def _peer_dense_kernel(*refs):
    groups = PEER_EXPERT_TILE // PEER_GROUP
    ht_ref, c_ref, e1_ref, rk_ref, e2_ref = refs[:5]
    u_refs = refs[5:5 + groups]
    vt_refs = refs[5 + groups:5 + 2 * groups]
    x_ref, gt_ref, o_ref, acc_ref, coef_ref, z_ref = refs[5 + 2 * groups:]
    e = pl.program_id(1)
    sr = PEER_STEP_ROWS
    spg = PEER_GROUP // (N_KEYS * sr)
    piece = D_MODEL // spg
    last_slot = (groups - 1) % 2

    @pl.when(e == 0)
    def _():
        acc_ref[...] = jnp.zeros_like(acc_ref)

    def pre_act(step):
        g, ii = divmod(step, spg)
        return _fdot(u_refs[g][0, ii * sr * N_KEYS:(ii + 1) * sr * N_KEYS, :], ht_ref[...])

    ahead = PEER_Z_AHEAD
    for step in range(ahead):
        z_ref[step] = pre_act(step)
    for g in range(groups):
        slot = g % 2
        for ii in range(spg):
            step = g * spg + ii
            if step + ahead < groups * spg:
                z_ref[(step + ahead) % (ahead + 1)] = pre_act(step + ahead)
            if g > 0:
                ps = slice(ii * piece, (ii + 1) * piece)
                acc_ref[ps, :] += _fdot(vt_refs[g - 1][0, ps, :], coef_ref[1 - slot])
            for rr in range(sr):
                row = step * sr + rr
                gate = None
                for hd in range(PEER_HEADS):
                    cnt = c_ref[hd, row:row + 1, :].astype(BF16)
                    e1 = e1_ref[hd, row:row + 1, :].astype(BF16)
                    term = jnp.where(rk_ref[hd] < cnt, e2_ref[hd], jnp.zeros((), BF16)) * e1
                    gate = term if gate is None else gate + term
                z = z_ref[step % (ahead + 1), rr * N_KEYS:(rr + 1) * N_KEYS, :]
                act = 0.5 * z * (1.0 + lax.erf(z * (1.0 / math.sqrt(2.0))))
                lo = (ii * sr + rr) * N_KEYS
                coef_ref[slot, lo:lo + N_KEYS, :] = gate * act.astype(BF16)

    acc_ref[...] += _fdot(vt_refs[groups - 1][0], coef_ref[last_slot])

    @pl.when(e == pl.num_programs(1) - 1)
    def _():
        o_ref[...] = x_ref[...] + gt_ref[0] * acc_ref[...].T


def _peer_weight_parts(peer_u, peer_v):
    groups = PEER_EXPERT_TILE // PEER_GROUP
    tiles = N_EXPERTS // PEER_EXPERT_TILE
    u5 = peer_u.astype(BF16).reshape(DEPTH, tiles, groups, PEER_GROUP, D_MODEL)
    v5 = jnp.swapaxes(peer_v.astype(BF16).reshape(DEPTH, tiles, groups, PEER_GROUP, D_MODEL), 3, 4)
    u_parts = [u5[:, :, k] for k in range(groups)]
    vt_parts = [v5[:, :, k] for k in range(groups)]
    return u_parts, vt_parts


def _peer_dense(lay, ht, cnt, e1, rank2, e2, u_parts, vt_parts, x, gt):
    tt = lay.tile
    groups = PEER_EXPERT_TILE // PEER_GROUP
    rows_per_tile = PEER_EXPERT_TILE // N_KEYS
    tpg = lay.tiles_per_group
    row_spec = pl.BlockSpec((PEER_HEADS, rows_per_tile, tt), lambda i, e: (0, e, i))
    key_spec = pl.BlockSpec((PEER_HEADS, N_KEYS, tt), lambda i, e: (0, 0, i))
    u_spec = pl.BlockSpec((1, PEER_GROUP, D_MODEL), lambda i, e: (e, 0, 0))
    vt_spec = pl.BlockSpec((1, D_MODEL, PEER_GROUP), lambda i, e: (e, 0, 0))
    return pl.pallas_call(
        _peer_dense_kernel,
        grid=(lay.n_tiles, N_EXPERTS // PEER_EXPERT_TILE),
        in_specs=[pl.BlockSpec((D_MODEL, tt), lambda i, e: (0, i)), row_spec, row_spec, key_spec, key_spec]
        + [u_spec] * groups + [vt_spec] * groups + [
            pl.BlockSpec((tt, D_MODEL), lambda i, e: (i, 0)),
            pl.BlockSpec((1, lay.rows, D_MODEL), lambda i, e: (i // tpg, 0, 0))],
        out_specs=pl.BlockSpec((tt, D_MODEL), lambda i, e: (i, 0)),
        out_shape=jax.ShapeDtypeStruct((lay.n, D_MODEL), F32),
        scratch_shapes=[pltpu.VMEM((D_MODEL, tt), F32), pltpu.VMEM((2, PEER_GROUP, tt), BF16),
                        pltpu.VMEM((PEER_Z_AHEAD + 1, PEER_STEP_ROWS * N_KEYS, tt), F32)],
        compiler_params=_params("arbitrary", "arbitrary"),
        name="peer_dense",
    )(ht, cnt, e1, rank2, e2, *u_parts, *vt_parts, x, gt)


def _run(batch, seq, x, m_all, st_conv, st_shift, st_wkv, p):
    lay = _Layout(batch, seq)
    x = x.reshape(lay.n, D_MODEL)
    new_conv, new_shift, new_wkv = [], [], []
    v_first = None
    short = seq < WKV_CHUNK
    for i in range(DEPTH):
        sh_m, sc_m, gt_m, sh_f, sc_f, gt_f = (lay.expand(t) for t in jnp.split(m_all[i], 6, axis=-1))
        j = i // 2
        if i % 2 == 0 and lay.rows == 1:
            x, cbuf = _conv_fused(lay, x, p["g_mix"][i], sc_m, sh_m, p["conv_w_in"][j], p["conv_k"][j],
                                  p["conv_w_out"][j], gt_m, st_conv[j])
            new_conv.append(cbuf)
        elif i % 2 == 0:
            bg, u = _conv_in(lay, x, p["g_mix"][i], sc_m, sh_m, p["conv_w_in"][j])
            up = jnp.concatenate([st_conv[j], u.reshape(batch, seq, D_MODEL)], axis=1)
            u0 = up[:, 0:seq].reshape(lay.n, D_MODEL)
            u1 = up[:, 1:seq + 1].reshape(lay.n, D_MODEL)
            new_conv.append(up[:, -(CONV_WIDTH - 1):])
            x = _conv_out(lay, bg, u0, u1, u, p["conv_k"][j], p["conv_w_out"][j], x, gt_m)
        else:
            lay_p = _Layout(batch, seq, RWKV_PROJ_TILE)
            if lay_p.rows == 1:
                front = (x, p["g_mix"][i], lay_p.expand(m_all[i][:, D_MODEL:2 * D_MODEL]),
                         lay_p.expand(m_all[i][:, 0:D_MODEL]), st_shift[j])
            else:
                h = _norm_mod(lay, x, p["g_mix"][i], sc_m, sh_m)
                h3 = h.reshape(batch, seq, D_MODEL)
                hs = jnp.concatenate([st_shift[j][:, None, :], h3[:, :-1]], axis=1).reshape(lay.n, D_MODEL)
                new_shift.append(h3[:, -1])
                front = (h, hs)
            zero = jnp.zeros((D_MODEL,), F32)
            v0 = p["rwkv_v0"][j - 1] if j > 0 else zero
            vecs = jnp.stack([p["rwkv_w0"][j], p["rwkv_a0"][j], v0, p["rwkv_k_k"][j], p["rwkv_k_a"][j],
                              zero, zero, zero])
            v1 = p["rwkv_v1"][j - 1] if j > 0 else None
            v2 = p["rwkv_v2"][j - 1] if j > 0 else None
            r, km, kk, v, lw, a, g, *shift_out = _rwkv_proj(
                lay_p, front, v_first if j > 0 else None, p["rwkv_mix"][j], vecs,
                p["rwkv_wr"][j], p["rwkv_wk"][j], p["rwkv_wv"][j], p["rwkv_w1"][j], p["rwkv_w2"][j],
                p["rwkv_a1"][j], p["rwkv_a2"][j], p["rwkv_g1"][j], p["rwkv_g2"][j], v1, v2)
            new_shift.extend(shift_out)
            if j == 0:
                v_first = v
            scan_in = (r, km, kk, v, lw, a, g)
            if short:
                pad = lambda t: jnp.pad(t.reshape(batch, seq, D_MODEL),
                                        ((0, 0), (0, WKV_SHORT_CHUNK - seq), (0, 0))).reshape(-1, D_MODEL)
                scan_in = tuple(pad(t) for t in scan_in)
                yg, s_new = _wkv(batch, WKV_SHORT_CHUNK, WKV_SHORT_CHUNK, N_HEADS, WKV_SHORT_SEQS_PER_STEP, *scan_in,
                                 p["rwkv_r_k"][j], p["rwkv_lnw"][j], p["rwkv_lnb"][j], st_wkv[j])
                yg = yg.reshape(batch, WKV_SHORT_CHUNK, D_MODEL)[:, :seq].reshape(lay.n, D_MODEL)
            else:
                yg, s_new = _wkv(batch, seq, WKV_CHUNK, N_HEADS, WKV_SEQS_PER_STEP, *scan_in,
                                 p["rwkv_r_k"][j], p["rwkv_lnw"][j], p["rwkv_lnb"][j], st_wkv[j])
            new_wkv.append(s_new)
            x = _mm_res(lay, yg, p["rwkv_wo"][j], x, gt_m)
        ht, cnt, e1, rank2, e2 = _peer_select(lay, x, p["g_ffn"][i], sc_f, sh_f,
                                              p["wq_t"][i], p["k1_b"][i], p["k2_b"][i])
        x = _peer_dense(lay, ht, cnt, e1, rank2, e2, [t[i] for t in p["u_parts"]],
                        [t[i] for t in p["vt_parts"]], x, gt_f)
    y = _final_norm(lay, x, p["g_final"]).reshape(batch, seq, D_MODEL)
    return y, jnp.stack(new_conv), jnp.stack(new_shift), jnp.stack(new_wkv)


def kernel(x_prompt, x_sample, state_conv, state_shift, state_wkv, c_prompt, c_sample, w_ada, b_ada, g_mix, g_ffn, g_final, conv_w_in, conv_k, conv_w_out, rwkv_mix, rwkv_wr, rwkv_wk, rwkv_wv, rwkv_wo, rwkv_w0, rwkv_w1, rwkv_w2, rwkv_a0, rwkv_a1, rwkv_a2, rwkv_v0, rwkv_v1, rwkv_v2, rwkv_g1, rwkv_g2, rwkv_k_k, rwkv_k_a, rwkv_r_k, rwkv_lnw, rwkv_lnb, peer_wq, peer_k1, peer_k2, peer_u, peer_v):
    p = dict(
        g_mix=g_mix, g_ffn=g_ffn, g_final=g_final, conv_w_in=conv_w_in, conv_k=conv_k, conv_w_out=conv_w_out,
        rwkv_mix=rwkv_mix, rwkv_wr=rwkv_wr, rwkv_wk=rwkv_wk, rwkv_wv=rwkv_wv, rwkv_wo=rwkv_wo,
        rwkv_w0=rwkv_w0, rwkv_w1=rwkv_w1, rwkv_w2=rwkv_w2, rwkv_a0=rwkv_a0, rwkv_a1=rwkv_a1, rwkv_a2=rwkv_a2,
        rwkv_v0=rwkv_v0, rwkv_v1=rwkv_v1, rwkv_v2=rwkv_v2, rwkv_g1=rwkv_g1, rwkv_g2=rwkv_g2,
        rwkv_k_k=rwkv_k_k, rwkv_k_a=rwkv_k_a, rwkv_r_k=rwkv_r_k, rwkv_lnw=rwkv_lnw, rwkv_lnb=rwkv_lnb,
        wq_t=jnp.swapaxes(peer_wq, 1, 2).astype(BF16),
        k1_b=peer_k1.astype(BF16), k2_b=peer_k2.astype(BF16),
        **dict(zip(("u_parts", "vt_parts"), _peer_weight_parts(peer_u, peer_v))),
    )
    n_prompt, seq_prompt = x_prompt.shape[0], x_prompt.shape[1]
    n_sample, seq_sample = x_sample.shape[0], x_sample.shape[1]
    m_all = _adaln(jnp.concatenate([c_prompt, c_sample], axis=0), w_ada, b_ada)
    zeros = lambda *s: jnp.zeros(s, F32)
    n_conv, n_rwkv = (DEPTH + 1) // 2, DEPTH // 2
    y_p, p_conv, p_shift, p_wkv = _run(
        n_prompt, seq_prompt, x_prompt, m_all[:, :n_prompt],
        zeros(n_conv, n_prompt, CONV_WIDTH - 1, D_MODEL), zeros(n_rwkv, n_prompt, D_MODEL),
        zeros(n_rwkv, n_prompt, N_HEADS, HEAD_SIZE, HEAD_SIZE), p)
    y_s, s_conv, s_shift, s_wkv = _run(
        n_sample, seq_sample, x_sample, m_all[:, n_prompt:], state_conv, state_shift, state_wkv, p)
    return (y_p, y_s, p_conv, p_shift, p_wkv, s_conv, s_shift, s_wkv)
```

```python
import functools
import math

import jax
import jax.numpy as jnp
from jax import lax
from jax.experimental import pallas as pl
from jax.experimental.pallas import tpu as pltpu

D_MODEL = 1024
DEPTH = 4
CONV_WIDTH = 3
HEAD_SIZE = 64
N_HEADS = D_MODEL // HEAD_SIZE
N_KEYS = 128
N_EXPERTS = N_KEYS * N_KEYS
PEER_HEADS = 8
PEER_TOPK = 16
D_QUERY = 256
RMS_EPS = 1e-6
GN_EPS = 64e-5

SUBLANES = 8
LORA_PAD = 128
TOKEN_TILE = 512
RWKV_PROJ_TILE = 256
PEER_EXPERT_TILE = 2048
PEER_GROUP = 512
PEER_STEP_ROWS = 4
PEER_Z_AHEAD = 1
PEER_LANE_CHUNK = 256
WKV_CHUNK = 64
WKV_SHORT_CHUNK = 8
WKV_SEQS_PER_STEP = 2
WKV_SHORT_SEQS_PER_STEP = 4
VMEM_LIMIT = 56 * 1024 * 1024

F32 = jnp.float32
BF16 = jnp.bfloat16
RWKV_PROJ_DTYPES = (BF16, BF16, BF16, F32, F32, BF16, BF16)


def _params(*sem):
    return pltpu.CompilerParams(dimension_semantics=sem, vmem_limit_bytes=VMEM_LIMIT)


def _sigmoid(x):
    return 1.0 / (1.0 + jnp.exp(-x))


def _rms_mod(x, g, sc, sh):
    y = x * lax.rsqrt(jnp.mean(x * x, axis=-1, keepdims=True) + RMS_EPS)
    return (y * g) * (1.0 + sc) + sh


def _bdot(a, b):
    return jnp.dot(a.astype(BF16), b.astype(BF16), preferred_element_type=F32)


def _fdot(a, b):
    return jnp.dot(a, b, preferred_element_type=F32)


def _fdot_nt(a, b):
    return lax.dot_general(a, b, (((1,), (1,)), ((), ())), preferred_element_type=F32)


def _fdot_tn(a, b):
    return lax.dot_general(a, b, (((0,), (0,)), ((), ())), preferred_element_type=F32)


class _Layout:
    def __init__(self, batch, seq, tile=TOKEN_TILE):
        self.batch, self.seq = batch, seq
        self.n = batch * seq
        self.tile = min(tile, self.n)
        assert self.n % self.tile == 0
        self.n_tiles = self.n // self.tile
        if seq % self.tile == 0:
            self.rows = 1
            self.tiles_per_group = seq // self.tile
        else:
            assert self.tile % seq == 0
            self.rows = self.tile
            self.tiles_per_group = 1

    def expand(self, m):
        if self.rows == 1:
            return m[:, None, :]
        return jnp.repeat(m, self.seq, axis=0).reshape(self.n_tiles, self.tile, m.shape[-1])

    def mod_spec(self):
        tpg = self.tiles_per_group
        return pl.BlockSpec((1, self.rows, D_MODEL), lambda i: (i // tpg, 0, 0))

    def tok_spec(self, width=D_MODEL):
        return pl.BlockSpec((self.tile, width), lambda i: (i, 0))


def _full_spec(shape):
    nd = len(shape)
    return pl.BlockSpec(shape, lambda *_: (0,) * nd)


def _adaln_kernel(c_ref, w_ref, b_ref, o_ref):
    c = c_ref[...]
    cs = c * _sigmoid(c)
    o_ref[0] = _fdot(cs, w_ref[0]) + b_ref[0]


def _adaln(c, w_ada, b_ada):
    nb = c.shape[0]
    tn = 1536
    return pl.pallas_call(
        _adaln_kernel,
        grid=(DEPTH, 6 * D_MODEL // tn),
        in_specs=[
            pl.BlockSpec((nb, D_MODEL), lambda l, j: (0, 0)),
            pl.BlockSpec((1, D_MODEL, tn), lambda l, j: (l, 0, j)),
            pl.BlockSpec((1, 1, tn), lambda l, j: (l, 0, j)),
        ],
        out_specs=pl.BlockSpec((1, nb, tn), lambda l, j: (l, 0, j)),
        out_shape=jax.ShapeDtypeStruct((DEPTH, nb, 6 * D_MODEL), F32),
        compiler_params=_params("arbitrary", "arbitrary"),
        name="adaln",
    )(c, w_ada, b_ada.reshape(DEPTH, 1, 6 * D_MODEL))


def _norm_mod_kernel(x_ref, g_ref, sc_ref, sh_ref, o_ref):
    o_ref[...] = _rms_mod(x_ref[...], g_ref[...], sc_ref[0], sh_ref[0])


def _norm_mod(lay, x, g, sc, sh):
    return pl.pallas_call(
        _norm_mod_kernel,
        grid=(lay.n_tiles,),
        in_specs=[lay.tok_spec(), _full_spec((1, D_MODEL)), lay.mod_spec(), lay.mod_spec()],
        out_specs=lay.tok_spec(),
        out_shape=jax.ShapeDtypeStruct((lay.n, D_MODEL), F32),
        compiler_params=_params("arbitrary"),
        name="norm_mod",
    )(x, g.reshape(1, D_MODEL), sc, sh)


def _final_norm_kernel(x_ref, g_ref, o_ref):
    x = x_ref[...]
    o_ref[...] = x * lax.rsqrt(jnp.mean(x * x, axis=-1, keepdims=True) + RMS_EPS) * g_ref[...]


def _final_norm(lay, x, g):
    return pl.pallas_call(
        _final_norm_kernel,
        grid=(lay.n_tiles,),
        in_specs=[lay.tok_spec(), _full_spec((1, D_MODEL))],
        out_specs=lay.tok_spec(),
        out_shape=jax.ShapeDtypeStruct((lay.n, D_MODEL), F32),
        compiler_params=_params("arbitrary"),
        name="final_norm",
    )(x, g.reshape(1, D_MODEL))


def _conv_in_kernel(x_ref, g_ref, sc_ref, sh_ref, w_ref, bg_ref, u_ref):
    h = _rms_mod(x_ref[...], g_ref[...], sc_ref[0], sh_ref[0]).astype(BF16)
    d = D_MODEL
    bg_ref[...] = _fdot(h, w_ref[:, 0:d])
    u_ref[...] = _fdot(h, w_ref[:, d:2 * d]) * _fdot(h, w_ref[:, 2 * d:3 * d])


def _conv_in(lay, x, g, sc, sh, w_in):
    shp = jax.ShapeDtypeStruct((lay.n, D_MODEL), F32)
    return pl.pallas_call(
        _conv_in_kernel,
        grid=(lay.n_tiles,),
        in_specs=[lay.tok_spec(), _full_spec((1, D_MODEL)), lay.mod_spec(), lay.mod_spec(),
                  _full_spec((D_MODEL, 3 * D_MODEL))],
        out_specs=[lay.tok_spec(), lay.tok_spec()],
        out_shape=[shp, shp],
        compiler_params=_params("arbitrary"),
        name="conv_in",
    )(x, g.reshape(1, D_MODEL), sc, sh, w_in.astype(BF16))


def _conv_out_kernel(bg_ref, u0_ref, u1_ref, u2_ref, ck_ref, w_ref, x_ref, gt_ref, o_ref):
    conv = u0_ref[...] * ck_ref[0:1, :] + u1_ref[...] * ck_ref[1:2, :] + u2_ref[...] * ck_ref[2:3, :]
    y = _bdot(bg_ref[...] * conv, w_ref[...])
    o_ref[...] = x_ref[...] + gt_ref[0] * y


def _conv_out(lay, bg, u0, u1, u2, conv_k, w_out, x, gt):
    return pl.pallas_call(
        _conv_out_kernel,
        grid=(lay.n_tiles,),
        in_specs=[lay.tok_spec()] * 4 + [_full_spec((CONV_WIDTH, D_MODEL)), _full_spec((D_MODEL, D_MODEL)),
                                         lay.tok_spec(), lay.mod_spec()],
        out_specs=lay.tok_spec(),
        out_shape=jax.ShapeDtypeStruct((lay.n, D_MODEL), F32),
        compiler_params=_params("arbitrary"),
        name="conv_out",
    )(bg, u0, u1, u2, conv_k, w_out.astype(BF16), x, gt)


def _conv_fused_kernel(x_ref, g_ref, sc_ref, sh_ref, win_ref, ck_ref, wout_ref, gt_ref, st_ref,
                       o_ref, ns_ref, carry_ref, *, tiles_per_seq):
    x = x_ref[...]
    h = _rms_mod(x, g_ref[...], sc_ref[0], sh_ref[0]).astype(BF16)
    d = D_MODEL
    rows = x.shape[0]
    u = _fdot(h, win_ref[:, d:2 * d]) * _fdot(h, win_ref[:, 2 * d:3 * d])
    @pl.when(pl.program_id(0) == 0)
    def _():
        carry_ref[...] = jnp.zeros_like(carry_ref)

    first = pl.program_id(0) % tiles_per_seq == 0
    prev = jnp.where(first, st_ref[0], carry_ref[...])
    tail = u[rows - (CONV_WIDTH - 1):, :]
    carry_ref[...] = tail
    ns_ref[0] = tail
    row = lax.broadcasted_iota(jnp.int32, u.shape, 0)
    um1 = jnp.where(row == 0, prev[1:2, :], pltpu.roll(u, 1, axis=0))
    um2 = jnp.where(row == 0, prev[0:1, :], jnp.where(row == 1, prev[1:2, :], pltpu.roll(u, 2, axis=0)))
    conv = um2 * ck_ref[0:1, :] + um1 * ck_ref[1:2, :] + u * ck_ref[2:3, :]
    bg = _fdot(h, win_ref[:, 0:d])
    o_ref[...] = x + gt_ref[0] * _bdot(bg * conv, wout_ref[...])


def _conv_fused(lay, x, g, sc, sh, w_in, conv_k, w_out, gt, st):
    assert lay.rows == 1 and CONV_WIDTH == 3
    tps = lay.tiles_per_group
    st_spec = pl.BlockSpec((1, CONV_WIDTH - 1, D_MODEL), lambda i: (i // tps, 0, 0))
    return pl.pallas_call(
        functools.partial(_conv_fused_kernel, tiles_per_seq=tps),
        grid=(lay.n_tiles,),
        in_specs=[lay.tok_spec(), _full_spec((1, D_MODEL)), lay.mod_spec(), lay.mod_spec(),
                  _full_spec((D_MODEL, 3 * D_MODEL)), _full_spec((CONV_WIDTH, D_MODEL)),
                  _full_spec((D_MODEL, D_MODEL)), lay.mod_spec(), st_spec],
        out_specs=[lay.tok_spec(), st_spec],
        out_shape=[jax.ShapeDtypeStruct((lay.n, D_MODEL), F32),
                   jax.ShapeDtypeStruct((lay.batch, CONV_WIDTH - 1, D_MODEL), F32)],
        scratch_shapes=[pltpu.VMEM((CONV_WIDTH - 1, D_MODEL), F32)],
        compiler_params=_params("arbitrary"),
        name="conv_fused",
    )(x, g.reshape(1, D_MODEL), sc, sh, w_in.astype(BF16), conv_k, w_out.astype(BF16), gt, st)


def _mm_res_kernel(a_ref, w_ref, x_ref, gt_ref, o_ref):
    o_ref[...] = x_ref[...] + gt_ref[0] * _bdot(a_ref[...], w_ref[...])


def _mm_res(lay, a, w, x, gt):
    return pl.pallas_call(
        _mm_res_kernel,
        grid=(lay.n_tiles,),
        in_specs=[lay.tok_spec(), _full_spec((D_MODEL, D_MODEL)), lay.tok_spec(), lay.mod_spec()],
        out_specs=lay.tok_spec(),
        out_shape=jax.ShapeDtypeStruct((lay.n, D_MODEL), F32),
        compiler_params=_params("arbitrary"),
        name="mm_res",
    )(a, w.astype(BF16), x, gt)


def _rwkv_proj_kernel(*refs, has_vres, tiles_per_seq):
    refs = list(refs)
    fused = tiles_per_seq is not None
    if fused:
        x_ref, gm_ref, sc_ref, sh_ref, st_ref = refs[:5]
        del refs[:5]
        carry_ref = refs.pop()
        ns_ref = refs.pop()
    else:
        h_ref, hs_ref = refs[:2]
        del refs[:2]
    vf_ref = refs.pop(0) if has_vres else None
    mix_ref, vec_ref, wr_ref, wk_ref, wv_ref, w1_ref, w2_ref, a1_ref, a2_ref, g1_ref, g2_ref = refs[:11]
    del refs[:11]
    if has_vres:
        v1_ref, v2_ref = refs[:2]
        del refs[:2]
    r_ref, km_ref, kk_ref, v_ref, lw_ref, a_ref, g_ref = refs
    if fused:
        @pl.when(pl.program_id(0) == 0)
        def _():
            carry_ref[...] = jnp.zeros_like(carry_ref)

        h = _rms_mod(x_ref[...], gm_ref[...], sc_ref[0], sh_ref[0])
        rows = h.shape[0]
        first = pl.program_id(0) % tiles_per_seq == 0
        prev = jnp.where(first, st_ref[0], carry_ref[0:1, :])
        last = h[rows - 1:rows, :]
        carry_ref[0:1, :] = last
        ns_ref[0] = last
        row = lax.broadcasted_iota(jnp.int32, h.shape, 0)
        hs = jnp.where(row == 0, prev, pltpu.roll(h, 1, axis=0))
    else:
        h = h_ref[...]
        hs = hs_ref[...]
    xx = hs - h
    mixed = lambda i: (h + xx * mix_ref[i:i + 1, :]).astype(BF16)
    w0, a0, v0, k_k, k_a = (vec_ref[i:i + 1, :] for i in range(5))

    r_ref[...] = _fdot(mixed(0), wr_ref[...]).astype(r_ref.dtype)
    wl = w0 + _bdot(jnp.tanh(_fdot(mixed(1), w1_ref[...])), w2_ref[...])
    z = -wl
    wlog = -(jnp.maximum(z, 0.0) + jnp.log(1.0 + jnp.exp(-jnp.abs(z)))) - 0.5
    lw_ref[...] = -jnp.exp(wlog)
    k = _fdot(mixed(2), wk_ref[...])
    xv = mixed(3)
    v = _fdot(xv, wv_ref[...])
    if has_vres:
        v = v + (vf_ref[...] - v) * _sigmoid(v0 + _bdot(_fdot(xv, v1_ref[...]), v2_ref[...]))
    v_ref[...] = v
    a = _sigmoid(a0 + _bdot(_fdot(mixed(4), a1_ref[...]), a2_ref[...]))
    a_ref[...] = a.astype(a_ref.dtype)
    g_ref[...] = _bdot(_sigmoid(_fdot(mixed(5), g1_ref[...])), g2_ref[...]).astype(g_ref.dtype)
    kk_ref[...] = (k * k_k).astype(kk_ref.dtype)
    km_ref[...] = (k * (1.0 + (a - 1.0) * k_a)).astype(km_ref.dtype)


def _pad_cols(w):
    return jnp.pad(w, ((0, 0), (0, LORA_PAD - w.shape[1]))).astype(BF16)


def _pad_rows(w):
    return jnp.pad(w, ((0, LORA_PAD - w.shape[0]), (0, 0))).astype(BF16)


def _rwkv_proj(lay, front, v_first, mix, vecs, wr, wk, wv, w1, w2, a1, a2, g1, g2, v1, v2):
    has_vres = v_first is not None
    fused = len(front) == 5
    sq = _full_spec((D_MODEL, D_MODEL))
    down = _full_spec((D_MODEL, LORA_PAD))
    up = _full_spec((LORA_PAD, D_MODEL))
    tps = lay.tiles_per_group
    row_spec = pl.BlockSpec((1, 1, D_MODEL), lambda i: (i // tps, 0, 0))
    if fused:
        assert lay.rows == 1
        x, g, sc, sh, st = front
        in_specs = [lay.tok_spec(), _full_spec((1, D_MODEL)), lay.mod_spec(), lay.mod_spec(), row_spec]
        args = [x, g.reshape(1, D_MODEL), sc, sh, st[:, None, :]]
    else:
        in_specs = [lay.tok_spec(), lay.tok_spec()]
        args = list(front)
    in_specs += [lay.tok_spec()] if has_vres else []
    in_specs += [_full_spec((6, D_MODEL)), _full_spec((8, D_MODEL)), sq, sq, sq, down, up, down, up, down, up]
    args += [v_first] if has_vres else []
    args += [mix, vecs, wr.astype(BF16), wk.astype(BF16), wv.astype(BF16), _pad_cols(w1), _pad_rows(w2),
             _pad_cols(a1), _pad_rows(a2), _pad_cols(g1), _pad_rows(g2)]
    if has_vres:
        in_specs += [down, up]
        args += [_pad_cols(v1), _pad_rows(v2)]
    shp = jax.ShapeDtypeStruct((lay.n, D_MODEL), F32)
    outs = pl.pallas_call(
        functools.partial(_rwkv_proj_kernel, has_vres=has_vres, tiles_per_seq=tps if fused else None),
        grid=(lay.n_tiles,),
        in_specs=in_specs,
        out_specs=[lay.tok_spec()] * 7 + ([row_spec] if fused else []),
        out_shape=[jax.ShapeDtypeStruct(shp.shape, dt) for dt in RWKV_PROJ_DTYPES]
        + ([jax.ShapeDtypeStruct((lay.batch, 1, D_MODEL), F32)] if fused else []),
        scratch_shapes=[pltpu.VMEM((SUBLANES, D_MODEL), F32)] if fused else [],
        compiler_params=_params("arbitrary"),
        name="rwkv_proj",
    )(*args)
    return outs if not fused else (*outs[:7], outs[7][:, 0, :])


def _wkv_kernel(r_ref, km_ref, kk_ref, v_ref, lw_ref, a_ref, g_ref, rk_ref, lnw_ref, lnb_ref, s0_ref,
                y_ref, s_ref, st_ref, *, chunk, heads, seqs):
    t = pl.program_id(2)
    L = chunk

    pairs = heads // 2
    pw = 2 * HEAD_SIZE
    lo, hi = slice(0, HEAD_SIZE), slice(HEAD_SIZE, pw)
    chains = [(q, p) for q in range(seqs) for p in range(pairs)]

    @pl.when(t == 0)
    def _():
        st_ref[...] = jnp.zeros_like(st_ref)
        for n, (q, p) in enumerate(chains):
            st_ref[n, lo, lo] = s0_ref[q, 2 * p]
            st_ref[n, hi, hi] = s0_ref[q, 2 * p + 1]

    row = lax.broadcasted_iota(jnp.int32, (L, 2 * L), 0)
    col = lax.broadcasted_iota(jnp.int32, (L, 2 * L), 1)
    col = jnp.where(col >= L, col - L, col)
    strict2 = row > col
    incl2 = row >= col
    tri = incl2[:, :L].astype(F32)
    even = lax.broadcasted_iota(jnp.int32, (1, pw), 1) < HEAD_SIZE
    same_head = ((lax.broadcasted_iota(jnp.int32, (pw, pw), 0) < HEAD_SIZE)
                 == (lax.broadcasted_iota(jnp.int32, (pw, pw), 1) < HEAD_SIZE))
    ones_bd = same_head.astype(BF16)

    def head_sum(x):
        x_hi = x.astype(BF16)
        x_lo = (x - x_hi.astype(F32)).astype(BF16)
        return _fdot(x_hi, ones_bd) + _fdot(x_lo, ones_bd)

    pick = lambda x0, x1: jnp.where(even, x0, x1)

    each = lambda f, *cols: [f(*xs) for xs in zip(*cols)]
    sls = [slice(p * pw, (p + 1) * pw) for _, p in chains]
    r, km, kkr, v, lw, asig, g = ([ref[q, :, slice(p * pw, (p + 1) * pw)].astype(F32) for q, p in chains]
                                  for ref in (r_ref, km_ref, kk_ref, v_ref, lw_ref, a_ref, g_ref))
    s0 = [st_ref[n] for n in range(len(chains))]
    c = each(lambda x: _fdot(tri, x), lw)
    kk = each(lambda x: x * lax.rsqrt(jnp.maximum(head_sum(x * x), 1e-24)), kkr)
    b = each(lambda x, y: x * y, kk, asig)
    c_last = each(lambda x: x[L - 1:L, :], c)
    ginv = each(lambda x: jnp.exp(-x), c)
    lhs = each(lambda kk_, c_, lw_, r_: jnp.concatenate([-kk_ * jnp.exp(c_ - lw_), r_ * jnp.exp(c_)], axis=0),
               kk, c, lw, r)
    rhs = each(lambda b_, km_, gi: jnp.concatenate([b_ * gi, km_ * gi], axis=0), b, km, ginv)
    qk0 = each(lambda l_, r_: _fdot_nt(jnp.where(even, l_, 0.0), r_), lhs, rhs)
    qk1 = each(lambda l_, r_: _fdot_nt(jnp.where(even, 0.0, l_), r_), lhs, rhs)
    hs0 = each(_fdot_nt, lhs, s0)
    zv = each(lambda v_: jnp.concatenate([jnp.zeros_like(v_), v_], axis=0), v)
    m0 = each(lambda q: jnp.where(strict2, q[:L, :], 0.0), qk0)
    m1 = each(lambda q: jnp.where(strict2, q[:L, :], 0.0), qk1)
    u = each(lambda h_, a0, a1, z_: h_[:L] + pick(_fdot(a0, z_), _fdot(a1, z_)), hs0, m0, m1, zv)
    p0 = each(lambda m: m[:, :L], m0)
    p1 = each(lambda m: m[:, :L], m1)
    for it in range(int(math.log2(L))):
        if it > 0:
            p0 = each(lambda x: _fdot(x, x), p0)
            p1 = each(lambda x: _fdot(x, x), p1)
        u = each(lambda u_, a0, a1: u_ + pick(_fdot(a0, u_), _fdot(a1, u_)), u, p0, p1)
    uv = each(lambda u_, v_: jnp.concatenate([u_, v_], axis=0), u, v)
    bk = each(lambda cl, c_, b_, km_: jnp.concatenate([b_, km_], axis=0)
              * jnp.exp(cl - jnp.concatenate([c_, c_], axis=0)), c_last, c, b, km)
    s_new = each(lambda s_, cl, uv_, bk_: jnp.where(same_head, s_ * jnp.exp(cl) + _fdot_tn(uv_, bk_), 0.0),
                 s0, c_last, uv, bk)
    for n in range(len(chains)):
        st_ref[n] = s_new[n]
    y = each(lambda h_, q0, q1, uv_: h_[L:] + pick(_fdot(jnp.where(incl2, q0[L:, :], 0.0), uv_),
                                                   _fdot(jnp.where(incl2, q1[L:, :], 0.0), uv_)),
             hs0, qk0, qk1, uv)

    def finish(y_, r_, km_, v_, g_, sl):
        yc = y_ - head_sum(y_) * (1.0 / HEAD_SIZE)
        var = head_sum(yc * yc) * (1.0 / HEAD_SIZE)
        yn = yc * lax.rsqrt(var + GN_EPS) * lnw_ref[:, sl] + lnb_ref[:, sl]
        bonus = head_sum(r_ * km_ * rk_ref[:, sl]) * v_
        return (yn + bonus) * g_

    for (q, p), out in zip(chains, each(finish, y, r, km, v, g, sls)):
        y_ref[q, :, slice(p * pw, (p + 1) * pw)] = out

    @pl.when(t == pl.num_programs(2) - 1)
    def _():
        for n, (q, p) in enumerate(chains):
            s_ref[q, 2 * p] = st_ref[n, lo, lo]
            s_ref[q, 2 * p + 1] = st_ref[n, hi, hi]


def _wkv(batch, seq, chunk, heads, seqs, r, km, kk, v, lw, a, g, r_k, lnw, lnb, s0):
    n_t = seq // chunk
    width = heads * HEAD_SIZE
    tok = pl.BlockSpec((seqs, chunk, width), lambda b, h, t: (b, t, h))
    vec = pl.BlockSpec((1, width), lambda b, h, t: (0, h))
    st = pl.BlockSpec((seqs, heads, HEAD_SIZE, HEAD_SIZE), lambda b, h, t: (b, h, 0, 0))
    as3d = lambda x: x.reshape(batch, seq, D_MODEL)
    y, s_new = pl.pallas_call(
        functools.partial(_wkv_kernel, chunk=chunk, heads=heads, seqs=seqs),
        grid=(batch // seqs, N_HEADS // heads, n_t),
        in_specs=[tok] * 7 + [vec] * 3 + [st],
        out_specs=[tok, st],
        out_shape=[jax.ShapeDtypeStruct((batch, seq, D_MODEL), F32),
                   jax.ShapeDtypeStruct((batch, N_HEADS, HEAD_SIZE, HEAD_SIZE), F32)],
        scratch_shapes=[pltpu.VMEM((seqs * heads // 2, 2 * HEAD_SIZE, 2 * HEAD_SIZE), F32)],
        compiler_params=_params("arbitrary", "arbitrary", "arbitrary"),
        name="wkv",
    )(*(as3d(x) for x in (r, km, kk, v, lw, a, g)),
      r_k.reshape(1, D_MODEL), lnw.reshape(1, D_MODEL), lnb.reshape(1, D_MODEL), s0)
    return y.reshape(batch * seq, D_MODEL), s_new


def _col_reduce(x3, op):
    m = x3[0]
    for r in range(1, x3.shape[0]):
        m = op(m, x3[r])
    for shift in (4, 2, 1):
        m = op(m, pltpu.roll(m, shift, axis=0))
    return m


def _top16(s3, want_rank):
    work = s3
    rank = jnp.full(s3.shape, float(PEER_TOPK), F32) if want_rank else None
    vals = []
    for k in range(PEER_TOPK):
        m = _col_reduce(work, jnp.maximum)
        hit = work == m[None]
        if want_rank:
            rank = jnp.where(hit, float(k), rank)
        work = jnp.where(hit, -jnp.inf, work)
        vals.append(m)
    return vals, work, rank


def _sorting_network(n):
    pairs = []
    p = 1
    while p < n:
        k = p
        while k >= 1:
            for j in range(k % p, n - k, 2 * k):
                for i in range(min(k, n - j - k)):
                    if (i + j) // (2 * p) == (i + j + k) // (2 * p):
                        pairs.append((i + j, i + j + k))
            k //= 2
        p *= 2
    return pairs


def _top16_values(s3):
    cols = [s3[r] for r in range(s3.shape[0])]
    for i, j in _sorting_network(len(cols)):
        cols[i], cols[j] = jnp.maximum(cols[i], cols[j]), jnp.minimum(cols[i], cols[j])
    vals = []
    for k in range(PEER_TOPK):
        m = _col_reduce(cols[0][None], jnp.maximum)
        vals.append(m)
        if k + 1 < PEER_TOPK:
            hit = cols[0] == m
            for d in range(PEER_TOPK - 1 - k):
                cols[d] = jnp.where(hit, cols[d + 1], cols[d])
    return vals


def _rows_from(vals):
    row = lax.broadcasted_iota(jnp.int32, vals[0].shape, 0)
    out = vals[0]
    for s in range(1, SUBLANES):
        out = jnp.where(row == s, vals[s], out)
    return out


def _peer_select_chunk(s1, s2):
    n_keys, w = s1.shape
    s1 = s1.reshape(n_keys // SUBLANES, SUBLANES, w)
    s2 = s2.reshape(n_keys // SUBLANES, SUBLANES, w)
    v1 = _top16_values(s1)
    v2, _, rank2 = _top16(s2, True)
    v2_lo, v2_hi, v1_hi = _rows_from(v2[:SUBLANES]), _rows_from(v2[SUBLANES:]), _rows_from(v1[SUBLANES:])
    cand = jnp.stack([v1[0] + v2_lo, v1[0] + v2_hi] + [v1[a] + v2_lo for a in range(1, SUBLANES)]
                     + [v1_hi + v2[0]])
    work = cand
    tau = None
    for _ in range(PEER_TOPK):
        tau = _col_reduce(work, jnp.maximum)
        work = jnp.where(work == tau[None], -jnp.inf, work)
    top = v1[0] + v2[0]
    kept = cand >= tau[None]
    zsum = _col_reduce(jnp.where(kept, jnp.exp(cand - top[None]), 0.0), jnp.add)
    few = PEER_TOPK // 4
    cnt = jnp.zeros(s1.shape, F32)
    for b in range(few):
        cnt = cnt + jnp.where(s1 + v2[b][None] >= tau[None], 1.0, 0.0)
    upper = lax.broadcasted_iota(jnp.int32, cand.shape[1:], 0) >= few
    ones = lambda m: jnp.where(m, 1.0, 0.0)
    extra = [_col_reduce((ones(kept[0] & upper) + ones(kept[1]))[None], jnp.add),
             _col_reduce(ones(kept[2] & upper)[None], jnp.add),
             _col_reduce(ones(kept[3] & upper)[None], jnp.add)]
    for a, x in enumerate(extra):
        cnt = jnp.where(s1 == v1[a][None], cnt + x[None], cnt)
    cnt = jnp.where(s1 >= v1[PEER_TOPK - 1][None], cnt, 0.0)
    e1 = jnp.exp(s1 - v1[0][None])
    e2 = jnp.exp(s2 - v2[0][None]) / zsum[None]
    return tuple(t.reshape(n_keys, w) for t in (cnt, e1, rank2, e2))


def _peer_select_kernel(x_ref, g_ref, sc_ref, sh_ref, wq_ref, k1_ref, k2_ref,
                        ht_ref, c_ref, e1_ref, rk_ref, e2_ref, q_scr):
    h = _rms_mod(x_ref[...], g_ref[...], sc_ref[0], sh_ref[0])
    ht = h.T.astype(BF16)
    ht_ref[...] = ht
    q_scr[...] = _fdot(wq_ref[...], ht)
    half = D_QUERY // 2
    n_chunks = x_ref.shape[0] // PEER_LANE_CHUNK

    def head_body(hd, carry):
        q1 = q_scr[pl.ds(pl.multiple_of(hd * D_QUERY, D_QUERY), half), :]
        q2 = q_scr[pl.ds(pl.multiple_of(hd * D_QUERY + half, half), half), :]
        s1 = _bdot(k1_ref[hd], q1)
        s2 = _bdot(k2_ref[hd], q2)
        for ch in range(n_chunks):
            ls = slice(ch * PEER_LANE_CHUNK, (ch + 1) * PEER_LANE_CHUNK)
            cnt, e1, rank2, e2 = _peer_select_chunk(s1[:, ls], s2[:, ls])
            c_ref[hd, :, ls] = cnt
            e1_ref[hd, :, ls] = e1
            rk_ref[hd, :, ls] = rank2.astype(BF16)
            e2_ref[hd, :, ls] = e2.astype(BF16)
        return carry

    lax.fori_loop(0, PEER_HEADS, head_body, 0)


def _peer_select(lay, x, g, sc, sh, wq_t, k1, k2):
    tt = lay.tile
    sel_spec = pl.BlockSpec((PEER_HEADS, N_KEYS, tt), lambda i: (0, 0, i))
    sel_shape = lambda dt: jax.ShapeDtypeStruct((PEER_HEADS, N_KEYS, lay.n), dt)
    return pl.pallas_call(
        _peer_select_kernel,
        grid=(lay.n_tiles,),
        in_specs=[lay.tok_spec(), _full_spec((1, D_MODEL)), lay.mod_spec(), lay.mod_spec(),
                  _full_spec((PEER_HEADS * D_QUERY, D_MODEL)),
                  _full_spec((PEER_HEADS, N_KEYS, D_QUERY // 2)), _full_spec((PEER_HEADS, N_KEYS, D_QUERY // 2))],
        out_specs=[pl.BlockSpec((D_MODEL, tt), lambda i: (0, i))] + [sel_spec] * 4,
        out_shape=[jax.ShapeDtypeStruct((D_MODEL, lay.n), BF16),
                   sel_shape(F32), sel_shape(F32), sel_shape(BF16), sel_shape(BF16)],
        scratch_shapes=[pltpu.VMEM((PEER_HEADS * D_QUERY, tt), F32)],
        compiler_params=_params("arbitrary"),
        name="peer_select",
    )(x, g.reshape(1, D_MODEL), sc, sh, wq_t, k1, k2)


def _peer_dense_kernel(*refs):
    groups = PEER_EXPERT_TILE // PEER_GROUP
    ht_ref, c_ref, e1_ref, rk_ref, e2_ref = refs[:5]
    u_refs = refs[5:5 + groups]
    vt_refs = refs[5 + groups:5 + 2 * groups]
    x_ref, gt_ref, o_ref, acc_ref, coef_ref, z_ref = refs[5 + 2 * groups:]
    e = pl.program_id(1)
    sr = PEER_STEP_ROWS
    spg = PEER_GROUP // (N_KEYS * sr)
    piece = D_MODEL // spg
    last_slot = (groups - 1) % 2

    @pl.when(e == 0)
    def _():
        acc_ref[...] = jnp.zeros_like(acc_ref)

    def pre_act(step):
        g, ii = divmod(step, spg)
        return _fdot(u_refs[g][0, ii * sr * N_KEYS:(ii + 1) * sr * N_KEYS, :], ht_ref[...])

    ahead = PEER_Z_AHEAD
    for step in range(ahead):
        z_ref[step] = pre_act(step)
    for g in range(groups):
        slot = g % 2
        for ii in range(spg):
            step = g * spg + ii
            if step + ahead < groups * spg:
                z_ref[(step + ahead) % (ahead + 1)] = pre_act(step + ahead)
            if g > 0:
                ps = slice(ii * piece, (ii + 1) * piece)
                acc_ref[ps, :] += _fdot(vt_refs[g - 1][0, ps, :], coef_ref[1 - slot])
            for rr in range(sr):
                row = step * sr + rr
                gate = None
                for hd in range(PEER_HEADS):
                    cnt = c_ref[hd, row:row + 1, :].astype(BF16)
                    e1 = e1_ref[hd, row:row + 1, :].astype(BF16)
                    term = jnp.where(rk_ref[hd] < cnt, e2_ref[hd], jnp.zeros((), BF16)) * e1
                    gate = term if gate is None else gate + term
                z = z_ref[step % (ahead + 1), rr * N_KEYS:(rr + 1) * N_KEYS, :]
                act = 0.5 * z * (1.0 + lax.erf(z * (1.0 / math.sqrt(2.0))))
                lo = (ii * sr + rr) * N_KEYS
                coef_ref[slot, lo:lo + N_KEYS, :] = gate * act.astype(BF16)

    acc_ref[...] += _fdot(vt_refs[groups - 1][0], coef_ref[last_slot])

    @pl.when(e == pl.num_programs(1) - 1)
    def _():
        o_ref[...] = x_ref[...] + gt_ref[0] * acc_ref[...].T


def _peer_weight_parts(peer_u, peer_v):
    groups = PEER_EXPERT_TILE // PEER_GROUP
    tiles = N_EXPERTS // PEER_EXPERT_TILE
    u5 = peer_u.astype(BF16).reshape(DEPTH, tiles, groups, PEER_GROUP, D_MODEL)
    v5 = jnp.swapaxes(peer_v.astype(BF16).reshape(DEPTH, tiles, groups, PEER_GROUP, D_MODEL), 3, 4)
    u_parts = [u5[:, :, k] for k in range(groups)]
    vt_parts = [v5[:, :, k] for k in range(groups)]
    return u_parts, vt_parts


def _peer_dense(lay, ht, cnt, e1, rank2, e2, u_parts, vt_parts, x, gt):
    tt = lay.tile
    groups = PEER_EXPERT_TILE // PEER_GROUP
    rows_per_tile = PEER_EXPERT_TILE // N_KEYS
    tpg = lay.tiles_per_group
    row_spec = pl.BlockSpec((PEER_HEADS, rows_per_tile, tt), lambda i, e: (0, e, i))
    key_spec = pl.BlockSpec((PEER_HEADS, N_KEYS, tt), lambda i, e: (0, 0, i))
    u_spec = pl.BlockSpec((1, PEER_GROUP, D_MODEL), lambda i, e: (e, 0, 0))
    vt_spec = pl.BlockSpec((1, D_MODEL, PEER_GROUP), lambda i, e: (e, 0, 0))
    return pl.pallas_call(
        _peer_dense_kernel,
        grid=(lay.n_tiles, N_EXPERTS // PEER_EXPERT_TILE),
        in_specs=[pl.BlockSpec((D_MODEL, tt), lambda i, e: (0, i)), row_spec, row_spec, key_spec, key_spec]
        + [u_spec] * groups + [vt_spec] * groups + [
            pl.BlockSpec((tt, D_MODEL), lambda i, e: (i, 0)),
            pl.BlockSpec((1, lay.rows, D_MODEL), lambda i, e: (i // tpg, 0, 0))],
        out_specs=pl.BlockSpec((tt, D_MODEL), lambda i, e: (i, 0)),
        out_shape=jax.ShapeDtypeStruct((lay.n, D_MODEL), F32),
        scratch_shapes=[pltpu.VMEM((D_MODEL, tt), F32), pltpu.VMEM((2, PEER_GROUP, tt), BF16),
                        pltpu.VMEM((PEER_Z_AHEAD + 1, PEER_STEP_ROWS * N_KEYS, tt), F32)],
        compiler_params=_params("arbitrary", "arbitrary"),
        name="peer_dense",
    )(ht, cnt, e1, rank2, e2, *u_parts, *vt_parts, x, gt)


def _run(batch, seq, x, m_all, st_conv, st_shift, st_wkv, p):
    lay = _Layout(batch, seq)
    x = x.reshape(lay.n, D_MODEL)
    new_conv, new_shift, new_wkv = [], [], []
    v_first = None
    short = seq < WKV_CHUNK
    for i in range(DEPTH):
        sh_m, sc_m, gt_m, sh_f, sc_f, gt_f = (lay.expand(t) for t in jnp.split(m_all[i], 6, axis=-1))
        j = i // 2
        if i % 2 == 0 and lay.rows == 1:
            x, cbuf = _conv_fused(lay, x, p["g_mix"][i], sc_m, sh_m, p["conv_w_in"][j], p["conv_k"][j],
                                  p["conv_w_out"][j], gt_m, st_conv[j])
            new_conv.append(cbuf)
        elif i % 2 == 0:
            bg, u = _conv_in(lay, x, p["g_mix"][i], sc_m, sh_m, p["conv_w_in"][j])
            up = jnp.concatenate([st_conv[j], u.reshape(batch, seq, D_MODEL)], axis=1)
            u0 = up[:, 0:seq].reshape(lay.n, D_MODEL)
            u1 = up[:, 1:seq + 1].reshape(lay.n, D_MODEL)
            new_conv.append(up[:, -(CONV_WIDTH - 1):])
            x = _conv_out(lay, bg, u0, u1, u, p["conv_k"][j], p["conv_w_out"][j], x, gt_m)
        else:
            lay_p = _Layout(batch, seq, RWKV_PROJ_TILE)
            if lay_p.rows == 1:
                front = (x, p["g_mix"][i], lay_p.expand(m_all[i][:, D_MODEL:2 * D_MODEL]),
                         lay_p.expand(m_all[i][:, 0:D_MODEL]), st_shift[j])
            else:
                h = _norm_mod(lay, x, p["g_mix"][i], sc_m, sh_m)
                h3 = h.reshape(batch, seq, D_MODEL)
                hs = jnp.concatenate([st_shift[j][:, None, :], h3[:, :-1]], axis=1).reshape(lay.n, D_MODEL)
                new_shift.append(h3[:, -1])
                front = (h, hs)
            zero = jnp.zeros((D_MODEL,), F32)
            v0 = p["rwkv_v0"][j - 1] if j > 0 else zero
            vecs = jnp.stack([p["rwkv_w0"][j], p["rwkv_a0"][j], v0, p["rwkv_k_k"][j], p["rwkv_k_a"][j],
                              zero, zero, zero])
            v1 = p["rwkv_v1"][j - 1] if j > 0 else None
            v2 = p["rwkv_v2"][j - 1] if j > 0 else None
            r, km, kk, v, lw, a, g, *shift_out = _rwkv_proj(
                lay_p, front, v_first if j > 0 else None, p["rwkv_mix"][j], vecs,
                p["rwkv_wr"][j], p["rwkv_wk"][j], p["rwkv_wv"][j], p["rwkv_w1"][j], p["rwkv_w2"][j],
                p["rwkv_a1"][j], p["rwkv_a2"][j], p["rwkv_g1"][j], p["rwkv_g2"][j], v1, v2)
            new_shift.extend(shift_out)
            if j == 0:
                v_first = v
            scan_in = (r, km, kk, v, lw, a, g)
            if short:
                pad = lambda t: jnp.pad(t.astype(F32).reshape(batch, seq, D_MODEL),
                                        ((0, 0), (0, WKV_SHORT_CHUNK - seq), (0, 0))).reshape(-1, D_MODEL)
                scan_in = tuple(pad(t) for t in scan_in)
                yg, s_new = _wkv(batch, WKV_SHORT_CHUNK, WKV_SHORT_CHUNK, N_HEADS, WKV_SHORT_SEQS_PER_STEP, *scan_in,
                                 p["rwkv_r_k"][j], p["rwkv_lnw"][j], p["rwkv_lnb"][j], st_wkv[j])
                yg = yg.reshape(batch, WKV_SHORT_CHUNK, D_MODEL)[:, :seq].reshape(lay.n, D_MODEL)
            else:
                yg, s_new = _wkv(batch, seq, WKV_CHUNK, N_HEADS, WKV_SEQS_PER_STEP, *scan_in,
                                 p["rwkv_r_k"][j], p["rwkv_lnw"][j], p["rwkv_lnb"][j], st_wkv[j])
            new_wkv.append(s_new)
            x = _mm_res(lay, yg, p["rwkv_wo"][j], x, gt_m)
        ht, cnt, e1, rank2, e2 = _peer_select(lay, x, p["g_ffn"][i], sc_f, sh_f,
                                              p["wq_t"][i], p["k1_b"][i], p["k2_b"][i])
        x = _peer_dense(lay, ht, cnt, e1, rank2, e2, [t[i] for t in p["u_parts"]],
                        [t[i] for t in p["vt_parts"]], x, gt_f)
    y = _final_norm(lay, x, p["g_final"]).reshape(batch, seq, D_MODEL)
    return y, jnp.stack(new_conv), jnp.stack(new_shift), jnp.stack(new_wkv)


def kernel(x_prompt, x_sample, state_conv, state_shift, state_wkv, c_prompt, c_sample, w_ada, b_ada, g_mix, g_ffn, g_final, conv_w_in, conv_k, conv_w_out, rwkv_mix, rwkv_wr, rwkv_wk, rwkv_wv, rwkv_wo, rwkv_w0, rwkv_w1, rwkv_w2, rwkv_a0, rwkv_a1, rwkv_a2, rwkv_v0, rwkv_v1, rwkv_v2, rwkv_g1, rwkv_g2, rwkv_k_k, rwkv_k_a, rwkv_r_k, rwkv_lnw, rwkv_lnb, peer_wq, peer_k1, peer_k2, peer_u, peer_v):
    p = dict(
        g_mix=g_mix, g_ffn=g_ffn, g_final=g_final, conv_w_in=conv_w_in, conv_k=conv_k, conv_w_out=conv_w_out,
        rwkv_mix=rwkv_mix, rwkv_wr=rwkv_wr, rwkv_wk=rwkv_wk, rwkv_wv=rwkv_wv, rwkv_wo=rwkv_wo,
        rwkv_w0=rwkv_w0, rwkv_w1=rwkv_w1, rwkv_w2=rwkv_w2, rwkv_a0=rwkv_a0, rwkv_a1=rwkv_a1, rwkv_a2=rwkv_a2,
        rwkv_v0=rwkv_v0, rwkv_v1=rwkv_v1, rwkv_v2=rwkv_v2, rwkv_g1=rwkv_g1, rwkv_g2=rwkv_g2,
        rwkv_k_k=rwkv_k_k, rwkv_k_a=rwkv_k_a, rwkv_r_k=rwkv_r_k, rwkv_lnw=rwkv_lnw, rwkv_lnb=rwkv_lnb,
        wq_t=jnp.swapaxes(peer_wq, 1, 2).astype(BF16),
        k1_b=peer_k1.astype(BF16), k2_b=peer_k2.astype(BF16),
        **dict(zip(("u_parts", "vt_parts"), _peer_weight_parts(peer_u, peer_v))),
    )
    n_prompt, seq_prompt = x_prompt.shape[0], x_prompt.shape[1]
    n_sample, seq_sample = x_sample.shape[0], x_sample.shape[1]
    m_all = _adaln(jnp.concatenate([c_prompt, c_sample], axis=0), w_ada, b_ada)
    zeros = lambda *s: jnp.zeros(s, F32)
    n_conv, n_rwkv = (DEPTH + 1) // 2, DEPTH // 2
    y_p, p_conv, p_shift, p_wkv = _run(
        n_prompt, seq_prompt, x_prompt, m_all[:, :n_prompt],
        zeros(n_conv, n_prompt, CONV_WIDTH - 1, D_MODEL), zeros(n_rwkv, n_prompt, D_MODEL),
        zeros(n_rwkv, n_prompt, N_HEADS, HEAD_SIZE, HEAD_SIZE), p)
    y_s, s_conv, s_shift, s_wkv = _run(
        n_sample, seq_sample, x_sample, m_all[:, n_prompt:], state_conv, state_shift, state_wkv, p)
    return (y_p, y_s, p_conv, p_shift, p_wkv, s_conv, s_shift, s_wkv)
```

```python
import functools
import math

import jax
import jax.numpy as jnp
from jax import lax
from jax.experimental import pallas as pl
from jax.experimental.pallas import tpu as pltpu

D_MODEL = 1024
DEPTH = 4
CONV_WIDTH = 3
HEAD_SIZE = 64
N_HEADS = D_MODEL // HEAD_SIZE
N_KEYS = 128
N_EXPERTS = N_KEYS * N_KEYS
PEER_HEADS = 8
PEER_TOPK = 16
D_QUERY = 256
RMS_EPS = 1e-6
GN_EPS = 64e-5

SUBLANES = 8
LORA_PAD = 128
TOKEN_TILE = 512
RWKV_PROJ_TILE = 256
PEER_EXPERT_TILE = 2048
PEER_GROUP = 512
PEER_STEP_ROWS = 4
PEER_Z_AHEAD = 1
PEER_LANE_CHUNK = 256
WKV_CHUNK = 64
WKV_SHORT_CHUNK = 8
WKV_SEQS_PER_STEP = 2
WKV_SHORT_SEQS_PER_STEP = 4
VMEM_LIMIT = 56 * 1024 * 1024

F32 = jnp.float32
BF16 = jnp.bfloat16


def _params(*sem):
    return pltpu.CompilerParams(dimension_semantics=sem, vmem_limit_bytes=VMEM_LIMIT)


def _sigmoid(x):
    return 1.0 / (1.0 + jnp.exp(-x))


def _rms_mod(x, g, sc, sh):
    y = x * lax.rsqrt(jnp.mean(x * x, axis=-1, keepdims=True) + RMS_EPS)
    return (y * g) * (1.0 + sc) + sh


def _bdot(a, b):
    return jnp.dot(a.astype(BF16), b.astype(BF16), preferred_element_type=F32)


def _fdot(a, b):
    return jnp.dot(a, b, preferred_element_type=F32)


def _fdot_nt(a, b):
    return lax.dot_general(a, b, (((1,), (1,)), ((), ())), preferred_element_type=F32)


def _fdot_tn(a, b):
    return lax.dot_general(a, b, (((0,), (0,)), ((), ())), preferred_element_type=F32)


class _Layout:
    def __init__(self, batch, seq, tile=TOKEN_TILE):
        self.batch, self.seq = batch, seq
        self.n = batch * seq
        self.tile = min(tile, self.n)
        assert self.n % self.tile == 0
        self.n_tiles = self.n // self.tile
        if seq % self.tile == 0:
            self.rows = 1
            self.tiles_per_group = seq // self.tile
        else:
            assert self.tile % seq == 0
            self.rows = self.tile
            self.tiles_per_group = 1

    def expand(self, m):
        if self.rows == 1:
            return m[:, None, :]
        return jnp.repeat(m, self.seq, axis=0).reshape(self.n_tiles, self.tile, m.shape[-1])

    def mod_spec(self):
        tpg = self.tiles_per_group
        return pl.BlockSpec((1, self.rows, D_MODEL), lambda i: (i // tpg, 0, 0))

    def tok_spec(self, width=D_MODEL):
        return pl.BlockSpec((self.tile, width), lambda i: (i, 0))


def _full_spec(shape):
    nd = len(shape)
    return pl.BlockSpec(shape, lambda *_: (0,) * nd)


def _adaln_kernel(c_ref, w_ref, b_ref, o_ref):
    c = c_ref[...]
    cs = c * _sigmoid(c)
    o_ref[0] = _fdot(cs, w_ref[0]) + b_ref[0]


def _adaln(c, w_ada, b_ada):
    nb = c.shape[0]
    tn = 1536
    return pl.pallas_call(
        _adaln_kernel,
        grid=(DEPTH, 6 * D_MODEL // tn),
        in_specs=[
            pl.BlockSpec((nb, D_MODEL), lambda l, j: (0, 0)),
            pl.BlockSpec((1, D_MODEL, tn), lambda l, j: (l, 0, j)),
            pl.BlockSpec((1, 1, tn), lambda l, j: (l, 0, j)),
        ],
        out_specs=pl.BlockSpec((1, nb, tn), lambda l, j: (l, 0, j)),
        out_shape=jax.ShapeDtypeStruct((DEPTH, nb, 6 * D_MODEL), F32),
        compiler_params=_params("arbitrary", "arbitrary"),
        name="adaln",
    )(c, w_ada, b_ada.reshape(DEPTH, 1, 6 * D_MODEL))


def _norm_mod_kernel(x_ref, g_ref, sc_ref, sh_ref, o_ref):
    o_ref[...] = _rms_mod(x_ref[...], g_ref[...], sc_ref[0], sh_ref[0])


def _norm_mod(lay, x, g, sc, sh):
    return pl.pallas_call(
        _norm_mod_kernel,
        grid=(lay.n_tiles,),
        in_specs=[lay.tok_spec(), _full_spec((1, D_MODEL)), lay.mod_spec(), lay.mod_spec()],
        out_specs=lay.tok_spec(),
        out_shape=jax.ShapeDtypeStruct((lay.n, D_MODEL), F32),
        compiler_params=_params("arbitrary"),
        name="norm_mod",
    )(x, g.reshape(1, D_MODEL), sc, sh)


def _final_norm_kernel(x_ref, g_ref, o_ref):
    x = x_ref[...]
    o_ref[...] = x * lax.rsqrt(jnp.mean(x * x, axis=-1, keepdims=True) + RMS_EPS) * g_ref[...]


def _final_norm(lay, x, g):
    return pl.pallas_call(
        _final_norm_kernel,
        grid=(lay.n_tiles,),
        in_specs=[lay.tok_spec(), _full_spec((1, D_MODEL))],
        out_specs=lay.tok_spec(),
        out_shape=jax.ShapeDtypeStruct((lay.n, D_MODEL), F32),
        compiler_params=_params("arbitrary"),
        name="final_norm",
    )(x, g.reshape(1, D_MODEL))


def _conv_in_kernel(x_ref, g_ref, sc_ref, sh_ref, w_ref, bg_ref, u_ref):
    h = _rms_mod(x_ref[...], g_ref[...], sc_ref[0], sh_ref[0]).astype(BF16)
    d = D_MODEL
    bg_ref[...] = _fdot(h, w_ref[:, 0:d])
    u_ref[...] = _fdot(h, w_ref[:, d:2 * d]) * _fdot(h, w_ref[:, 2 * d:3 * d])


def _conv_in(lay, x, g, sc, sh, w_in):
    shp = jax.ShapeDtypeStruct((lay.n, D_MODEL), F32)
    return pl.pallas_call(
        _conv_in_kernel,
        grid=(lay.n_tiles,),
        in_specs=[lay.tok_spec(), _full_spec((1, D_MODEL)), lay.mod_spec(), lay.mod_spec(),
                  _full_spec((D_MODEL, 3 * D_MODEL))],
        out_specs=[lay.tok_spec(), lay.tok_spec()],
        out_shape=[shp, shp],
        compiler_params=_params("arbitrary"),
        name="conv_in",
    )(x, g.reshape(1, D_MODEL), sc, sh, w_in.astype(BF16))


def _conv_out_kernel(bg_ref, u0_ref, u1_ref, u2_ref, ck_ref, w_ref, x_ref, gt_ref, o_ref):
    conv = u0_ref[...] * ck_ref[0:1, :] + u1_ref[...] * ck_ref[1:2, :] + u2_ref[...] * ck_ref[2:3, :]
    y = _bdot(bg_ref[...] * conv, w_ref[...])
    o_ref[...] = x_ref[...] + gt_ref[0] * y


def _conv_out(lay, bg, u0, u1, u2, conv_k, w_out, x, gt):
    return pl.pallas_call(
        _conv_out_kernel,
        grid=(lay.n_tiles,),
        in_specs=[lay.tok_spec()] * 4 + [_full_spec((CONV_WIDTH, D_MODEL)), _full_spec((D_MODEL, D_MODEL)),
                                         lay.tok_spec(), lay.mod_spec()],
        out_specs=lay.tok_spec(),
        out_shape=jax.ShapeDtypeStruct((lay.n, D_MODEL), F32),
        compiler_params=_params("arbitrary"),
        name="conv_out",
    )(bg, u0, u1, u2, conv_k, w_out.astype(BF16), x, gt)


def _conv_fused_kernel(x_ref, g_ref, sc_ref, sh_ref, win_ref, ck_ref, wout_ref, gt_ref, st_ref,
                       o_ref, ns_ref, carry_ref, *, tiles_per_seq):
    x = x_ref[...]
    h = _rms_mod(x, g_ref[...], sc_ref[0], sh_ref[0]).astype(BF16)
    d = D_MODEL
    rows = x.shape[0]
    u = _fdot(h, win_ref[:, d:2 * d]) * _fdot(h, win_ref[:, 2 * d:3 * d])
    @pl.when(pl.program_id(0) == 0)
    def _():
        carry_ref[...] = jnp.zeros_like(carry_ref)

    first = pl.program_id(0) % tiles_per_seq == 0
    prev = jnp.where(first, st_ref[0], carry_ref[...])
    tail = u[rows - (CONV_WIDTH - 1):, :]
    carry_ref[...] = tail
    ns_ref[0] = tail
    row = lax.broadcasted_iota(jnp.int32, u.shape, 0)
    um1 = jnp.where(row == 0, prev[1:2, :], pltpu.roll(u, 1, axis=0))
    um2 = jnp.where(row == 0, prev[0:1, :], jnp.where(row == 1, prev[1:2, :], pltpu.roll(u, 2, axis=0)))
    conv = um2 * ck_ref[0:1, :] + um1 * ck_ref[1:2, :] + u * ck_ref[2:3, :]
    bg = _fdot(h, win_ref[:, 0:d])
    o_ref[...] = x + gt_ref[0] * _bdot(bg * conv, wout_ref[...])


def _conv_fused(lay, x, g, sc, sh, w_in, conv_k, w_out, gt, st):
    assert lay.rows == 1 and CONV_WIDTH == 3
    tps = lay.tiles_per_group
    st_spec = pl.BlockSpec((1, CONV_WIDTH - 1, D_MODEL), lambda i: (i // tps, 0, 0))
    return pl.pallas_call(
        functools.partial(_conv_fused_kernel, tiles_per_seq=tps),
        grid=(lay.n_tiles,),
        in_specs=[lay.tok_spec(), _full_spec((1, D_MODEL)), lay.mod_spec(), lay.mod_spec(),
                  _full_spec((D_MODEL, 3 * D_MODEL)), _full_spec((CONV_WIDTH, D_MODEL)),
                  _full_spec((D_MODEL, D_MODEL)), lay.mod_spec(), st_spec],
        out_specs=[lay.tok_spec(), st_spec],
        out_shape=[jax.ShapeDtypeStruct((lay.n, D_MODEL), F32),
                   jax.ShapeDtypeStruct((lay.batch, CONV_WIDTH - 1, D_MODEL), F32)],
        scratch_shapes=[pltpu.VMEM((CONV_WIDTH - 1, D_MODEL), F32)],
        compiler_params=_params("arbitrary"),
        name="conv_fused",
    )(x, g.reshape(1, D_MODEL), sc, sh, w_in.astype(BF16), conv_k, w_out.astype(BF16), gt, st)


def _mm_res_kernel(a_ref, w_ref, x_ref, gt_ref, o_ref):
    o_ref[...] = x_ref[...] + gt_ref[0] * _bdot(a_ref[...], w_ref[...])


def _mm_res(lay, a, w, x, gt):
    return pl.pallas_call(
        _mm_res_kernel,
        grid=(lay.n_tiles,),
        in_specs=[lay.tok_spec(), _full_spec((D_MODEL, D_MODEL)), lay.tok_spec(), lay.mod_spec()],
        out_specs=lay.tok_spec(),
        out_shape=jax.ShapeDtypeStruct((lay.n, D_MODEL), F32),
        compiler_params=_params("arbitrary"),
        name="mm_res",
    )(a, w.astype(BF16), x, gt)


def _rwkv_proj_kernel(*refs, has_vres, tiles_per_seq):
    refs = list(refs)
    fused = tiles_per_seq is not None
    if fused:
        x_ref, gm_ref, sc_ref, sh_ref, st_ref = refs[:5]
        del refs[:5]
        carry_ref = refs.pop()
        ns_ref = refs.pop()
    else:
        h_ref, hs_ref = refs[:2]
        del refs[:2]
    vf_ref = refs.pop(0) if has_vres else None
    mix_ref, vec_ref, wr_ref, wk_ref, wv_ref, w1_ref, w2_ref, a1_ref, a2_ref, g1_ref, g2_ref = refs[:11]
    del refs[:11]
    if has_vres:
        v1_ref, v2_ref = refs[:2]
        del refs[:2]
    r_ref, km_ref, kk_ref, v_ref, lw_ref, a_ref, g_ref = refs
    if fused:
        @pl.when(pl.program_id(0) == 0)
        def _():
            carry_ref[...] = jnp.zeros_like(carry_ref)

        h = _rms_mod(x_ref[...], gm_ref[...], sc_ref[0], sh_ref[0])
        rows = h.shape[0]
        first = pl.program_id(0) % tiles_per_seq == 0
        prev = jnp.where(first, st_ref[0], carry_ref[0:1, :])
        last = h[rows - 1:rows, :]
        carry_ref[0:1, :] = last
        ns_ref[0] = last
        row = lax.broadcasted_iota(jnp.int32, h.shape, 0)
        hs = jnp.where(row == 0, prev, pltpu.roll(h, 1, axis=0))
    else:
        h = h_ref[...]
        hs = hs_ref[...]
    xx = hs - h
    mixed = lambda i: (h + xx * mix_ref[i:i + 1, :]).astype(BF16)
    w0, a0, v0, k_k, k_a = (vec_ref[i:i + 1, :] for i in range(5))

    r_ref[...] = _fdot(mixed(0), wr_ref[...])
    wl = w0 + _bdot(jnp.tanh(_fdot(mixed(1), w1_ref[...])), w2_ref[...])
    z = -wl
    wlog = -(jnp.maximum(z, 0.0) + jnp.log(1.0 + jnp.exp(-jnp.abs(z)))) - 0.5
    lw_ref[...] = -jnp.exp(wlog)
    k = _fdot(mixed(2), wk_ref[...])
    xv = mixed(3)
    v = _fdot(xv, wv_ref[...])
    if has_vres:
        v = v + (vf_ref[...] - v) * _sigmoid(v0 + _bdot(_fdot(xv, v1_ref[...]), v2_ref[...]))
    v_ref[...] = v
    a = _sigmoid(a0 + _bdot(_fdot(mixed(4), a1_ref[...]), a2_ref[...]))
    a_ref[...] = a
    g_ref[...] = _bdot(_sigmoid(_fdot(mixed(5), g1_ref[...])), g2_ref[...])
    kk_ref[...] = k * k_k
    km_ref[...] = k * (1.0 + (a - 1.0) * k_a)


def _pad_cols(w):
    return jnp.pad(w, ((0, 0), (0, LORA_PAD - w.shape[1]))).astype(BF16)


def _pad_rows(w):
    return jnp.pad(w, ((0, LORA_PAD - w.shape[0]), (0, 0))).astype(BF16)


def _rwkv_proj(lay, front, v_first, mix, vecs, wr, wk, wv, w1, w2, a1, a2, g1, g2, v1, v2):
    has_vres = v_first is not None
    fused = len(front) == 5
    sq = _full_spec((D_MODEL, D_MODEL))
    down = _full_spec((D_MODEL, LORA_PAD))
    up = _full_spec((LORA_PAD, D_MODEL))
    tps = lay.tiles_per_group
    row_spec = pl.BlockSpec((1, 1, D_MODEL), lambda i: (i // tps, 0, 0))
    if fused:
        assert lay.rows == 1
        x, g, sc, sh, st = front
        in_specs = [lay.tok_spec(), _full_spec((1, D_MODEL)), lay.mod_spec(), lay.mod_spec(), row_spec]
        args = [x, g.reshape(1, D_MODEL), sc, sh, st[:, None, :]]
    else:
        in_specs = [lay.tok_spec(), lay.tok_spec()]
        args = list(front)
    in_specs += [lay.tok_spec()] if has_vres else []
    in_specs += [_full_spec((6, D_MODEL)), _full_spec((8, D_MODEL)), sq, sq, sq, down, up, down, up, down, up]
    args += [v_first] if has_vres else []
    args += [mix, vecs, wr.astype(BF16), wk.astype(BF16), wv.astype(BF16), _pad_cols(w1), _pad_rows(w2),
             _pad_cols(a1), _pad_rows(a2), _pad_cols(g1), _pad_rows(g2)]
    if has_vres:
        in_specs += [down, up]
        args += [_pad_cols(v1), _pad_rows(v2)]
    shp = jax.ShapeDtypeStruct((lay.n, D_MODEL), F32)
    outs = pl.pallas_call(
        functools.partial(_rwkv_proj_kernel, has_vres=has_vres, tiles_per_seq=tps if fused else None),
        grid=(lay.n_tiles,),
        in_specs=in_specs,
        out_specs=[lay.tok_spec()] * 7 + ([row_spec] if fused else []),
        out_shape=[shp] * 7 + ([jax.ShapeDtypeStruct((lay.batch, 1, D_MODEL), F32)] if fused else []),
        scratch_shapes=[pltpu.VMEM((SUBLANES, D_MODEL), F32)] if fused else [],
        compiler_params=_params("arbitrary"),
        name="rwkv_proj",
    )(*args)
    return outs if not fused else (*outs[:7], outs[7][:, 0, :])


def _wkv_kernel(r_ref, km_ref, kk_ref, v_ref, lw_ref, a_ref, g_ref, rk_ref, lnw_ref, lnb_ref, s0_ref,
                y_ref, s_ref, st_ref, *, chunk, heads, seqs):
    t = pl.program_id(2)
    L = chunk

    pairs = heads // 2
    pw = 2 * HEAD_SIZE
    lo, hi = slice(0, HEAD_SIZE), slice(HEAD_SIZE, pw)
    chains = [(q, p) for q in range(seqs) for p in range(pairs)]

    @pl.when(t == 0)
    def _():
        st_ref[...] = jnp.zeros_like(st_ref)
        for n, (q, p) in enumerate(chains):
            st_ref[n, lo, lo] = s0_ref[q, 2 * p]
            st_ref[n, hi, hi] = s0_ref[q, 2 * p + 1]

    row = lax.broadcasted_iota(jnp.int32, (L, 2 * L), 0)
    col = lax.broadcasted_iota(jnp.int32, (L, 2 * L), 1)
    col = jnp.where(col >= L, col - L, col)
    strict2 = row > col
    incl2 = row >= col
    tri = incl2[:, :L].astype(F32)
    even = lax.broadcasted_iota(jnp.int32, (1, pw), 1) < HEAD_SIZE
    same_head = ((lax.broadcasted_iota(jnp.int32, (pw, pw), 0) < HEAD_SIZE)
                 == (lax.broadcasted_iota(jnp.int32, (pw, pw), 1) < HEAD_SIZE))
    ones_bd = same_head.astype(BF16)

    def head_sum(x):
        return _fdot(x.astype(BF16), ones_bd)

    pick = lambda x0, x1: jnp.where(even, x0, x1)

    each = lambda f, *cols: [f(*xs) for xs in zip(*cols)]
    sls = [slice(p * pw, (p + 1) * pw) for _, p in chains]
    r, km, kkr, v, lw, asig, g = ([ref[q, :, slice(p * pw, (p + 1) * pw)] for q, p in chains]
                                  for ref in (r_ref, km_ref, kk_ref, v_ref, lw_ref, a_ref, g_ref))
    s0 = [st_ref[n] for n in range(len(chains))]
    c = each(lambda x: _fdot(tri, x), lw)
    kk = each(lambda x: x * lax.rsqrt(jnp.maximum(head_sum(x * x), 1e-24)), kkr)
    b = each(lambda x, y: x * y, kk, asig)
    c_last = each(lambda x: x[L - 1:L, :], c)
    ginv = each(lambda x: jnp.exp(-x), c)
    lhs = each(lambda kk_, c_, lw_, r_: jnp.concatenate([-kk_ * jnp.exp(c_ - lw_), r_ * jnp.exp(c_)], axis=0),
               kk, c, lw, r)
    rhs = each(lambda b_, km_, gi: jnp.concatenate([b_ * gi, km_ * gi], axis=0), b, km, ginv)
    qk0 = each(lambda l_, r_: _fdot_nt(jnp.where(even, l_, 0.0), r_), lhs, rhs)
    qk1 = each(lambda l_, r_: _fdot_nt(jnp.where(even, 0.0, l_), r_), lhs, rhs)
    hs0 = each(_fdot_nt, lhs, s0)
    zv = each(lambda v_: jnp.concatenate([jnp.zeros_like(v_), v_], axis=0), v)
    m0 = each(lambda q: jnp.where(strict2, q[:L, :], 0.0), qk0)
    m1 = each(lambda q: jnp.where(strict2, q[:L, :], 0.0), qk1)
    u = each(lambda h_, a0, a1, z_: h_[:L] + pick(_fdot(a0, z_), _fdot(a1, z_)), hs0, m0, m1, zv)
    p0 = each(lambda m: m[:, :L], m0)
    p1 = each(lambda m: m[:, :L], m1)
    for it in range(int(math.log2(L))):
        if it > 0:
            p0 = each(lambda x: _fdot(x, x), p0)
            p1 = each(lambda x: _fdot(x, x), p1)
        u = each(lambda u_, a0, a1: u_ + pick(_fdot(a0, u_), _fdot(a1, u_)), u, p0, p1)
    uv = each(lambda u_, v_: jnp.concatenate([u_, v_], axis=0), u, v)
    bk = each(lambda cl, c_, b_, km_: jnp.concatenate([b_, km_], axis=0)
              * jnp.exp(cl - jnp.concatenate([c_, c_], axis=0)), c_last, c, b, km)
    s_new = each(lambda s_, cl, uv_, bk_: jnp.where(same_head, s_ * jnp.exp(cl) + _fdot_tn(uv_, bk_), 0.0),
                 s0, c_last, uv, bk)
    for n in range(len(chains)):
        st_ref[n] = s_new[n]
    y = each(lambda h_, q0, q1, uv_: h_[L:] + pick(_fdot(jnp.where(incl2, q0[L:, :], 0.0), uv_),
                                                   _fdot(jnp.where(incl2, q1[L:, :], 0.0), uv_)),
             hs0, qk0, qk1, uv)

    def finish(y_, r_, km_, v_, g_, sl):
        yc = y_ - head_sum(y_) * (1.0 / HEAD_SIZE)
        var = head_sum(yc * yc) * (1.0 / HEAD_SIZE)
        yn = yc * lax.rsqrt(var + GN_EPS) * lnw_ref[:, sl] + lnb_ref[:, sl]
        bonus = head_sum(r_ * km_ * rk_ref[:, sl]) * v_
        return (yn + bonus) * g_

    for (q, p), out in zip(chains, each(finish, y, r, km, v, g, sls)):
        y_ref[q, :, slice(p * pw, (p + 1) * pw)] = out

    @pl.when(t == pl.num_programs(2) - 1)
    def _():
        for n, (q, p) in enumerate(chains):
            s_ref[q, 2 * p] = st_ref[n, lo, lo]
            s_ref[q, 2 * p + 1] = st_ref[n, hi, hi]


def _wkv(batch, seq, chunk, heads, seqs, r, km, kk, v, lw, a, g, r_k, lnw, lnb, s0):
    n_t = seq // chunk
    width = heads * HEAD_SIZE
    tok = pl.BlockSpec((seqs, chunk, width), lambda b, h, t: (b, t, h))
    vec = pl.BlockSpec((1, width), lambda b, h, t: (0, h))
    st = pl.BlockSpec((seqs, heads, HEAD_SIZE, HEAD_SIZE), lambda b, h, t: (b, h, 0, 0))
    as3d = lambda x: x.reshape(batch, seq, D_MODEL)
    y, s_new = pl.pallas_call(
        functools.partial(_wkv_kernel, chunk=chunk, heads=heads, seqs=seqs),
        grid=(batch // seqs, N_HEADS // heads, n_t),
        in_specs=[tok] * 7 + [vec] * 3 + [st],
        out_specs=[tok, st],
        out_shape=[jax.ShapeDtypeStruct((batch, seq, D_MODEL), F32),
                   jax.ShapeDtypeStruct((batch, N_HEADS, HEAD_SIZE, HEAD_SIZE), F32)],
        scratch_shapes=[pltpu.VMEM((seqs * heads // 2, 2 * HEAD_SIZE, 2 * HEAD_SIZE), F32)],
        compiler_params=_params("arbitrary", "arbitrary", "arbitrary"),
        name="wkv",
    )(*(as3d(x) for x in (r, km, kk, v, lw, a, g)),
      r_k.reshape(1, D_MODEL), lnw.reshape(1, D_MODEL), lnb.reshape(1, D_MODEL), s0)
    return y.reshape(batch * seq, D_MODEL), s_new


def _col_reduce(x3, op):
    m = x3[0]
    for r in range(1, x3.shape[0]):
        m = op(m, x3[r])
    for shift in (4, 2, 1):
        m = op(m, pltpu.roll(m, shift, axis=0))
    return m


def _top16(s3, want_rank):
    work = s3
    rank = jnp.full(s3.shape, float(PEER_TOPK), F32) if want_rank else None
    vals = []
    for k in range(PEER_TOPK):
        m = _col_reduce(work, jnp.maximum)
        hit = work == m[None]
        if want_rank:
            rank = jnp.where(hit, float(k), rank)
        work = jnp.where(hit, -jnp.inf, work)
        vals.append(m)
    return vals, work, rank


def _sorting_network(n):
    pairs = []
    p = 1
    while p < n:
        k = p
        while k >= 1:
            for j in range(k % p, n - k, 2 * k):
                for i in range(min(k, n - j - k)):
                    if (i + j) // (2 * p) == (i + j + k) // (2 * p):
                        pairs.append((i + j, i + j + k))
            k //= 2
        p *= 2
    return pairs


def _top16_values(s3):
    cols = [s3[r] for r in range(s3.shape[0])]
    for i, j in _sorting_network(len(cols)):
        cols[i], cols[j] = jnp.maximum(cols[i], cols[j]), jnp.minimum(cols[i], cols[j])
    vals = []
    for k in range(PEER_TOPK):
        m = _col_reduce(cols[0][None], jnp.maximum)
        vals.append(m)
        if k + 1 < PEER_TOPK:
            hit = cols[0] == m
            for d in range(PEER_TOPK - 1 - k):
                cols[d] = jnp.where(hit, cols[d + 1], cols[d])
    return vals


def _rows_from(vals):
    row = lax.broadcasted_iota(jnp.int32, vals[0].shape, 0)
    out = vals[0]
    for s in range(1, SUBLANES):
        out = jnp.where(row == s, vals[s], out)
    return out


def _peer_select_chunk(s1, s2):
    n_keys, w = s1.shape
    s1 = s1.reshape(n_keys // SUBLANES, SUBLANES, w)
    s2 = s2.reshape(n_keys // SUBLANES, SUBLANES, w)
    v1 = _top16_values(s1)
    v2, _, rank2 = _top16(s2, True)
    v2_lo, v2_hi, v1_hi = _rows_from(v2[:SUBLANES]), _rows_from(v2[SUBLANES:]), _rows_from(v1[SUBLANES:])
    cand = jnp.stack([v1[0] + v2_lo, v1[0] + v2_hi] + [v1[a] + v2_lo for a in range(1, SUBLANES)]
                     + [v1_hi + v2[0]])
    work = cand
    tau = None
    for _ in range(PEER_TOPK):
        tau = _col_reduce(work, jnp.maximum)
        work = jnp.where(work == tau[None], -jnp.inf, work)
    top = v1[0] + v2[0]
    kept = cand >= tau[None]
    zsum = _col_reduce(jnp.where(kept, jnp.exp(cand - top[None]), 0.0), jnp.add)
    few = PEER_TOPK // 4
    cnt = jnp.zeros(s1.shape, F32)
    for b in range(few):
        cnt = cnt + jnp.where(s1 + v2[b][None] >= tau[None], 1.0, 0.0)
    upper = lax.broadcasted_iota(jnp.int32, cand.shape[1:], 0) >= few
    ones = lambda m: jnp.where(m, 1.0, 0.0)
    extra = [_col_reduce((ones(kept[0] & upper) + ones(kept[1]))[None], jnp.add),
             _col_reduce(ones(kept[2] & upper)[None], jnp.add),
             _col_reduce(ones(kept[3] & upper)[None], jnp.add)]
    for a, x in enumerate(extra):
        cnt = jnp.where(s1 == v1[a][None], cnt + x[None], cnt)
    cnt = jnp.where(s1 >= v1[PEER_TOPK - 1][None], cnt, 0.0)
    e1 = jnp.exp(s1 - v1[0][None])
    e2 = jnp.exp(s2 - v2[0][None]) / zsum[None]
    return tuple(t.reshape(n_keys, w) for t in (cnt, e1, rank2, e2))


def _peer_select_kernel(x_ref, g_ref, sc_ref, sh_ref, wq_ref, k1_ref, k2_ref,
                        ht_ref, c_ref, e1_ref, rk_ref, e2_ref, q_scr):
    h = _rms_mod(x_ref[...], g_ref[...], sc_ref[0], sh_ref[0])
    ht = h.T.astype(BF16)
    ht_ref[...] = ht
    q_scr[...] = _fdot(wq_ref[...], ht)
    half = D_QUERY // 2
    n_chunks = x_ref.shape[0] // PEER_LANE_CHUNK

    def head_body(hd, carry):
        q1 = q_scr[pl.ds(pl.multiple_of(hd * D_QUERY, D_QUERY), half), :]
        q2 = q_scr[pl.ds(pl.multiple_of(hd * D_QUERY + half, half), half), :]
        s1 = _bdot(k1_ref[hd], q1)
        s2 = _bdot(k2_ref[hd], q2)
        for ch in range(n_chunks):
            ls = slice(ch * PEER_LANE_CHUNK, (ch + 1) * PEER_LANE_CHUNK)
            cnt, e1, rank2, e2 = _peer_select_chunk(s1[:, ls], s2[:, ls])
            c_ref[hd, :, ls] = cnt
            e1_ref[hd, :, ls] = e1
            rk_ref[hd, :, ls] = rank2.astype(BF16)
            e2_ref[hd, :, ls] = e2.astype(BF16)
        return carry

    lax.fori_loop(0, PEER_HEADS, head_body, 0)


def _peer_select(lay, x, g, sc, sh, wq_t, k1, k2):
    tt = lay.tile
    sel_spec = pl.BlockSpec((PEER_HEADS, N_KEYS, tt), lambda i: (0, 0, i))
    sel_shape = lambda dt: jax.ShapeDtypeStruct((PEER_HEADS, N_KEYS, lay.n), dt)
    return pl.pallas_call(
        _peer_select_kernel,
        grid=(lay.n_tiles,),
        in_specs=[lay.tok_spec(), _full_spec((1, D_MODEL)), lay.mod_spec(), lay.mod_spec(),
                  _full_spec((PEER_HEADS * D_QUERY, D_MODEL)),
                  _full_spec((PEER_HEADS, N_KEYS, D_QUERY // 2)), _full_spec((PEER_HEADS, N_KEYS, D_QUERY // 2))],
        out_specs=[pl.BlockSpec((D_MODEL, tt), lambda i: (0, i))] + [sel_spec] * 4,
        out_shape=[jax.ShapeDtypeStruct((D_MODEL, lay.n), BF16),
                   sel_shape(F32), sel_shape(F32), sel_shape(BF16), sel_shape(BF16)],
        scratch_shapes=[pltpu.VMEM((PEER_HEADS * D_QUERY, tt), F32)],
        compiler_params=_params("arbitrary"),
        name="peer_select",
    )(x, g.reshape(1, D_MODEL), sc, sh, wq_t, k1, k2)


def _peer_dense_kernel(*refs):
    groups = PEER_EXPERT_TILE // PEER_GROUP
    ht_ref, c_ref, e1_ref, rk_ref, e2_ref = refs[:5]
    u_refs = refs[5:5 + groups]
    vt_refs = refs[5 + groups:5 + 2 * groups]
    x_ref, gt_ref, o_ref, acc_ref, coef_ref, z_ref = refs[5 + 2 * groups:]
    e = pl.program_id(1)
    sr = PEER_STEP_ROWS
    spg = PEER_GROUP // (N_KEYS * sr)
    piece = D_MODEL // spg
    last_slot = (groups - 1) % 2

    @pl.when(e == 0)
    def _():
        acc_ref[...] = jnp.zeros_like(acc_ref)

    def pre_act(step):
        g, ii = divmod(step, spg)
        return _fdot(u_refs[g][0, ii * sr * N_KEYS:(ii + 1) * sr * N_KEYS, :], ht_ref[...])

    ahead = PEER_Z_AHEAD
    for step in range(ahead):
        z_ref[step] = pre_act(step)
    for g in range(groups):
        slot = g % 2
        for ii in range(spg):
            step = g * spg + ii
            if step + ahead < groups * spg:
                z_ref[(step + ahead) % (ahead + 1)] = pre_act(step + ahead)
            if g > 0:
                ps = slice(ii * piece, (ii + 1) * piece)
                acc_ref[ps, :] += _fdot(vt_refs[g - 1][0, ps, :], coef_ref[1 - slot])
            for rr in range(sr):
                row = step * sr + rr
                gate = None
                for hd in range(PEER_HEADS):
                    cnt = c_ref[hd, row:row + 1, :].astype(BF16)
                    e1 = e1_ref[hd, row:row + 1, :].astype(BF16)
                    term = jnp.where(rk_ref[hd] < cnt, e2_ref[hd], jnp.zeros((), BF16)) * e1
                    gate = term if gate is None else gate + term
                z = z_ref[step % (ahead + 1), rr * N_KEYS:(rr + 1) * N_KEYS, :]
                act = 0.5 * z * (1.0 + lax.erf(z * (1.0 / math.sqrt(2.0))))
                lo = (ii * sr + rr) * N_KEYS
                coef_ref[slot, lo:lo + N_KEYS, :] = gate * act.astype(BF16)

    acc_ref[...] += _fdot(vt_refs[groups - 1][0], coef_ref[last_slot])

    @pl.when(e == pl.num_programs(1) - 1)
    def _():
        o_ref[...] = x_ref[...] + gt_ref[0] * acc_ref[...].T


def _peer_weight_parts(peer_u, peer_v):
    groups = PEER_EXPERT_TILE // PEER_GROUP
    tiles = N_EXPERTS // PEER_EXPERT_TILE
    u5 = peer_u.astype(BF16).reshape(DEPTH, tiles, groups, PEER_GROUP, D_MODEL)
    v5 = jnp.swapaxes(peer_v.astype(BF16).reshape(DEPTH, tiles, groups, PEER_GROUP, D_MODEL), 3, 4)
    u_parts = [u5[:, :, k] for k in range(groups)]
    vt_parts = [v5[:, :, k] for k in range(groups)]
    return u_parts, vt_parts


def _peer_dense(lay, ht, cnt, e1, rank2, e2, u_parts, vt_parts, x, gt):
    tt = lay.tile
    groups = PEER_EXPERT_TILE // PEER_GROUP
    rows_per_tile = PEER_EXPERT_TILE // N_KEYS
    tpg = lay.tiles_per_group
    row_spec = pl.BlockSpec((PEER_HEADS, rows_per_tile, tt), lambda i, e: (0, e, i))
    key_spec = pl.BlockSpec((PEER_HEADS, N_KEYS, tt), lambda i, e: (0, 0, i))
    u_spec = pl.BlockSpec((1, PEER_GROUP, D_MODEL), lambda i, e: (e, 0, 0))
    vt_spec = pl.BlockSpec((1, D_MODEL, PEER_GROUP), lambda i, e: (e, 0, 0))
    return pl.pallas_call(
        _peer_dense_kernel,
        grid=(lay.n_tiles, N_EXPERTS // PEER_EXPERT_TILE),
        in_specs=[pl.BlockSpec((D_MODEL, tt), lambda i, e: (0, i)), row_spec, row_spec, key_spec, key_spec]
        + [u_spec] * groups + [vt_spec] * groups + [
            pl.BlockSpec((tt, D_MODEL), lambda i, e: (i, 0)),
            pl.BlockSpec((1, lay.rows, D_MODEL), lambda i, e: (i // tpg, 0, 0))],
        out_specs=pl.BlockSpec((tt, D_MODEL), lambda i, e: (i, 0)),
        out_shape=jax.ShapeDtypeStruct((lay.n, D_MODEL), F32),
        scratch_shapes=[pltpu.VMEM((D_MODEL, tt), F32), pltpu.VMEM((2, PEER_GROUP, tt), BF16),
                        pltpu.VMEM((PEER_Z_AHEAD + 1, PEER_STEP_ROWS * N_KEYS, tt), F32)],
        compiler_params=_params("arbitrary", "arbitrary"),
        name="peer_dense",
    )(ht, cnt, e1, rank2, e2, *u_parts, *vt_parts, x, gt)


def _run(batch, seq, x, m_all, st_conv, st_shift, st_wkv, p):
    lay = _Layout(batch, seq)
    x = x.reshape(lay.n, D_MODEL)
    new_conv, new_shift, new_wkv = [], [], []
    v_first = None
    short = seq < WKV_CHUNK
    for i in range(DEPTH):
        sh_m, sc_m, gt_m, sh_f, sc_f, gt_f = (lay.expand(t) for t in jnp.split(m_all[i], 6, axis=-1))
        j = i // 2
        if i % 2 == 0 and lay.rows == 1:
            x, cbuf = _conv_fused(lay, x, p["g_mix"][i], sc_m, sh_m, p["conv_w_in"][j], p["conv_k"][j],
                                  p["conv_w_out"][j], gt_m, st_conv[j])
            new_conv.append(cbuf)
        elif i % 2 == 0:
            bg, u = _conv_in(lay, x, p["g_mix"][i], sc_m, sh_m, p["conv_w_in"][j])
            up = jnp.concatenate([st_conv[j], u.reshape(batch, seq, D_MODEL)], axis=1)
            u0 = up[:, 0:seq].reshape(lay.n, D_MODEL)
            u1 = up[:, 1:seq + 1].reshape(lay.n, D_MODEL)
            new_conv.append(up[:, -(CONV_WIDTH - 1):])
            x = _conv_out(lay, bg, u0, u1, u, p["conv_k"][j], p["conv_w_out"][j], x, gt_m)
        else:
            lay_p = _Layout(batch, seq, RWKV_PROJ_TILE)
            if lay_p.rows == 1:
                front = (x, p["g_mix"][i], lay_p.expand(m_all[i][:, D_MODEL:2 * D_MODEL]),
                         lay_p.expand(m_all[i][:, 0:D_MODEL]), st_shift[j])
            else:
                h = _norm_mod(lay, x, p["g_mix"][i], sc_m, sh_m)
                h3 = h.reshape(batch, seq, D_MODEL)
                hs = jnp.concatenate([st_shift[j][:, None, :], h3[:, :-1]], axis=1).reshape(lay.n, D_MODEL)
                new_shift.append(h3[:, -1])
                front = (h, hs)
            zero = jnp.zeros((D_MODEL,), F32)
            v0 = p["rwkv_v0"][j - 1] if j > 0 else zero
            vecs = jnp.stack([p["rwkv_w0"][j], p["rwkv_a0"][j], v0, p["rwkv_k_k"][j], p["rwkv_k_a"][j],
                              zero, zero, zero])
            v1 = p["rwkv_v1"][j - 1] if j > 0 else None
            v2 = p["rwkv_v2"][j - 1] if j > 0 else None
            r, km, kk, v, lw, a, g, *shift_out = _rwkv_proj(
                lay_p, front, v_first if j > 0 else None, p["rwkv_mix"][j], vecs,
                p["rwkv_wr"][j], p["rwkv_wk"][j], p["rwkv_wv"][j], p["rwkv_w1"][j], p["rwkv_w2"][j],
                p["rwkv_a1"][j], p["rwkv_a2"][j], p["rwkv_g1"][j], p["rwkv_g2"][j], v1, v2)
            new_shift.extend(shift_out)
            if j == 0:
                v_first = v
            scan_in = (r, km, kk, v, lw, a, g)
            if short:
                pad = lambda t: jnp.pad(t.reshape(batch, seq, D_MODEL),
                                        ((0, 0), (0, WKV_SHORT_CHUNK - seq), (0, 0))).reshape(-1, D_MODEL)
                scan_in = tuple(pad(t) for t in scan_in)
                yg, s_new = _wkv(batch, WKV_SHORT_CHUNK, WKV_SHORT_CHUNK, N_HEADS, WKV_SHORT_SEQS_PER_STEP, *scan_in,
                                 p["rwkv_r_k"][j], p["rwkv_lnw"][j], p["rwkv_lnb"][j], st_wkv[j])
                yg = yg.reshape(batch, WKV_SHORT_CHUNK, D_MODEL)[:, :seq].reshape(lay.n, D_MODEL)
            else:
                yg, s_new = _wkv(batch, seq, WKV_CHUNK, N_HEADS, WKV_SEQS_PER_STEP, *scan_in,
                                 p["rwkv_r_k"][j], p["rwkv_lnw"][j], p["rwkv_lnb"][j], st_wkv[j])
            new_wkv.append(s_new)
            x = _mm_res(lay, yg, p["rwkv_wo"][j], x, gt_m)
        ht, cnt, e1, rank2, e2 = _peer_select(lay, x, p["g_ffn"][i], sc_f, sh_f,
                                              p["wq_t"][i], p["k1_b"][i], p["k2_b"][i])
        x = _peer_dense(lay, ht, cnt, e1, rank2, e2, [t[i] for t in p["u_parts"]],
                        [t[i] for t in p["vt_parts"]], x, gt_f)
    y = _final_norm(lay, x, p["g_final"]).reshape(batch, seq, D_MODEL)
    return y, jnp.stack(new_conv), jnp.stack(new_shift), jnp.stack(new_wkv)


def kernel(x_prompt, x_sample, state_conv, state_shift, state_wkv, c_prompt, c_sample, w_ada, b_ada, g_mix, g_ffn, g_final, conv_w_in, conv_k, conv_w_out, rwkv_mix, rwkv_wr, rwkv_wk, rwkv_wv, rwkv_wo, rwkv_w0, rwkv_w1, rwkv_w2, rwkv_a0, rwkv_a1, rwkv_a2, rwkv_v0, rwkv_v1, rwkv_v2, rwkv_g1, rwkv_g2, rwkv_k_k, rwkv_k_a, rwkv_r_k, rwkv_lnw, rwkv_lnb, peer_wq, peer_k1, peer_k2, peer_u, peer_v):
    p = dict(
        g_mix=g_mix, g_ffn=g_ffn, g_final=g_final, conv_w_in=conv_w_in, conv_k=conv_k, conv_w_out=conv_w_out,
        rwkv_mix=rwkv_mix, rwkv_wr=rwkv_wr, rwkv_wk=rwkv_wk, rwkv_wv=rwkv_wv, rwkv_wo=rwkv_wo,
        rwkv_w0=rwkv_w0, rwkv_w1=rwkv_w1, rwkv_w2=rwkv_w2, rwkv_a0=rwkv_a0, rwkv_a1=rwkv_a1, rwkv_a2=rwkv_a2,
        rwkv_v0=rwkv_v0, rwkv_v1=rwkv_v1, rwkv_v2=rwkv_v2, rwkv_g1=rwkv_g1, rwkv_g2=rwkv_g2,
        rwkv_k_k=rwkv_k_k, rwkv_k_a=rwkv_k_a, rwkv_r_k=rwkv_r_k, rwkv_lnw=rwkv_lnw, rwkv_lnb=rwkv_lnb,
        wq_t=jnp.swapaxes(peer_wq, 1, 2).astype(BF16),
        k1_b=peer_k1.astype(BF16), k2_b=peer_k2.astype(BF16),
        **dict(zip(("u_parts", "vt_parts"), _peer_weight_parts(peer_u, peer_v))),
    )
    n_prompt, seq_prompt = x_prompt.shape[0], x_prompt.shape[1]
    n_sample, seq_sample = x_sample.shape[0], x_sample.shape[1]
    m_all = _adaln(jnp.concatenate([c_prompt, c_sample], axis=0), w_ada, b_ada)
    zeros = lambda *s: jnp.zeros(s, F32)
    n_conv, n_rwkv = (DEPTH + 1) // 2, DEPTH // 2
    y_p, p_conv, p_shift, p_wkv = _run(
        n_prompt, seq_prompt, x_prompt, m_all[:, :n_prompt],
        zeros(n_conv, n_prompt, CONV_WIDTH - 1, D_MODEL), zeros(n_rwkv, n_prompt, D_MODEL),
        zeros(n_rwkv, n_prompt, N_HEADS, HEAD_SIZE, HEAD_SIZE), p)
    y_s, s_conv, s_shift, s_wkv = _run(
        n_sample, seq_sample, x_sample, m_all[:, n_prompt:], state_conv, state_shift, state_wkv, p)
    return (y_p, y_s, p_conv, p_shift, p_wkv, s_conv, s_shift, s_wkv)
```

```python
import functools
import math

import jax
import jax.numpy as jnp
from jax import lax
from jax.experimental import pallas as pl
from jax.experimental.pallas import tpu as pltpu

D_MODEL = 1024
DEPTH = 4
CONV_WIDTH = 3
HEAD_SIZE = 64
N_HEADS = D_MODEL // HEAD_SIZE
N_KEYS = 128
N_EXPERTS = N_KEYS * N_KEYS
PEER_HEADS = 8
PEER_TOPK = 16
D_QUERY = 256
RMS_EPS = 1e-6
GN_EPS = 64e-5

SUBLANES = 8
LORA_PAD = 128
TOKEN_TILE = 512
RWKV_PROJ_TILE = 512
PEER_EXPERT_TILE = 2048
PEER_GROUP = 512
PEER_STEP_ROWS = 4
PEER_Z_AHEAD = 1
PEER_LANE_CHUNK = 256
WKV_CHUNK = 64
WKV_SHORT_CHUNK = 8
WKV_SEQS_PER_STEP = 2
WKV_SHORT_SEQS_PER_STEP = 4
VMEM_LIMIT = 56 * 1024 * 1024

F32 = jnp.float32
BF16 = jnp.bfloat16


def _params(*sem):
    return pltpu.CompilerParams(dimension_semantics=sem, vmem_limit_bytes=VMEM_LIMIT)


def _sigmoid(x):
    return 1.0 / (1.0 + jnp.exp(-x))


def _rms_mod(x, g, sc, sh):
    y = x * lax.rsqrt(jnp.mean(x * x, axis=-1, keepdims=True) + RMS_EPS)
    return (y * g) * (1.0 + sc) + sh


def _bdot(a, b):
    return jnp.dot(a.astype(BF16), b.astype(BF16), preferred_element_type=F32)


def _fdot(a, b):
    return jnp.dot(a, b, preferred_element_type=F32)


def _fdot_nt(a, b):
    return lax.dot_general(a, b, (((1,), (1,)), ((), ())), preferred_element_type=F32)


def _fdot_tn(a, b):
    return lax.dot_general(a, b, (((0,), (0,)), ((), ())), preferred_element_type=F32)


class _Layout:
    def __init__(self, batch, seq, tile=TOKEN_TILE):
        self.batch, self.seq = batch, seq
        self.n = batch * seq
        self.tile = min(tile, self.n)
        assert self.n % self.tile == 0
        self.n_tiles = self.n // self.tile
        if seq % self.tile == 0:
            self.rows = 1
            self.tiles_per_group = seq // self.tile
        else:
            assert self.tile % seq == 0
            self.rows = self.tile
            self.tiles_per_group = 1

    def expand(self, m):
        if self.rows == 1:
            return m[:, None, :]
        return jnp.repeat(m, self.seq, axis=0).reshape(self.n_tiles, self.tile, m.shape[-1])

    def mod_spec(self):
        tpg = self.tiles_per_group
        return pl.BlockSpec((1, self.rows, D_MODEL), lambda i: (i // tpg, 0, 0))

    def tok_spec(self, width=D_MODEL):
        return pl.BlockSpec((self.tile, width), lambda i: (i, 0))


def _full_spec(shape, single_buffer=False):
    nd = len(shape)
    if single_buffer:
        return pl.BlockSpec(shape, lambda *_: (0,) * nd, pipeline_mode=pl.Buffered(1))
    return pl.BlockSpec(shape, lambda *_: (0,) * nd)


def _adaln_kernel(c_ref, w_ref, b_ref, o_ref):
    c = c_ref[...]
    cs = c * _sigmoid(c)
    o_ref[0] = _fdot(cs, w_ref[0]) + b_ref[0]


def _adaln(c, w_ada, b_ada):
    nb = c.shape[0]
    tn = 1536
    return pl.pallas_call(
        _adaln_kernel,
        grid=(DEPTH, 6 * D_MODEL // tn),
        in_specs=[
            pl.BlockSpec((nb, D_MODEL), lambda l, j: (0, 0)),
            pl.BlockSpec((1, D_MODEL, tn), lambda l, j: (l, 0, j)),
            pl.BlockSpec((1, 1, tn), lambda l, j: (l, 0, j)),
        ],
        out_specs=pl.BlockSpec((1, nb, tn), lambda l, j: (l, 0, j)),
        out_shape=jax.ShapeDtypeStruct((DEPTH, nb, 6 * D_MODEL), F32),
        compiler_params=_params("arbitrary", "arbitrary"),
        name="adaln",
    )(c, w_ada, b_ada.reshape(DEPTH, 1, 6 * D_MODEL))


def _norm_mod_kernel(x_ref, g_ref, sc_ref, sh_ref, o_ref):
    o_ref[...] = _rms_mod(x_ref[...], g_ref[...], sc_ref[0], sh_ref[0])


def _norm_mod(lay, x, g, sc, sh):
    return pl.pallas_call(
        _norm_mod_kernel,
        grid=(lay.n_tiles,),
        in_specs=[lay.tok_spec(), _full_spec((1, D_MODEL)), lay.mod_spec(), lay.mod_spec()],
        out_specs=lay.tok_spec(),
        out_shape=jax.ShapeDtypeStruct((lay.n, D_MODEL), F32),
        compiler_params=_params("arbitrary"),
        name="norm_mod",
    )(x, g.reshape(1, D_MODEL), sc, sh)


def _final_norm_kernel(x_ref, g_ref, o_ref):
    x = x_ref[...]
    o_ref[...] = x * lax.rsqrt(jnp.mean(x * x, axis=-1, keepdims=True) + RMS_EPS) * g_ref[...]


def _final_norm(lay, x, g):
    return pl.pallas_call(
        _final_norm_kernel,
        grid=(lay.n_tiles,),
        in_specs=[lay.tok_spec(), _full_spec((1, D_MODEL))],
        out_specs=lay.tok_spec(),
        out_shape=jax.ShapeDtypeStruct((lay.n, D_MODEL), F32),
        compiler_params=_params("arbitrary"),
        name="final_norm",
    )(x, g.reshape(1, D_MODEL))


def _conv_in_kernel(x_ref, g_ref, sc_ref, sh_ref, w_ref, bg_ref, u_ref):
    h = _rms_mod(x_ref[...], g_ref[...], sc_ref[0], sh_ref[0]).astype(BF16)
    d = D_MODEL
    bg_ref[...] = _fdot(h, w_ref[:, 0:d])
    u_ref[...] = _fdot(h, w_ref[:, d:2 * d]) * _fdot(h, w_ref[:, 2 * d:3 * d])


def _conv_in(lay, x, g, sc, sh, w_in):
    shp = jax.ShapeDtypeStruct((lay.n, D_MODEL), F32)
    return pl.pallas_call(
        _conv_in_kernel,
        grid=(lay.n_tiles,),
        in_specs=[lay.tok_spec(), _full_spec((1, D_MODEL)), lay.mod_spec(), lay.mod_spec(),
                  _full_spec((D_MODEL, 3 * D_MODEL))],
        out_specs=[lay.tok_spec(), lay.tok_spec()],
        out_shape=[shp, shp],
        compiler_params=_params("arbitrary"),
        name="conv_in",
    )(x, g.reshape(1, D_MODEL), sc, sh, w_in.astype(BF16))


def _conv_out_kernel(bg_ref, u0_ref, u1_ref, u2_ref, ck_ref, w_ref, x_ref, gt_ref, o_ref):
    conv = u0_ref[...] * ck_ref[0:1, :] + u1_ref[...] * ck_ref[1:2, :] + u2_ref[...] * ck_ref[2:3, :]
    y = _bdot(bg_ref[...] * conv, w_ref[...])
    o_ref[...] = x_ref[...] + gt_ref[0] * y


def _conv_out(lay, bg, u0, u1, u2, conv_k, w_out, x, gt):
    return pl.pallas_call(
        _conv_out_kernel,
        grid=(lay.n_tiles,),
        in_specs=[lay.tok_spec()] * 4 + [_full_spec((CONV_WIDTH, D_MODEL)), _full_spec((D_MODEL, D_MODEL)),
                                         lay.tok_spec(), lay.mod_spec()],
        out_specs=lay.tok_spec(),
        out_shape=jax.ShapeDtypeStruct((lay.n, D_MODEL), F32),
        compiler_params=_params("arbitrary"),
        name="conv_out",
    )(bg, u0, u1, u2, conv_k, w_out.astype(BF16), x, gt)


def _conv_fused_kernel(x_ref, g_ref, sc_ref, sh_ref, win_ref, ck_ref, wout_ref, gt_ref, st_ref,
                       o_ref, ns_ref, carry_ref, *, tiles_per_seq):
    x = x_ref[...]
    h = _rms_mod(x, g_ref[...], sc_ref[0], sh_ref[0]).astype(BF16)
    d = D_MODEL
    rows = x.shape[0]
    u = _fdot(h, win_ref[:, d:2 * d]) * _fdot(h, win_ref[:, 2 * d:3 * d])
    @pl.when(pl.program_id(0) == 0)
    def _():
        carry_ref[...] = jnp.zeros_like(carry_ref)

    first = pl.program_id(0) % tiles_per_seq == 0
    prev = jnp.where(first, st_ref[0], carry_ref[...])
    tail = u[rows - (CONV_WIDTH - 1):, :]
    carry_ref[...] = tail
    ns_ref[0] = tail
    row = lax.broadcasted_iota(jnp.int32, u.shape, 0)
    um1 = jnp.where(row == 0, prev[1:2, :], pltpu.roll(u, 1, axis=0))
    um2 = jnp.where(row == 0, prev[0:1, :], jnp.where(row == 1, prev[1:2, :], pltpu.roll(u, 2, axis=0)))
    conv = um2 * ck_ref[0:1, :] + um1 * ck_ref[1:2, :] + u * ck_ref[2:3, :]
    bg = _fdot(h, win_ref[:, 0:d])
    o_ref[...] = x + gt_ref[0] * _bdot(bg * conv, wout_ref[...])


def _conv_fused(lay, x, g, sc, sh, w_in, conv_k, w_out, gt, st):
    assert lay.rows == 1 and CONV_WIDTH == 3
    tps = lay.tiles_per_group
    st_spec = pl.BlockSpec((1, CONV_WIDTH - 1, D_MODEL), lambda i: (i // tps, 0, 0))
    return pl.pallas_call(
        functools.partial(_conv_fused_kernel, tiles_per_seq=tps),
        grid=(lay.n_tiles,),
        in_specs=[lay.tok_spec(), _full_spec((1, D_MODEL)), lay.mod_spec(), lay.mod_spec(),
                  _full_spec((D_MODEL, 3 * D_MODEL)), _full_spec((CONV_WIDTH, D_MODEL)),
                  _full_spec((D_MODEL, D_MODEL)), lay.mod_spec(), st_spec],
        out_specs=[lay.tok_spec(), st_spec],
        out_shape=[jax.ShapeDtypeStruct((lay.n, D_MODEL), F32),
                   jax.ShapeDtypeStruct((lay.batch, CONV_WIDTH - 1, D_MODEL), F32)],
        scratch_shapes=[pltpu.VMEM((CONV_WIDTH - 1, D_MODEL), F32)],
        compiler_params=_params("arbitrary"),
        name="conv_fused",
    )(x, g.reshape(1, D_MODEL), sc, sh, w_in.astype(BF16), conv_k, w_out.astype(BF16), gt, st)


def _mm_res_kernel(a_ref, w_ref, x_ref, gt_ref, o_ref):
    o_ref[...] = x_ref[...] + gt_ref[0] * _bdot(a_ref[...], w_ref[...])


def _mm_res(lay, a, w, x, gt):
    return pl.pallas_call(
        _mm_res_kernel,
        grid=(lay.n_tiles,),
        in_specs=[lay.tok_spec(), _full_spec((D_MODEL, D_MODEL)), lay.tok_spec(), lay.mod_spec()],
        out_specs=lay.tok_spec(),
        out_shape=jax.ShapeDtypeStruct((lay.n, D_MODEL), F32),
        compiler_params=_params("arbitrary"),
        name="mm_res",
    )(a, w.astype(BF16), x, gt)


def _rwkv_proj_kernel(*refs, has_vres, tiles_per_seq):
    refs = list(refs)
    fused = tiles_per_seq is not None
    if fused:
        x_ref, gm_ref, sc_ref, sh_ref, st_ref = refs[:5]
        del refs[:5]
        carry_ref = refs.pop()
        ns_ref = refs.pop()
    else:
        h_ref, hs_ref = refs[:2]
        del refs[:2]
    vf_ref = refs.pop(0) if has_vres else None
    mix_ref, vec_ref, wr_ref, wk_ref, wv_ref, w1_ref, w2_ref, a1_ref, a2_ref, g1_ref, g2_ref = refs[:11]
    del refs[:11]
    if has_vres:
        v1_ref, v2_ref = refs[:2]
        del refs[:2]
    r_ref, km_ref, kk_ref, v_ref, lw_ref, a_ref, g_ref = refs
    if fused:
        @pl.when(pl.program_id(0) == 0)
        def _():
            carry_ref[...] = jnp.zeros_like(carry_ref)

        h = _rms_mod(x_ref[...], gm_ref[...], sc_ref[0], sh_ref[0])
        rows = h.shape[0]
        first = pl.program_id(0) % tiles_per_seq == 0
        prev = jnp.where(first, st_ref[0], carry_ref[0:1, :])
        last = h[rows - 1:rows, :]
        carry_ref[0:1, :] = last
        ns_ref[0] = last
        row = lax.broadcasted_iota(jnp.int32, h.shape, 0)
        hs = jnp.where(row == 0, prev, pltpu.roll(h, 1, axis=0))
    else:
        h = h_ref[...]
        hs = hs_ref[...]
    xx = hs - h
    mixed = lambda i: (h + xx * mix_ref[i:i + 1, :]).astype(BF16)
    w0, a0, v0, k_k, k_a = (vec_ref[i:i + 1, :] for i in range(5))

    r_ref[...] = _fdot(mixed(0), wr_ref[...])
    wl = w0 + _bdot(jnp.tanh(_fdot(mixed(1), w1_ref[...])), w2_ref[...])
    z = -wl
    wlog = -(jnp.maximum(z, 0.0) + jnp.log(1.0 + jnp.exp(-jnp.abs(z)))) - 0.5
    lw_ref[...] = -jnp.exp(wlog)
    k = _fdot(mixed(2), wk_ref[...])
    xv = mixed(3)
    v = _fdot(xv, wv_ref[...])
    if has_vres:
        v = v + (vf_ref[...] - v) * _sigmoid(v0 + _bdot(_fdot(xv, v1_ref[...]), v2_ref[...]))
    v_ref[...] = v
    a = _sigmoid(a0 + _bdot(_fdot(mixed(4), a1_ref[...]), a2_ref[...]))
    a_ref[...] = a
    g_ref[...] = _bdot(_sigmoid(_fdot(mixed(5), g1_ref[...])), g2_ref[...])
    kk_ref[...] = k * k_k
    km_ref[...] = k * (1.0 + (a - 1.0) * k_a)


def _pad_cols(w):
    return jnp.pad(w, ((0, 0), (0, LORA_PAD - w.shape[1]))).astype(BF16)


def _pad_rows(w):
    return jnp.pad(w, ((0, LORA_PAD - w.shape[0]), (0, 0))).astype(BF16)


def _rwkv_proj(lay, front, v_first, mix, vecs, wr, wk, wv, w1, w2, a1, a2, g1, g2, v1, v2):
    has_vres = v_first is not None
    fused = len(front) == 5
    sq = _full_spec((D_MODEL, D_MODEL), single_buffer=True)
    down = _full_spec((D_MODEL, LORA_PAD), single_buffer=True)
    up = _full_spec((LORA_PAD, D_MODEL), single_buffer=True)
    tps = lay.tiles_per_group
    row_spec = pl.BlockSpec((1, 1, D_MODEL), lambda i: (i // tps, 0, 0))
    if fused:
        assert lay.rows == 1
        x, g, sc, sh, st = front
        in_specs = [lay.tok_spec(), _full_spec((1, D_MODEL)), lay.mod_spec(), lay.mod_spec(), row_spec]
        args = [x, g.reshape(1, D_MODEL), sc, sh, st[:, None, :]]
    else:
        in_specs = [lay.tok_spec(), lay.tok_spec()]
        args = list(front)
    in_specs += [lay.tok_spec()] if has_vres else []
    in_specs += [_full_spec((6, D_MODEL)), _full_spec((8, D_MODEL)), sq, sq, sq, down, up, down, up, down, up]
    args += [v_first] if has_vres else []
    args += [mix, vecs, wr.astype(BF16), wk.astype(BF16), wv.astype(BF16), _pad_cols(w1), _pad_rows(w2),
             _pad_cols(a1), _pad_rows(a2), _pad_cols(g1), _pad_rows(g2)]
    if has_vres:
        in_specs += [down, up]
        args += [_pad_cols(v1), _pad_rows(v2)]
    shp = jax.ShapeDtypeStruct((lay.n, D_MODEL), F32)
    outs = pl.pallas_call(
        functools.partial(_rwkv_proj_kernel, has_vres=has_vres, tiles_per_seq=tps if fused else None),
        grid=(lay.n_tiles,),
        in_specs=in_specs,
        out_specs=[lay.tok_spec()] * 7 + ([row_spec] if fused else []),
        out_shape=[shp] * 7 + ([jax.ShapeDtypeStruct((lay.batch, 1, D_MODEL), F32)] if fused else []),
        scratch_shapes=[pltpu.VMEM((SUBLANES, D_MODEL), F32)] if fused else [],
        compiler_params=_params("arbitrary"),
        name="rwkv_proj",
    )(*args)
    return outs if not fused else (*outs[:7], outs[7][:, 0, :])


def _wkv_kernel(r_ref, km_ref, kk_ref, v_ref, lw_ref, a_ref, g_ref, rk_ref, lnw_ref, lnb_ref, s0_ref,
                y_ref, s_ref, st_ref, *, chunk, heads, seqs):
    t = pl.program_id(2)
    L = chunk

    pairs = heads // 2
    pw = 2 * HEAD_SIZE
    lo, hi = slice(0, HEAD_SIZE), slice(HEAD_SIZE, pw)
    chains = [(q, p) for q in range(seqs) for p in range(pairs)]

    @pl.when(t == 0)
    def _():
        st_ref[...] = jnp.zeros_like(st_ref)
        for n, (q, p) in enumerate(chains):
            st_ref[n, lo, lo] = s0_ref[q, 2 * p]
            st_ref[n, hi, hi] = s0_ref[q, 2 * p + 1]

    row = lax.broadcasted_iota(jnp.int32, (L, 2 * L), 0)
    col = lax.broadcasted_iota(jnp.int32, (L, 2 * L), 1)
    col = jnp.where(col >= L, col - L, col)
    strict2 = row > col
    incl2 = row >= col
    tri = incl2[:, :L].astype(F32)
    even = lax.broadcasted_iota(jnp.int32, (1, pw), 1) < HEAD_SIZE
    same_head = ((lax.broadcasted_iota(jnp.int32, (pw, pw), 0) < HEAD_SIZE)
                 == (lax.broadcasted_iota(jnp.int32, (pw, pw), 1) < HEAD_SIZE))
    ones_bd = same_head.astype(BF16)

    def head_sum(x):
        return _fdot(x.astype(BF16), ones_bd)

    pick = lambda x0, x1: jnp.where(even, x0, x1)

    each = lambda f, *cols: [f(*xs) for xs in zip(*cols)]
    sls = [slice(p * pw, (p + 1) * pw) for _, p in chains]
    r, km, kkr, v, lw, asig, g = ([ref[q, :, slice(p * pw, (p + 1) * pw)] for q, p in chains]
                                  for ref in (r_ref, km_ref, kk_ref, v_ref, lw_ref, a_ref, g_ref))
    s0 = [st_ref[n] for n in range(len(chains))]
    c = each(lambda x: _fdot(tri, x), lw)
    kk = each(lambda x: x * lax.rsqrt(jnp.maximum(head_sum(x * x), 1e-24)), kkr)
    b = each(lambda x, y: x * y, kk, asig)
    c_last = each(lambda x: x[L - 1:L, :], c)
    ginv = each(lambda x: jnp.exp(-x), c)
    lhs = each(lambda kk_, c_, lw_, r_: jnp.concatenate([-kk_ * jnp.exp(c_ - lw_), r_ * jnp.exp(c_)], axis=0),
               kk, c, lw, r)
    rhs = each(lambda b_, km_, gi: jnp.concatenate([b_ * gi, km_ * gi], axis=0), b, km, ginv)
    qk0 = each(lambda l_, r_: _fdot_nt(jnp.where(even, l_, 0.0), r_), lhs, rhs)
    qk1 = each(lambda l_, r_: _fdot_nt(jnp.where(even, 0.0, l_), r_), lhs, rhs)
    hs0 = each(_fdot_nt, lhs, s0)
    zv = each(lambda v_: jnp.concatenate([jnp.zeros_like(v_), v_], axis=0), v)
    m0 = each(lambda q: jnp.where(strict2, q[:L, :], 0.0), qk0)
    m1 = each(lambda q: jnp.where(strict2, q[:L, :], 0.0), qk1)
    u = each(lambda h_, a0, a1, z_: h_[:L] + pick(_fdot(a0, z_), _fdot(a1, z_)), hs0, m0, m1, zv)
    p0 = each(lambda m: m[:, :L], m0)
    p1 = each(lambda m: m[:, :L], m1)
    for it in range(int(math.log2(L))):
        if it > 0:
            p0 = each(lambda x: _fdot(x, x), p0)
            p1 = each(lambda x: _fdot(x, x), p1)
        u = each(lambda u_, a0, a1: u_ + pick(_fdot(a0, u_), _fdot(a1, u_)), u, p0, p1)
    uv = each(lambda u_, v_: jnp.concatenate([u_, v_], axis=0), u, v)
    bk = each(lambda cl, c_, b_, km_: jnp.concatenate([b_, km_], axis=0)
              * jnp.exp(cl - jnp.concatenate([c_, c_], axis=0)), c_last, c, b, km)
    s_new = each(lambda s_, cl, uv_, bk_: jnp.where(same_head, s_ * jnp.exp(cl) + _fdot_tn(uv_, bk_), 0.0),
                 s0, c_last, uv, bk)
    for n in range(len(chains)):
        st_ref[n] = s_new[n]
    y = each(lambda h_, q0, q1, uv_: h_[L:] + pick(_fdot(jnp.where(incl2, q0[L:, :], 0.0), uv_),
                                                   _fdot(jnp.where(incl2, q1[L:, :], 0.0), uv_)),
             hs0, qk0, qk1, uv)

    def finish(y_, r_, km_, v_, g_, sl):
        yc = y_ - head_sum(y_) * (1.0 / HEAD_SIZE)
        var = head_sum(yc * yc) * (1.0 / HEAD_SIZE)
        yn = yc * lax.rsqrt(var + GN_EPS) * lnw_ref[:, sl] + lnb_ref[:, sl]
        bonus = head_sum(r_ * km_ * rk_ref[:, sl]) * v_
        return (yn + bonus) * g_

    for (q, p), out in zip(chains, each(finish, y, r, km, v, g, sls)):
        y_ref[q, :, slice(p * pw, (p + 1) * pw)] = out

    @pl.when(t == pl.num_programs(2) - 1)
    def _():
        for n, (q, p) in enumerate(chains):
            s_ref[q, 2 * p] = st_ref[n, lo, lo]
            s_ref[q, 2 * p + 1] = st_ref[n, hi, hi]


def _wkv(batch, seq, chunk, heads, seqs, r, km, kk, v, lw, a, g, r_k, lnw, lnb, s0):
    n_t = seq // chunk
    width = heads * HEAD_SIZE
    tok = pl.BlockSpec((seqs, chunk, width), lambda b, h, t: (b, t, h))
    vec = pl.BlockSpec((1, width), lambda b, h, t: (0, h))
    st = pl.BlockSpec((seqs, heads, HEAD_SIZE, HEAD_SIZE), lambda b, h, t: (b, h, 0, 0))
    as3d = lambda x: x.reshape(batch, seq, D_MODEL)
    y, s_new = pl.pallas_call(
        functools.partial(_wkv_kernel, chunk=chunk, heads=heads, seqs=seqs),
        grid=(batch // seqs, N_HEADS // heads, n_t),
        in_specs=[tok] * 7 + [vec] * 3 + [st],
        out_specs=[tok, st],
        out_shape=[jax.ShapeDtypeStruct((batch, seq, D_MODEL), F32),
                   jax.ShapeDtypeStruct((batch, N_HEADS, HEAD_SIZE, HEAD_SIZE), F32)],
        scratch_shapes=[pltpu.VMEM((seqs * heads // 2, 2 * HEAD_SIZE, 2 * HEAD_SIZE), F32)],
        compiler_params=_params("arbitrary", "arbitrary", "arbitrary"),
        name="wkv",
    )(*(as3d(x) for x in (r, km, kk, v, lw, a, g)),
      r_k.reshape(1, D_MODEL), lnw.reshape(1, D_MODEL), lnb.reshape(1, D_MODEL), s0)
    return y.reshape(batch * seq, D_MODEL), s_new


def _col_reduce(x3, op):
    m = x3[0]
    for r in range(1, x3.shape[0]):
        m = op(m, x3[r])
    for shift in (4, 2, 1):
        m = op(m, pltpu.roll(m, shift, axis=0))
    return m


def _top16(s3, want_rank):
    work = s3
    rank = jnp.full(s3.shape, float(PEER_TOPK), F32) if want_rank else None
    vals = []
    for k in range(PEER_TOPK):
        m = _col_reduce(work, jnp.maximum)
        hit = work == m[None]
        if want_rank:
            rank = jnp.where(hit, float(k), rank)
        work = jnp.where(hit, -jnp.inf, work)
        vals.append(m)
    return vals, work, rank


def _sorting_network(n):
    pairs = []
    p = 1
    while p < n:
        k = p
        while k >= 1:
            for j in range(k % p, n - k, 2 * k):
                for i in range(min(k, n - j - k)):
                    if (i + j) // (2 * p) == (i + j + k) // (2 * p):
                        pairs.append((i + j, i + j + k))
            k //= 2
        p *= 2
    return pairs


def _top16_values(s3):
    cols = [s3[r] for r in range(s3.shape[0])]
    for i, j in _sorting_network(len(cols)):
        cols[i], cols[j] = jnp.maximum(cols[i], cols[j]), jnp.minimum(cols[i], cols[j])
    vals = []
    for k in range(PEER_TOPK):
        m = _col_reduce(cols[0][None], jnp.maximum)
        vals.append(m)
        if k + 1 < PEER_TOPK:
            hit = cols[0] == m
            for d in range(PEER_TOPK - 1 - k):
                cols[d] = jnp.where(hit, cols[d + 1], cols[d])
    return vals


def _rows_from(vals):
    row = lax.broadcasted_iota(jnp.int32, vals[0].shape, 0)
    out = vals[0]
    for s in range(1, SUBLANES):
        out = jnp.where(row == s, vals[s], out)
    return out


def _peer_select_chunk(s1, s2):
    n_keys, w = s1.shape
    s1 = s1.reshape(n_keys // SUBLANES, SUBLANES, w)
    s2 = s2.reshape(n_keys // SUBLANES, SUBLANES, w)
    v1 = _top16_values(s1)
    v2, _, rank2 = _top16(s2, True)
    v2_lo, v2_hi, v1_hi = _rows_from(v2[:SUBLANES]), _rows_from(v2[SUBLANES:]), _rows_from(v1[SUBLANES:])
    cand = jnp.stack([v1[0] + v2_lo, v1[0] + v2_hi] + [v1[a] + v2_lo for a in range(1, SUBLANES)]
                     + [v1_hi + v2[0]])
    work = cand
    tau = None
    for _ in range(PEER_TOPK):
        tau = _col_reduce(work, jnp.maximum)
        work = jnp.where(work == tau[None], -jnp.inf, work)
    top = v1[0] + v2[0]
    kept = cand >= tau[None]
    zsum = _col_reduce(jnp.where(kept, jnp.exp(cand - top[None]), 0.0), jnp.add)
    few = PEER_TOPK // 4
    cnt = jnp.zeros(s1.shape, F32)
    for b in range(few):
        cnt = cnt + jnp.where(s1 + v2[b][None] >= tau[None], 1.0, 0.0)
    upper = lax.broadcasted_iota(jnp.int32, cand.shape[1:], 0) >= few
    ones = lambda m: jnp.where(m, 1.0, 0.0)
    extra = [_col_reduce((ones(kept[0] & upper) + ones(kept[1]))[None], jnp.add),
             _col_reduce(ones(kept[2] & upper)[None], jnp.add),
             _col_reduce(ones(kept[3] & upper)[None], jnp.add)]
    for a, x in enumerate(extra):
        cnt = jnp.where(s1 == v1[a][None], cnt + x[None], cnt)
    cnt = jnp.where(s1 >= v1[PEER_TOPK - 1][None], cnt, 0.0)
    e1 = jnp.exp(s1 - v1[0][None])
    e2 = jnp.exp(s2 - v2[0][None]) / zsum[None]
    return tuple(t.reshape(n_keys, w) for t in (cnt, e1, rank2, e2))


def _peer_select_kernel(x_ref, g_ref, sc_ref, sh_ref, wq_ref, k1_ref, k2_ref,
                        ht_ref, c_ref, e1_ref, rk_ref, e2_ref, q_scr):
    h = _rms_mod(x_ref[...], g_ref[...], sc_ref[0], sh_ref[0])
    ht = h.T.astype(BF16)
    ht_ref[...] = ht
    q_scr[...] = _fdot(wq_ref[...], ht)
    half = D_QUERY // 2
    n_chunks = x_ref.shape[0] // PEER_LANE_CHUNK

    def head_body(hd, carry):
        q1 = q_scr[pl.ds(pl.multiple_of(hd * D_QUERY, D_QUERY), half), :]
        q2 = q_scr[pl.ds(pl.multiple_of(hd * D_QUERY + half, half), half), :]
        s1 = _bdot(k1_ref[hd], q1)
        s2 = _bdot(k2_ref[hd], q2)
        for ch in range(n_chunks):
            ls = slice(ch * PEER_LANE_CHUNK, (ch + 1) * PEER_LANE_CHUNK)
            cnt, e1, rank2, e2 = _peer_select_chunk(s1[:, ls], s2[:, ls])
            c_ref[hd, :, ls] = cnt
            e1_ref[hd, :, ls] = e1
            rk_ref[hd, :, ls] = rank2.astype(BF16)
            e2_ref[hd, :, ls] = e2.astype(BF16)
        return carry

    lax.fori_loop(0, PEER_HEADS, head_body, 0)


def _peer_select(lay, x, g, sc, sh, wq_t, k1, k2):
    tt = lay.tile
    sel_spec = pl.BlockSpec((PEER_HEADS, N_KEYS, tt), lambda i: (0, 0, i))
    sel_shape = lambda dt: jax.ShapeDtypeStruct((PEER_HEADS, N_KEYS, lay.n), dt)
    return pl.pallas_call(
        _peer_select_kernel,
        grid=(lay.n_tiles,),
        in_specs=[lay.tok_spec(), _full_spec((1, D_MODEL)), lay.mod_spec(), lay.mod_spec(),
                  _full_spec((PEER_HEADS * D_QUERY, D_MODEL)),
                  _full_spec((PEER_HEADS, N_KEYS, D_QUERY // 2)), _full_spec((PEER_HEADS, N_KEYS, D_QUERY // 2))],
        out_specs=[pl.BlockSpec((D_MODEL, tt), lambda i: (0, i))] + [sel_spec] * 4,
        out_shape=[jax.ShapeDtypeStruct((D_MODEL, lay.n), BF16),
                   sel_shape(F32), sel_shape(F32), sel_shape(BF16), sel_shape(BF16)],
        scratch_shapes=[pltpu.VMEM((PEER_HEADS * D_QUERY, tt), F32)],
        compiler_params=_params("arbitrary"),
        name="peer_select",
    )(x, g.reshape(1, D_MODEL), sc, sh, wq_t, k1, k2)


def _peer_dense_kernel(*refs):
    groups = PEER_EXPERT_TILE // PEER_GROUP
    ht_ref, c_ref, e1_ref, rk_ref, e2_ref = refs[:5]
    u_refs = refs[5:5 + groups]
    vt_refs = refs[5 + groups:5 + 2 * groups]
    x_ref, gt_ref, o_ref, acc_ref, coef_ref, z_ref = refs[5 + 2 * groups:]
    e = pl.program_id(1)
    sr = PEER_STEP_ROWS
    spg = PEER_GROUP // (N_KEYS * sr)
    piece = D_MODEL // spg
    last_slot = (groups - 1) % 2

    @pl.when(e == 0)
    def _():
        acc_ref[...] = jnp.zeros_like(acc_ref)

    def pre_act(step):
        g, ii = divmod(step, spg)
        return _fdot(u_refs[g][0, ii * sr * N_KEYS:(ii + 1) * sr * N_KEYS, :], ht_ref[...])

    ahead = PEER_Z_AHEAD
    for step in range(ahead):
        z_ref[step] = pre_act(step)
    for g in range(groups):
        slot = g % 2
        for ii in range(spg):
            step = g * spg + ii
            if step + ahead < groups * spg:
                z_ref[(step + ahead) % (ahead + 1)] = pre_act(step + ahead)
            if g > 0:
                ps = slice(ii * piece, (ii + 1) * piece)
                acc_ref[ps, :] += _fdot(vt_refs[g - 1][0, ps, :], coef_ref[1 - slot])
            for rr in range(sr):
                row = step * sr + rr
                gate = None
                for hd in range(PEER_HEADS):
                    cnt = c_ref[hd, row:row + 1, :].astype(BF16)
                    e1 = e1_ref[hd, row:row + 1, :].astype(BF16)
                    term = jnp.where(rk_ref[hd] < cnt, e2_ref[hd], jnp.zeros((), BF16)) * e1
                    gate = term if gate is None else gate + term
                z = z_ref[step % (ahead + 1), rr * N_KEYS:(rr + 1) * N_KEYS, :]
                act = 0.5 * z * (1.0 + lax.erf(z * (1.0 / math.sqrt(2.0))))
                lo = (ii * sr + rr) * N_KEYS
                coef_ref[slot, lo:lo + N_KEYS, :] = gate * act.astype(BF16)

    acc_ref[...] += _fdot(vt_refs[groups - 1][0], coef_ref[last_slot])

    @pl.when(e == pl.num_programs(1) - 1)
    def _():
        o_ref[...] = x_ref[...] + gt_ref[0] * acc_ref[...].T


def _peer_weight_parts(peer_u, peer_v):
    groups = PEER_EXPERT_TILE // PEER_GROUP
    tiles = N_EXPERTS // PEER_EXPERT_TILE
    u5 = peer_u.astype(BF16).reshape(DEPTH, tiles, groups, PEER_GROUP, D_MODEL)
    v5 = jnp.swapaxes(peer_v.astype(BF16).reshape(DEPTH, tiles, groups, PEER_GROUP, D_MODEL), 3, 4)
    u_parts = [u5[:, :, k] for k in range(groups)]
    vt_parts = [v5[:, :, k] for k in range(groups)]
    return u_parts, vt_parts


def _peer_dense(lay, ht, cnt, e1, rank2, e2, u_parts, vt_parts, x, gt):
    tt = lay.tile
    groups = PEER_EXPERT_TILE // PEER_GROUP
    rows_per_tile = PEER_EXPERT_TILE // N_KEYS
    tpg = lay.tiles_per_group
    row_spec = pl.BlockSpec((PEER_HEADS, rows_per_tile, tt), lambda i, e: (0, e, i))
    key_spec = pl.BlockSpec((PEER_HEADS, N_KEYS, tt), lambda i, e: (0, 0, i))
    u_spec = pl.BlockSpec((1, PEER_GROUP, D_MODEL), lambda i, e: (e, 0, 0))
    vt_spec = pl.BlockSpec((1, D_MODEL, PEER_GROUP), lambda i, e: (e, 0, 0))
    return pl.pallas_call(
        _peer_dense_kernel,
        grid=(lay.n_tiles, N_EXPERTS // PEER_EXPERT_TILE),
        in_specs=[pl.BlockSpec((D_MODEL, tt), lambda i, e: (0, i)), row_spec, row_spec, key_spec, key_spec]
        + [u_spec] * groups + [vt_spec] * groups + [
            pl.BlockSpec((tt, D_MODEL), lambda i, e: (i, 0)),
            pl.BlockSpec((1, lay.rows, D_MODEL), lambda i, e: (i // tpg, 0, 0))],
        out_specs=pl.BlockSpec((tt, D_MODEL), lambda i, e: (i, 0)),
        out_shape=jax.ShapeDtypeStruct((lay.n, D_MODEL), F32),
        scratch_shapes=[pltpu.VMEM((D_MODEL, tt), F32), pltpu.VMEM((2, PEER_GROUP, tt), BF16),
                        pltpu.VMEM((PEER_Z_AHEAD + 1, PEER_STEP_ROWS * N_KEYS, tt), F32)],
        compiler_params=_params("arbitrary", "arbitrary"),
        name="peer_dense",
    )(ht, cnt, e1, rank2, e2, *u_parts, *vt_parts, x, gt)


def _run(batch, seq, x, m_all, st_conv, st_shift, st_wkv, p):
    lay = _Layout(batch, seq)
    x = x.reshape(lay.n, D_MODEL)
    new_conv, new_shift, new_wkv = [], [], []
    v_first = None
    short = seq < WKV_CHUNK
    for i in range(DEPTH):
        sh_m, sc_m, gt_m, sh_f, sc_f, gt_f = (lay.expand(t) for t in jnp.split(m_all[i], 6, axis=-1))
        j = i // 2
        if i % 2 == 0 and lay.rows == 1:
            x, cbuf = _conv_fused(lay, x, p["g_mix"][i], sc_m, sh_m, p["conv_w_in"][j], p["conv_k"][j],
                                  p["conv_w_out"][j], gt_m, st_conv[j])
            new_conv.append(cbuf)
        elif i % 2 == 0:
            bg, u = _conv_in(lay, x, p["g_mix"][i], sc_m, sh_m, p["conv_w_in"][j])
            up = jnp.concatenate([st_conv[j], u.reshape(batch, seq, D_MODEL)], axis=1)
            u0 = up[:, 0:seq].reshape(lay.n, D_MODEL)
            u1 = up[:, 1:seq + 1].reshape(lay.n, D_MODEL)
            new_conv.append(up[:, -(CONV_WIDTH - 1):])
            x = _conv_out(lay, bg, u0, u1, u, p["conv_k"][j], p["conv_w_out"][j], x, gt_m)
        else:
            lay_p = _Layout(batch, seq, RWKV_PROJ_TILE)
            if lay_p.rows == 1:
                front = (x, p["g_mix"][i], lay_p.expand(m_all[i][:, D_MODEL:2 * D_MODEL]),
                         lay_p.expand(m_all[i][:, 0:D_MODEL]), st_shift[j])
            else:
                h = _norm_mod(lay, x, p["g_mix"][i], sc_m, sh_m)
                h3 = h.reshape(batch, seq, D_MODEL)
                hs = jnp.concatenate([st_shift[j][:, None, :], h3[:, :-1]], axis=1).reshape(lay.n, D_MODEL)
                new_shift.append(h3[:, -1])
                front = (h, hs)
            zero = jnp.zeros((D_MODEL,), F32)
            v0 = p["rwkv_v0"][j - 1] if j > 0 else zero
            vecs = jnp.stack([p["rwkv_w0"][j], p["rwkv_a0"][j], v0, p["rwkv_k_k"][j], p["rwkv_k_a"][j],
                              zero, zero, zero])
            v1 = p["rwkv_v1"][j - 1] if j > 0 else None
            v2 = p["rwkv_v2"][j - 1] if j > 0 else None
            r, km, kk, v, lw, a, g, *shift_out = _rwkv_proj(
                lay_p, front, v_first if j > 0 else None, p["rwkv_mix"][j], vecs,
                p["rwkv_wr"][j], p["rwkv_wk"][j], p["rwkv_wv"][j], p["rwkv_w1"][j], p["rwkv_w2"][j],
                p["rwkv_a1"][j], p["rwkv_a2"][j], p["rwkv_g1"][j], p["rwkv_g2"][j], v1, v2)
            new_shift.extend(shift_out)
            if j == 0:
                v_first = v
            scan_in = (r, km, kk, v, lw, a, g)
            if short:
                pad = lambda t: jnp.pad(t.reshape(batch, seq, D_MODEL),
                                        ((0, 0), (0, WKV_SHORT_CHUNK - seq), (0, 0))).reshape(-1, D_MODEL)
                scan_in = tuple(pad(t) for t in scan_in)
                yg, s_new = _wkv(batch, WKV_SHORT_CHUNK, WKV_SHORT_CHUNK, N_HEADS, WKV_SHORT_SEQS_PER_STEP, *scan_in,
                                 p["rwkv_r_k"][j], p["rwkv_lnw"][j], p["rwkv_lnb"][j], st_wkv[j])
                yg = yg.reshape(batch, WKV_SHORT_CHUNK, D_MODEL)[:, :seq].reshape(lay.n, D_MODEL)
            else:
                yg, s_new = _wkv(batch, seq, WKV_CHUNK, N_HEADS, WKV_SEQS_PER_STEP, *scan_in,
                                 p["rwkv_r_k"][j], p["rwkv_lnw"][j], p["rwkv_lnb"][j], st_wkv[j])
            new_wkv.append(s_new)
            x = _mm_res(lay, yg, p["rwkv_wo"][j], x, gt_m)
        ht, cnt, e1, rank2, e2 = _peer_select(lay, x, p["g_ffn"][i], sc_f, sh_f,
                                              p["wq_t"][i], p["k1_b"][i], p["k2_b"][i])
        x = _peer_dense(lay, ht, cnt, e1, rank2, e2, [t[i] for t in p["u_parts"]],
                        [t[i] for t in p["vt_parts"]], x, gt_f)
    y = _final_norm(lay, x, p["g_final"]).reshape(batch, seq, D_MODEL)
    return y, jnp.stack(new_conv), jnp.stack(new_shift), jnp.stack(new_wkv)


def kernel(x_prompt, x_sample, state_conv, state_shift, state_wkv, c_prompt, c_sample, w_ada, b_ada, g_mix, g_ffn, g_final, conv_w_in, conv_k, conv_w_out, rwkv_mix, rwkv_wr, rwkv_wk, rwkv_wv, rwkv_wo, rwkv_w0, rwkv_w1, rwkv_w2, rwkv_a0, rwkv_a1, rwkv_a2, rwkv_v0, rwkv_v1, rwkv_v2, rwkv_g1, rwkv_g2, rwkv_k_k, rwkv_k_a, rwkv_r_k, rwkv_lnw, rwkv_lnb, peer_wq, peer_k1, peer_k2, peer_u, peer_v):
    p = dict(
        g_mix=g_mix, g_ffn=g_ffn, g_final=g_final, conv_w_in=conv_w_in, conv_k=conv_k, conv_w_out=conv_w_out,
        rwkv_mix=rwkv_mix, rwkv_wr=rwkv_wr, rwkv_wk=rwkv_wk, rwkv_wv=rwkv_wv, rwkv_wo=rwkv_wo,
        rwkv_w0=rwkv_w0, rwkv_w1=rwkv_w1, rwkv_w2=rwkv_w2, rwkv_a0=rwkv_a0, rwkv_a1=rwkv_a1, rwkv_a2=rwkv_a2,
        rwkv_v0=rwkv_v0, rwkv_v1=rwkv_v1, rwkv_v2=rwkv_v2, rwkv_g1=rwkv_g1, rwkv_g2=rwkv_g2,
        rwkv_k_k=rwkv_k_k, rwkv_k_a=rwkv_k_a, rwkv_r_k=rwkv_r_k, rwkv_lnw=rwkv_lnw, rwkv_lnb=rwkv_lnb,
        wq_t=jnp.swapaxes(peer_wq, 1, 2).astype(BF16),
        k1_b=peer_k1.astype(BF16), k2_b=peer_k2.astype(BF16),
        **dict(zip(("u_parts", "vt_parts"), _peer_weight_parts(peer_u, peer_v))),
    )
    n_prompt, seq_prompt = x_prompt.shape[0], x_prompt.shape[1]
    n_sample, seq_sample = x_sample.shape[0], x_sample.shape[1]
    m_all = _adaln(jnp.concatenate([c_prompt, c_sample], axis=0), w_ada, b_ada)
    zeros = lambda *s: jnp.zeros(s, F32)
    n_conv, n_rwkv = (DEPTH + 1) // 2, DEPTH // 2
    y_p, p_conv, p_shift, p_wkv = _run(
        n_prompt, seq_prompt, x_prompt, m_all[:, :n_prompt],
        zeros(n_conv, n_prompt, CONV_WIDTH - 1, D_MODEL), zeros(n_rwkv, n_prompt, D_MODEL),
        zeros(n_rwkv, n_prompt, N_HEADS, HEAD_SIZE, HEAD_SIZE), p)
    y_s, s_conv, s_shift, s_wkv = _run(
        n_sample, seq_sample, x_sample, m_all[:, n_prompt:], state_conv, state_shift, state_wkv, p)
    return (y_p, y_s, p_conv, p_shift, p_wkv, s_conv, s_shift, s_wkv)
```
